```python
import math
import jax, jax.numpy as jnp
from jax import lax
import numpy as np

D_MODEL = 2048
BATCH = 1
SEQ = 8192
DEPTH = 1

D_MIX = D_MODEL
D_ATTN = D_MIX // 2
D_CONV = D_MIX - D_ATTN
HEAD_DIM = 128
N_HEADS = D_ATTN // HEAD_DIM
CONV_GROUPS = 8
MOBA_BLOCK = 256
MOBA_TOPK = 3
QUERY_CHUNK = 64
CONV_WIDTH = 31
N_EXPERTS = 32
TOP_K = 4
D_FF = D_MODEL
SWIGLU_ALPHA = 1.702
SWIGLU_LIMIT = 7.0
ROUTE_BLOCK = 128
LN_EPS = 1e-5
DEEPNORM_ALPHA = (2.0 * DEPTH) ** 0.25
DEEPNORM_BETA = (8.0 * DEPTH) ** -0.25
IN_COLS = 3 * D_ATTN + 2 * D_CONV

kernel_name = "hybrid_moba_conformer_moe_deepnorm"


def layer_norm(x, g, b):
    xf = x.astype(jnp.float32)
    mu = jnp.mean(xf, axis=-1, keepdims=True)
    var = jnp.mean(jnp.square(xf - mu), axis=-1, keepdims=True)
    y = (xf - mu) * lax.rsqrt(var + LN_EPS) * g.astype(jnp.float32) + b.astype(jnp.float32)
    return y.astype(x.dtype)


def alibi_slopes(n_heads):
    return 2.0 ** (-(8.0 / n_heads) * jnp.arange(1, n_heads + 1, dtype=jnp.float32))


def moba_attention(q, k, v):
    B, S, H, hd = q.shape
    blk = MOBA_BLOCK
    nb = -(-S // blk)
    s_pad = nb * blk
    pad = ((0, 0), (0, s_pad - S), (0, 0), (0, 0))
    kb = jnp.pad(k, pad).reshape(B, nb, blk, H, hd).transpose(0, 3, 1, 2, 4)
    vb = jnp.pad(v, pad).reshape(B, nb, blk, H, hd).transpose(0, 3, 1, 2, 4)
    k_mean = jnp.mean(kb.astype(jnp.float32), axis=3)

    q_blk = jnp.arange(S) // blk
    gate = jnp.einsum('bshd,bhnd->bhsn', q.astype(jnp.float32), k_mean)
    fully_past = jnp.arange(nb)[None, :] < q_blk[:, None]
    gate = jnp.where(fully_past[None, None], gate, -jnp.inf)
    n_sel = min(MOBA_TOPK, nb)
    sel_score, sel = lax.top_k(gate, n_sel)
    sel_valid = sel_score > -jnp.inf

    qh = q.transpose(0, 2, 1, 3)
    slopes = alibi_slopes(H)
    scale = hd ** -0.5
    b_ix = jnp.arange(B)[:, None, None, None]
    h_ix = jnp.arange(H)[None, :, None, None]
    qc = QUERY_CHUNK

    def chunk(c):
        t0 = c * qc
        q_c = lax.dynamic_slice_in_dim(qh, t0, qc, axis=2)
        sel_c = lax.dynamic_slice_in_dim(sel, t0, qc, axis=2)
        val_c = lax.dynamic_slice_in_dim(sel_valid, t0, qc, axis=2)
        t = t0 + jnp.arange(qc)
        k_sel = kb[b_ix, h_ix, sel_c]
        v_sel = vb[b_ix, h_ix, sel_c]
        s_sel = jnp.einsum('bhqd,bhqnkd->bhqnk', q_c, k_sel).astype(jnp.float32) * scale
        pos_sel = sel_c[..., None] * blk + jnp.arange(blk)
        dist_sel = (t[None, None, :, None, None] - pos_sel).astype(jnp.float32)
        s_sel = s_sel - slopes[None, :, None, None, None] * dist_sel
        s_sel = jnp.where(val_c[..., None], s_sel, -jnp.inf)
        own = t0 // blk
        k_own = lax.dynamic_index_in_dim(kb, own, axis=2, keepdims=False)
        v_own = lax.dynamic_index_in_dim(vb, own, axis=2, keepdims=False)
        s_own = jnp.einsum('bhqd,bhkd->bhqk', q_c, k_own).astype(jnp.float32) * scale
        dist_own = t[:, None] - (own * blk + jnp.arange(blk))[None, :]
        s_own = jnp.where(dist_own[None, None] >= 0,
                          s_own - slopes[None, :, None, None] * dist_own.astype(jnp.float32)[None, None],
                          -jnp.inf)
        scores = jnp.concatenate([s_sel.reshape(B, H, qc, n_sel * blk), s_own], axis=-1)
        p = jax.nn.softmax(scores, axis=-1).astype(v.dtype)
        p_sel = p[..., :n_sel * blk].reshape(B, H, qc, n_sel, blk)
        p_own = p[..., n_sel * blk:]
        return (jnp.einsum('bhqnk,bhqnkd->bhqd', p_sel, v_sel)
                + jnp.einsum('bhqk,bhkd->bhqd', p_own, v_own))

    outs = lax.map(chunk, jnp.arange(S // qc))
    return outs.transpose(1, 0, 3, 2, 4).reshape(B, S, H * hd)


def conformer_conv(ga, gb, conv_w, conv_b, conv_ln_g, conv_ln_b):
    u = ga * jax.nn.sigmoid(gb)
    c = lax.conv_general_dilated(
        u, conv_w[:, None, :], window_strides=(1,), padding=[(CONV_WIDTH - 1, 0)],
        dimension_numbers=('NWC', 'WIO', 'NWC'), feature_group_count=u.shape[-1]) + conv_b
    c = layer_norm(c, conv_ln_g, conv_ln_b)
    return c * jax.nn.sigmoid(c)


def moe_ffn(h, w_router, b_router, w_gate_up, b_gate_up, w_down, b_down):
    B, S, D = h.shape
    T = B * S
    xt = h.reshape(T, D)
    logits = (xt @ w_router + b_router).astype(jnp.float32)
    top_val, top_idx = lax.top_k(logits, TOP_K)
    gates = jax.nn.softmax(top_val, axis=-1)
    A = T * TOP_K
    flat_e = top_idx.reshape(A)
    flat_tok = jnp.arange(A, dtype=jnp.int32) // TOP_K
    flat_g = gates.reshape(A)
    order = jnp.argsort(flat_e)
    e_sorted = flat_e[order]
    counts = jnp.bincount(flat_e, length=N_EXPERTS)
    starts = jnp.cumsum(counts) - counts
    padded = ((counts + ROUTE_BLOCK - 1) // ROUTE_BLOCK) * ROUTE_BLOCK
    pad_end = jnp.cumsum(padded)
    pad_start = pad_end - padded
    dest = pad_start[e_sorted] + (jnp.arange(A) - starts[e_sorted])
    n_blocks = -(-A // ROUTE_BLOCK) + N_EXPERTS
    cap = n_blocks * ROUTE_BLOCK
    row_tok = jnp.zeros((cap,), jnp.int32).at[dest].set(flat_tok[order])
    row_gate = jnp.zeros((cap,), h.dtype).at[dest].set(flat_g[order].astype(h.dtype))
    block_e = jnp.minimum(
        jnp.searchsorted(pad_end, jnp.arange(n_blocks) * ROUTE_BLOCK, side='right'),
        N_EXPERTS - 1)

    def run_block(args):
        tok, e = args
        xb = xt[tok]
        gu = xb @ w_gate_up[e] + b_gate_up[e]
        g, u = gu[:, :D_FF], gu[:, D_FF:]
        g = jnp.minimum(g, SWIGLU_LIMIT)
        u = jnp.clip(u, -SWIGLU_LIMIT, SWIGLU_LIMIT)
        act = (u + 1.0) * (g * jax.nn.sigmoid(SWIGLU_ALPHA * g))
        return act @ w_down[e] + b_down[e]

    y_rows = lax.map(run_block, (row_tok.reshape(n_blocks, ROUTE_BLOCK), block_e))
    y = jax.ops.segment_sum(y_rows.reshape(cap, D) * row_gate[:, None], row_tok, num_segments=T)
    return y.reshape(B, S, D)


def setup_inputs(seed: int = 0) -> dict:
    key = jax.random.key(seed)
    ks = jax.random.split(key, 24)
    f32 = jnp.float32
    nrm = lambda k, shape, s: jax.random.normal(k, shape, f32) * s
    d_in = D_MODEL ** -0.5
    w_q = nrm(ks[1], (D_MODEL, D_ATTN), d_in)
    w_k = nrm(ks[2], (D_MODEL, D_ATTN), d_in)
    w_v = nrm(ks[3], (D_MODEL, D_ATTN), d_in * DEEPNORM_BETA)
    w_ga = nrm(ks[4], (D_MODEL, D_CONV), d_in)
    w_gb = nrm(ks[5], (D_MODEL, D_CONV), d_in)
    return {
        "x": jax.random.normal(ks[0], (BATCH, SEQ, D_MODEL), f32),
        "w_in": jnp.concatenate([w_q, w_k, w_v, w_ga, w_gb], axis=1),
        "conv_w": nrm(ks[6], (CONV_WIDTH, D_CONV), CONV_WIDTH ** -0.5),
        "conv_b": nrm(ks[7], (D_CONV,), 0.01),
        "conv_ln_g": 1.0 + nrm(ks[8], (D_CONV,), 0.01),
        "conv_ln_b": nrm(ks[9], (D_CONV,), 0.01),
        "w_out": nrm(ks[10], (D_MIX, D_MODEL), D_MIX ** -0.5 * DEEPNORM_BETA),
        "ln1_g": 1.0 + nrm(ks[11], (D_MODEL,), 0.01),
        "ln1_b": nrm(ks[12], (D_MODEL,), 0.01),
        "w_router": nrm(ks[13], (D_MODEL, N_EXPERTS), d_in),
        "b_router": nrm(ks[14], (N_EXPERTS,), 0.01),
        "w_gate_up": nrm(ks[15], (N_EXPERTS, D_MODEL, 2 * D_FF), d_in * DEEPNORM_BETA),
        "b_gate_up": nrm(ks[16], (N_EXPERTS, 2 * D_FF), 0.01),
        "w_down": nrm(ks[17], (N_EXPERTS, D_FF, D_MODEL), D_FF ** -0.5 * DEEPNORM_BETA),
        "b_down": nrm(ks[18], (N_EXPERTS, D_MODEL), 0.01),
        "ln2_g": 1.0 + nrm(ks[19], (D_MODEL,), 0.01),
        "ln2_b": nrm(ks[20], (D_MODEL,), 0.01),
    }


def reference(x, w_in, conv_w, conv_b, conv_ln_g, conv_ln_b, w_out, ln1_g, ln1_b,
              w_router, b_router, w_gate_up, b_gate_up, w_down, b_down, ln2_g, ln2_b):
    B, S, _ = x.shape
    h = x
    for _layer in range(DEPTH):
        proj = h @ w_in
        q, k, v, ga, gb = jnp.split(
            proj, [D_ATTN, 2 * D_ATTN, 3 * D_ATTN, 3 * D_ATTN + D_CONV], axis=-1)
        q = q.reshape(B, S, N_HEADS, HEAD_DIM)
        k = k.reshape(B, S, N_HEADS, HEAD_DIM)
        v = v.reshape(B, S, N_HEADS, HEAD_DIM)
        attn = moba_attention(q, k, v)
        conv = conformer_conv(ga, gb, conv_w, conv_b, conv_ln_g, conv_ln_b)
        mix = jnp.concatenate([attn, conv], axis=-1) @ w_out
        h = layer_norm(DEEPNORM_ALPHA * h + mix, ln1_g, ln1_b)
        ffn = moe_ffn(h, w_router, b_router, w_gate_up, b_gate_up, w_down, b_down)
        h = layer_norm(DEEPNORM_ALPHA * h + ffn, ln2_g, ln2_b)
    return h
```

```python
import functools

import jax
import jax.numpy as jnp
from jax import lax
from jax.experimental import pallas as pl
from jax.experimental.pallas import tpu as pltpu
from jax.experimental.pallas import tpu_sc as plsc

F32 = jnp.float32
BF16 = jnp.bfloat16
I32 = jnp.int32

HEAD_DIM = 128
MOBA_BLOCK = 256
MOBA_TOPK = 3
CONV_WIDTH = 31
CONV_HALO = 32
TOP_K = 4
SWIGLU_ALPHA = 1.702
SWIGLU_LIMIT = 7.0
LN_EPS = 1e-5
DEPTH = 1
DEEPNORM_ALPHA = (2.0 * DEPTH) ** 0.25
ROUTE_PAD = 128
AUG_DIM = 256
MASK_NEG = -1e30
VMEM_LIMIT = 56 * 1024 * 1024


def _cparams(sem):
    return pltpu.CompilerParams(dimension_semantics=sem, vmem_limit_bytes=VMEM_LIMIT)


def _proj_nn_kernel(x_ref, w_ref, o_ref, wb_ref):
    @pl.when(pl.program_id(1) == 0)
    def _():
        wb_ref[...] = w_ref[...].astype(BF16)

    o_ref[...] = jnp.dot(x_ref[...].astype(BF16), wb_ref[...],
                         preferred_element_type=F32).astype(o_ref.dtype)


def _proj_nt_kernel(wt_ref, x_ref, o_ref, wb_ref):
    @pl.when(pl.program_id(1) == 0)
    def _():
        wb_ref[...] = wt_ref[...].astype(BF16)

    o_ref[...] = lax.dot_general(wb_ref[...], x_ref[...].astype(BF16),
                                 (((1,), (1,)), ((), ())),
                                 preferred_element_type=F32).astype(o_ref.dtype)


def _in_projection(x2, w_in, g, tm):
    s, d = x2.shape
    kgg = pl.pallas_call(
        _proj_nn_kernel,
        out_shape=jax.ShapeDtypeStruct((s, 3 * g), BF16),
        grid=(3, s // tm),
        in_specs=[pl.BlockSpec((tm, d), lambda n, m: (m, 0)),
                  pl.BlockSpec((d, g), lambda n, m: (0, jnp.where(n == 0, 1, n + 2)))],
        out_specs=pl.BlockSpec((tm, g), lambda n, m: (m, n)),
        scratch_shapes=[pltpu.VMEM((d, g), BF16)],
        compiler_params=_cparams(("arbitrary", "arbitrary")),
        name="proj_nn",
    )(x2, w_in)
    w_qv_t = jnp.concatenate([w_in[:, :g], w_in[:, 2 * g:3 * g]], axis=1).T
    qv_t = pl.pallas_call(
        _proj_nt_kernel,
        out_shape=jax.ShapeDtypeStruct((2 * g, s), BF16),
        grid=(2, s // tm),
        in_specs=[pl.BlockSpec((g, d), lambda n, m: (n, 0)),
                  pl.BlockSpec((tm, d), lambda n, m: (m, 0))],
        out_specs=pl.BlockSpec((g, tm), lambda n, m: (n, m)),
        scratch_shapes=[pltpu.VMEM((g, d), BF16)],
        compiler_params=_cparams(("arbitrary", "arbitrary")),
        name="proj_nt",
    )(w_qv_t, x2)
    return kgg, qv_t


def _kmean_kernel(k_ref, o_ref, *, blocks):
    for b in range(blocks):
        kb = k_ref[b * MOBA_BLOCK:(b + 1) * MOBA_BLOCK, :].astype(F32)
        o_ref[b:b + 1, :] = jnp.sum(kb, axis=0, keepdims=True) * (1.0 / MOBA_BLOCK)


def _block_means(kgg, g, nb):
    blocks = 8
    return pl.pallas_call(
        functools.partial(_kmean_kernel, blocks=blocks),
        out_shape=jax.ShapeDtypeStruct((nb, g), F32),
        grid=(nb // blocks,),
        in_specs=[pl.BlockSpec((blocks * MOBA_BLOCK, g), lambda i: (i, 0))],
        out_specs=pl.BlockSpec((blocks, g), lambda i: (i, 0)),
        compiler_params=_cparams(("arbitrary",)),
        name="moba_kmean",
    )(kgg)


def _select_kernel(slopes_ref, qt_ref, km_ref, k_ref, qa_ref, ka_ref, *, nb, tq):
    h = pl.program_id(0)
    t = pl.program_id(1)
    slope = slopes_ref[h]
    q_t = qt_ref[...].astype(F32)
    gate = jnp.dot(km_ref[...], q_t, preferred_element_type=F32,
                   precision=lax.Precision.HIGHEST)
    col = t * tq + lax.broadcasted_iota(I32, (nb, tq), 1)
    qblk = col // MOBA_BLOCK
    j = lax.broadcasted_iota(I32, (nb, tq), 0)
    neg_inf = jnp.float32(-jnp.inf)
    gcur = jnp.where(j < qblk, gate, neg_inf)
    sel = j == qblk
    for _ in range(MOBA_TOPK):
        m = jnp.max(gcur, axis=0, keepdims=True)
        is_max = jnp.logical_and(gcur == m, m > neg_inf)
        idx = jnp.min(jnp.where(is_max, j, nb), axis=0, keepdims=True)
        pick = j == idx
        sel = jnp.logical_or(sel, pick)
        gcur = jnp.where(pick, neg_inf, gcur)
    selbias = jnp.where(sel, 0.0, MASK_NEG)

    scale = HEAD_DIM ** -0.5
    qa_ref[0, 0:HEAD_DIM, :] = (q_t * scale).astype(BF16)
    qa_ref[0, HEAD_DIM:HEAD_DIM + nb, :] = selbias.astype(BF16)
    n_extra = AUG_DIM - HEAD_DIM - nb
    r = lax.broadcasted_iota(I32, (n_extra, tq), 0)
    qb = ((t * tq + lax.broadcasted_iota(I32, (n_extra, tq), 1)) // MOBA_BLOCK).astype(F32)
    extra = jnp.where(r == 0, slope,
                      jnp.where(r == 1, slope * MOBA_BLOCK,
                                jnp.where(r == 2, -slope * MOBA_BLOCK * qb, 0.0)))
    qa_ref[0, HEAD_DIM + nb:AUG_DIM, :] = extra.astype(BF16)

    n_aug = AUG_DIM - HEAD_DIM
    pos = t * tq + lax.broadcasted_iota(I32, (tq, n_aug), 0)
    kblk = pos // MOBA_BLOCK
    pib = pos % MOBA_BLOCK
    lane = lax.broadcasted_iota(I32, (tq, n_aug), 1)
    aug = jnp.where(lane < nb, (lane == kblk).astype(F32),
                    jnp.where(lane == nb, pib.astype(F32),
                              jnp.where(lane == nb + 1, kblk.astype(F32),
                                        jnp.where(lane == nb + 2, 1.0, 0.0))))
    ka_ref[0, :, 0:HEAD_DIM] = k_ref[...]
    ka_ref[0, :, HEAD_DIM:AUG_DIM] = aug.astype(BF16)


def _select(slopes, qv_t, kmean, kgg, n_heads, nb, tq):
    s = kgg.shape[0]
    return pl.pallas_call(
        functools.partial(_select_kernel, nb=nb, tq=tq),
        out_shape=(jax.ShapeDtypeStruct((n_heads, AUG_DIM, s), BF16),
                   jax.ShapeDtypeStruct((n_heads, s, AUG_DIM), BF16)),
        grid_spec=pltpu.PrefetchScalarGridSpec(
            num_scalar_prefetch=1,
            grid=(n_heads, s // tq),
            in_specs=[pl.BlockSpec((HEAD_DIM, tq), lambda h, t, sl: (h, t)),
                      pl.BlockSpec((nb, HEAD_DIM), lambda h, t, sl: (0, h)),
                      pl.BlockSpec((tq, HEAD_DIM), lambda h, t, sl: (t, h))],
            out_specs=[pl.BlockSpec((1, AUG_DIM, tq), lambda h, t, sl: (h, 0, t)),
                       pl.BlockSpec((1, tq, AUG_DIM), lambda h, t, sl: (h, t, 0))]),
        compiler_params=_cparams(("arbitrary", "arbitrary")),
        name="moba_select",
    )(slopes, qv_t, kmean, kgg)


def _attn_kernel(qa_ref, ka_ref, vt_ref, o_ref):
    i = pl.program_id(1)
    tq = MOBA_BLOCK
    q_aug = qa_ref[0]
    neg_inf = jnp.float32(-jnp.inf)

    def step(j, carry, diagonal):
        m, l, acc = carry
        off = pl.multiple_of(j * MOBA_BLOCK, MOBA_BLOCK)
        kb = ka_ref[0, pl.ds(off, MOBA_BLOCK), :]
        s_t = jnp.dot(kb, q_aug, preferred_element_type=F32)
        if diagonal:
            key = lax.broadcasted_iota(I32, s_t.shape, 0)
            qry = lax.broadcasted_iota(I32, s_t.shape, 1)
            s_t = jnp.where(key <= qry, s_t, neg_inf)
        m_new = jnp.maximum(m, jnp.max(s_t, axis=0, keepdims=True))
        alpha = jnp.exp(m - m_new)
        p = jnp.exp(s_t - m_new)
        l_new = alpha * l + jnp.sum(p, axis=0, keepdims=True)
        vb = vt_ref[:, pl.ds(off, MOBA_BLOCK)]
        acc_new = alpha * acc + jnp.dot(vb, p.astype(BF16), preferred_element_type=F32)
        return m_new, l_new, acc_new

    init = (jnp.full((1, tq), neg_inf, F32), jnp.zeros((1, tq), F32),
            jnp.zeros((HEAD_DIM, tq), F32))
    carry = step(i, init, True)
    carry = lax.fori_loop(0, i, lambda j, c: step(j, c, False), carry)
    _, l, acc = carry
    o_ref[...] = (acc / l).T.astype(o_ref.dtype)


def _attention(q_aug, k_aug, qv_t, n_heads, nb):
    s = k_aug.shape[1]
    return pl.pallas_call(
        _attn_kernel,
        out_shape=jax.ShapeDtypeStruct((s, n_heads * HEAD_DIM), BF16),
        grid=(n_heads, nb),
        in_specs=[pl.BlockSpec((1, AUG_DIM, MOBA_BLOCK), lambda h, i: (h, 0, i)),
                  pl.BlockSpec((1, s, AUG_DIM), lambda h, i: (h, 0, 0)),
                  pl.BlockSpec((HEAD_DIM, s), lambda h, i: (n_heads + h, 0))],
        out_specs=pl.BlockSpec((MOBA_BLOCK, HEAD_DIM), lambda h, i: (i, h)),
        compiler_params=_cparams(("arbitrary", "arbitrary")),
        name="moba_attention",
    )(q_aug, k_aug, qv_t)


def _conv_kernel(ga_ref, gb_ref, w_ref, b_ref, g_ref, beta_ref, o_ref, u_ref, *, ts, sub):
    t = pl.program_id(0)

    @pl.when(t == 0)
    def _():
        u_ref[0:CONV_HALO, :] = jnp.zeros((CONV_HALO, u_ref.shape[1]), F32)

    @pl.when(t > 0)
    def _():
        u_ref[0:CONV_HALO, :] = u_ref[ts:ts + CONV_HALO, :]

    gb = gb_ref[...].astype(F32)
    u_ref[CONV_HALO:CONV_HALO + ts, :] = ga_ref[...].astype(F32) * jax.nn.sigmoid(gb)
    first = CONV_HALO - (CONV_WIDTH - 1)
    for r0 in range(0, ts, sub):
        acc = jnp.broadcast_to(b_ref[...], (sub, u_ref.shape[1]))
        for tap in range(CONV_WIDTH):
            acc = acc + w_ref[tap:tap + 1, :] * u_ref[r0 + first + tap:r0 + first + tap + sub, :]
        mu = jnp.mean(acc, axis=-1, keepdims=True)
        cen = acc - mu
        var = jnp.mean(cen * cen, axis=-1, keepdims=True)
        y = cen * lax.rsqrt(var + LN_EPS) * g_ref[...] + beta_ref[...]
        o_ref[r0:r0 + sub, :] = (y * jax.nn.sigmoid(y)).astype(o_ref.dtype)


def _conformer_conv(kgg, conv_w, conv_b, conv_ln_g, conv_ln_b, g, ts):
    s = kgg.shape[0]
    row = lambda v: v.reshape(1, g).astype(F32)
    return pl.pallas_call(
        functools.partial(_conv_kernel, ts=ts, sub=32),
        out_shape=jax.ShapeDtypeStruct((s, g), BF16),
        grid=(s // ts,),
        in_specs=[pl.BlockSpec((ts, g), lambda t: (t, 1)),
                  pl.BlockSpec((ts, g), lambda t: (t, 2)),
                  pl.BlockSpec((CONV_WIDTH, g), lambda t: (0, 0)),
                  pl.BlockSpec((1, g), lambda t: (0, 0)),
                  pl.BlockSpec((1, g), lambda t: (0, 0)),
                  pl.BlockSpec((1, g), lambda t: (0, 0))],
        out_specs=pl.BlockSpec((ts, g), lambda t: (t, 0)),
        scratch_shapes=[pltpu.VMEM((ts + CONV_HALO, g), F32)],
        compiler_params=_cparams(("arbitrary",)),
        name="conformer_conv",
    )(kgg, kgg, conv_w, row(conv_b), row(conv_ln_g), row(conv_ln_b))


def _layer_norm_rows(z, gain, bias):
    mu = jnp.mean(z, axis=-1, keepdims=True)
    cen = z - mu
    var = jnp.mean(cen * cen, axis=-1, keepdims=True)
    return cen * lax.rsqrt(var + LN_EPS) * gain + bias


def _outproj_kernel(attn_ref, conv_ref, wo_ref, x_ref, g_ref, b_ref, wr_ref, br_ref,
                    h_ref, lg_ref, *, g):
    mix = jnp.dot(attn_ref[...], wo_ref[0:g, :], preferred_element_type=F32)
    mix = mix + jnp.dot(conv_ref[...], wo_ref[g:2 * g, :], preferred_element_type=F32)
    h1 = _layer_norm_rows(DEEPNORM_ALPHA * x_ref[...] + mix, g_ref[...], b_ref[...])
    h_ref[...] = h1
    h_hi = h1.astype(BF16)
    h_lo = (h1 - h_hi.astype(F32)).astype(BF16)
    wr = wr_ref[...]
    w_hi = wr.astype(BF16)
    w_lo = (wr - w_hi.astype(F32)).astype(BF16)
    nt = (((1,), (1,)), ((), ()))
    lg = lax.dot_general(w_hi, h_hi, nt, preferred_element_type=F32)
    lg = lg + lax.dot_general(w_hi, h_lo, nt, preferred_element_type=F32)
    lg = lg + lax.dot_general(w_lo, h_hi, nt, preferred_element_type=F32)
    lg_ref[...] = lg + br_ref[...]


def _out_projection(attn, conv, w_out, x2, ln_g, ln_b, w_router, b_router, tm):
    s, d = x2.shape
    g = attn.shape[1]
    e = w_router.shape[1]
    return pl.pallas_call(
        functools.partial(_outproj_kernel, g=g),
        out_shape=(jax.ShapeDtypeStruct((s, d), F32), jax.ShapeDtypeStruct((e, s), F32)),
        grid=(s // tm,),
        in_specs=[pl.BlockSpec((tm, g), lambda m: (m, 0)),
                  pl.BlockSpec((tm, g), lambda m: (m, 0)),
                  pl.BlockSpec((2 * g, d), lambda m: (0, 0)),
                  pl.BlockSpec((tm, d), lambda m: (m, 0)),
                  pl.BlockSpec((1, d), lambda m: (0, 0)),
                  pl.BlockSpec((1, d), lambda m: (0, 0)),
                  pl.BlockSpec((e, d), lambda m: (0, 0)),
                  pl.BlockSpec((e, 1), lambda m: (0, 0))],
        out_specs=[pl.BlockSpec((tm, d), lambda m: (m, 0)),
                   pl.BlockSpec((e, tm), lambda m: (0, m))],
        compiler_params=_cparams(("arbitrary",)),
        name="outproj_ln1_router",
    )(attn, conv, w_out.astype(BF16), x2, ln_g.reshape(1, d), ln_b.reshape(1, d),
      w_router.T, b_router.reshape(e, 1))


def _route_kernel(lg_ref, e_ref, gate_ref, rank_ref, cnt_ref, carry_ref, *, n_exp, tr):
    t = pl.program_id(0)

    @pl.when(t == 0)
    def _():
        carry_ref[...] = jnp.zeros_like(carry_ref)

    neg_inf = jnp.float32(-jnp.inf)
    cur = lg_ref[...]
    j = lax.broadcasted_iota(I32, (n_exp, tr), 0)
    vals, picks = [], []
    for r in range(TOP_K):
        m = jnp.max(cur, axis=0, keepdims=True)
        idx = jnp.min(jnp.where(cur == m, j, n_exp), axis=0, keepdims=True)
        pick = j == idx
        e_ref[r:r + 1, :] = idx
        vals.append(m)
        picks.append(pick)
        cur = jnp.where(pick, neg_inf, cur)
    exps = [jnp.exp(v - vals[0]) for v in vals]
    den = exps[0]
    for r in range(1, TOP_K):
        den = den + exps[r]
    for r in range(TOP_K):
        gate_ref[r:r + 1, :] = exps[r] / den

    chosen = picks[0].astype(F32)
    for r in range(1, TOP_K):
        chosen = chosen + picks[r].astype(F32)
    a = lax.broadcasted_iota(I32, (tr, tr), 0)
    b = lax.broadcasted_iota(I32, (tr, tr), 1)
    upper = (a < b).astype(BF16)
    excl = jnp.dot(chosen.astype(BF16), upper, preferred_element_type=F32)
    base = carry_ref[:, 0:1]
    rank = excl + base
    for r in range(TOP_K):
        rank_ref[r:r + 1, :] = jnp.sum(jnp.where(picks[r], rank, 0.0), axis=0,
                                       keepdims=True).astype(I32)
    total = base + jnp.sum(chosen, axis=1, keepdims=True)
    carry_ref[...] = jnp.broadcast_to(total, carry_ref.shape)
    cnt_ref[...] = jnp.broadcast_to(total, cnt_ref.shape).astype(I32)


def _route(logits_t, tr):
    n_exp, s = logits_t.shape
    return pl.pallas_call(
        functools.partial(_route_kernel, n_exp=n_exp, tr=tr),
        out_shape=(jax.ShapeDtypeStruct((TOP_K, s), I32), jax.ShapeDtypeStruct((TOP_K, s), F32),
                   jax.ShapeDtypeStruct((TOP_K, s), I32), jax.ShapeDtypeStruct((n_exp, 128), I32)),
        grid=(s // tr,),
        in_specs=[pl.BlockSpec((n_exp, tr), lambda t: (0, t))],
        out_specs=[pl.BlockSpec((TOP_K, tr), lambda t: (0, t)),
                   pl.BlockSpec((TOP_K, tr), lambda t: (0, t)),
                   pl.BlockSpec((TOP_K, tr), lambda t: (0, t)),
                   pl.BlockSpec((n_exp, 128), lambda t: (0, 0))],
        scratch_shapes=[pltpu.VMEM((n_exp, 128), F32)],
        compiler_params=_cparams(("arbitrary",)),
        name="moe_route",
    )(logits_t)


SC_CORES = 2
SC_SUBCORES = 16
SC_GATHER_ROWS = 32


def _sc_gather_rows(table, idx):
    n_rows, d = idx.shape[0], table.shape[1]
    n_workers = SC_CORES * SC_SUBCORES
    per_worker = n_rows // n_workers
    chunk = SC_GATHER_ROWS
    assert n_rows % (n_workers * chunk) == 0
    mesh = plsc.VectorSubcoreMesh(core_axis_name="c", subcore_axis_name="s",
                                  num_cores=SC_CORES, num_subcores=SC_SUBCORES)

    @functools.partial(
        pl.kernel, mesh=mesh, out_type=jax.ShapeDtypeStruct((n_rows, d), table.dtype),
        scratch_types=[pltpu.VMEM((chunk,), I32), pltpu.VMEM((chunk, d), table.dtype),
                       pltpu.SemaphoreType.DMA],
        name="sc_gather_rows")
    def gather(table_hbm, idx_hbm, out_hbm, idx_v, rows_v, sem):
        worker = lax.axis_index("s") * SC_CORES + lax.axis_index("c")
        base = worker * per_worker

        @pl.loop(0, per_worker // chunk)
        def _(ci):
            off = base + ci * chunk
            pltpu.sync_copy(idx_hbm.at[pl.ds(off, chunk)], idx_v)
            pltpu.async_copy(table_hbm.at[idx_v], rows_v, sem).wait()
            pltpu.sync_copy(rows_v, out_hbm.at[pl.ds(off, chunk)])

    return gather(table, idx)


def _expert_kernel(item_e, item_start, item_rows, xs_ref, wg_ref, wu_ref, wd_ref,
                   bg_ref, bu_ref, bd_ref, y_ref, xb_ref, acc_ref, wgu_ref, wdb_ref,
                   sem_in, sem_out, *, tf, nf):
    w = pl.program_id(0)
    f = pl.program_id(1)
    rows = item_rows[w]
    start = item_start[w]
    n_chunks = rows // ROUTE_PAD

    def in_copy(c):
        r0 = pl.multiple_of(c * ROUTE_PAD, ROUTE_PAD)
        src = pl.multiple_of(start + c * ROUTE_PAD, ROUTE_PAD)
        return pltpu.make_async_copy(xs_ref.at[pl.ds(src, ROUTE_PAD)],
                                     acc_ref.at[pl.ds(r0, ROUTE_PAD)], sem_in)

    def out_copy(c):
        r0 = pl.multiple_of(c * ROUTE_PAD, ROUTE_PAD)
        dst = pl.multiple_of(start + c * ROUTE_PAD, ROUTE_PAD)
        return pltpu.make_async_copy(acc_ref.at[pl.ds(r0, ROUTE_PAD)],
                                     y_ref.at[pl.ds(dst, ROUTE_PAD)], sem_out)

    def for_chunks(fn):
        def body(c, carry):
            fn(c)
            return carry
        lax.fori_loop(0, n_chunks, body, 0)

    @pl.when(rows > 0)
    def _():
        @pl.when(f == 0)
        def _():
            for_chunks(lambda c: in_copy(c).start())
            for_chunks(lambda c: in_copy(c).wait())

            def to_bf16(c):
                r0 = pl.multiple_of(c * ROUTE_PAD, ROUTE_PAD)
                xb_ref[pl.ds(r0, ROUTE_PAD), :] = acc_ref[pl.ds(r0, ROUTE_PAD), :].astype(BF16)
            for_chunks(to_bf16)

        wgu_ref[:, 0:tf] = wg_ref[0].astype(BF16)
        wgu_ref[:, tf:2 * tf] = wu_ref[0].astype(BF16)
        wdb_ref[...] = wd_ref[0].astype(BF16)

        def compute(c):
            r0 = pl.multiple_of(c * ROUTE_PAD, ROUTE_PAD)
            xc = xb_ref[pl.ds(r0, ROUTE_PAD), :]
            gu = jnp.dot(xc, wgu_ref[...], preferred_element_type=F32)
            gate = jnp.minimum(gu[:, 0:tf] + bg_ref[0], SWIGLU_LIMIT)
            up = jnp.clip(gu[:, tf:2 * tf] + bu_ref[0], -SWIGLU_LIMIT, SWIGLU_LIMIT)
            act = (up + 1.0) * (gate * jax.nn.sigmoid(SWIGLU_ALPHA * gate))
            part = jnp.dot(act.astype(BF16), wdb_ref[...], preferred_element_type=F32)

            @pl.when(f == 0)
            def _():
                acc_ref[pl.ds(r0, ROUTE_PAD), :] = part + bd_ref[0]

            @pl.when(f > 0)
            def _():
                acc_ref[pl.ds(r0, ROUTE_PAD), :] += part
        for_chunks(compute)

        @pl.when(f == nf - 1)
        def _():
            for_chunks(lambda c: out_copy(c).start())
            for_chunks(lambda c: out_copy(c).wait())


def _experts(item_e, item_start, item_rows, xs, w_gate_up, b_gate_up, w_down, b_down, tm, tf):
    cap, d = xs.shape
    n_exp, _, two_f = w_gate_up.shape
    ff = two_f // 2
    nf = ff // tf
    n_items = item_e.shape[0]

    def fidx(w, f, rows):
        return jnp.where(rows[w] > 0, f, nf - 1)

    return pl.pallas_call(
        functools.partial(_expert_kernel, tf=tf, nf=nf),
        out_shape=jax.ShapeDtypeStruct((cap, d), F32),
        grid_spec=pltpu.PrefetchScalarGridSpec(
            num_scalar_prefetch=3,
            grid=(n_items, nf),
            in_specs=[
                pl.BlockSpec(memory_space=pl.ANY),
                pl.BlockSpec((1, d, tf), lambda w, f, ie, ist, ir: (ie[w], 0, fidx(w, f, ir))),
                pl.BlockSpec((1, d, tf), lambda w, f, ie, ist, ir: (ie[w], 0, nf + fidx(w, f, ir))),
                pl.BlockSpec((1, tf, d), lambda w, f, ie, ist, ir: (ie[w], fidx(w, f, ir), 0)),
                pl.BlockSpec((1, 1, tf), lambda w, f, ie, ist, ir: (ie[w], 0, fidx(w, f, ir))),
                pl.BlockSpec((1, 1, tf), lambda w, f, ie, ist, ir: (ie[w], 0, nf + fidx(w, f, ir))),
                pl.BlockSpec((1, 1, d), lambda w, f, ie, ist, ir: (ie[w], 0, 0)),
            ],
            out_specs=pl.BlockSpec(memory_space=pl.ANY),
            scratch_shapes=[pltpu.VMEM((tm, d), BF16), pltpu.VMEM((tm, d), F32),
                            pltpu.VMEM((d, 2 * tf), BF16), pltpu.VMEM((tf, d), BF16),
                            pltpu.SemaphoreType.DMA, pltpu.SemaphoreType.DMA]),
        compiler_params=_cparams(("arbitrary", "arbitrary")),
        name="moe_experts",
    )(item_e, item_start, item_rows, xs, w_gate_up, w_gate_up, w_down,
      b_gate_up.reshape(n_exp, 1, two_f), b_gate_up.reshape(n_exp, 1, two_f),
      b_down.reshape(n_exp, 1, d))


def _combine_kernel(yg_ref, gate_ref, h_ref, g_ref, b_ref, o_ref):
    gates = gate_ref[...]
    ffn = gates[:, 0:1] * yg_ref[0]
    for r in range(1, TOP_K):
        ffn = ffn + gates[:, r:r + 1] * yg_ref[r]
    o_ref[...] = _layer_norm_rows(DEEPNORM_ALPHA * h_ref[...] + ffn, g_ref[...], b_ref[...])


def _combine(yg, gates, h1, ln_g, ln_b, tc):
    s, d = h1.shape
    return pl.pallas_call(
        _combine_kernel,
        out_shape=jax.ShapeDtypeStruct((s, d), F32),
        grid=(s // tc,),
        in_specs=[pl.BlockSpec((TOP_K, tc, d), lambda t: (0, t, 0)),
                  pl.BlockSpec((tc, TOP_K), lambda t: (t, 0)),
                  pl.BlockSpec((tc, d), lambda t: (t, 0)),
                  pl.BlockSpec((1, d), lambda t: (0, 0)),
                  pl.BlockSpec((1, d), lambda t: (0, 0))],
        out_specs=pl.BlockSpec((tc, d), lambda t: (t, 0)),
        compiler_params=_cparams(("arbitrary",)),
        name="moe_combine_ln2",
    )(yg, gates, h1, ln_g.reshape(1, d), ln_b.reshape(1, d))


def _tiles(s, d, ff):
    return dict(
        proj_tm=min(512, s), select_tq=min(512, s), conv_ts=min(256, s), outproj_tm=min(256, s),
        route_tr=min(256, s), combine_tc=min(256, s),
        expert_tm=min(1536, max(ROUTE_PAD, (s * TOP_K // 16) // ROUTE_PAD * ROUTE_PAD)),
        expert_tf=min(256, ff))


def kernel(x, w_in, conv_w, conv_b, conv_ln_g, conv_ln_b, w_out, ln1_g, ln1_b,
           w_router, b_router, w_gate_up, b_gate_up, w_down, b_down, ln2_g, ln2_b):
    b, s, d = x.shape
    assert b == 1, "the kernels treat the sequence of the single batch element as the row axis"
    g = w_in.shape[1] // 5
    n_heads = g // HEAD_DIM
    nb = s // MOBA_BLOCK
    n_exp = w_router.shape[1]
    ff = w_down.shape[1]
    assert s % MOBA_BLOCK == 0 and nb % 8 == 0 and HEAD_DIM + nb + 3 <= AUG_DIM
    tl = _tiles(s, d, ff)
    x2 = x.reshape(s, d)

    kgg, qv_t = _in_projection(x2, w_in, g, tl["proj_tm"])
    kmean = _block_means(kgg, g, nb)
    slopes = 2.0 ** (-(8.0 / n_heads) * jnp.arange(1, n_heads + 1, dtype=F32))
    q_aug, k_aug = _select(slopes, qv_t, kmean, kgg, n_heads, nb, tl["select_tq"])
    attn = _attention(q_aug, k_aug, qv_t, n_heads, nb)
    conv = _conformer_conv(kgg, conv_w, conv_b, conv_ln_g, conv_ln_b, g, tl["conv_ts"])
    h1, logits_t = _out_projection(attn, conv, w_out, x2, ln1_g, ln1_b, w_router, b_router,
                                   tl["outproj_tm"])

    e_t, gate_t, rank_t, counts = _route(logits_t, tl["route_tr"])
    counts = counts[:, 0]
    padded = (counts + ROUTE_PAD - 1) // ROUTE_PAD * ROUTE_PAD
    pad_end = jnp.cumsum(padded)
    pad_start = pad_end - padded
    cap = s * TOP_K + n_exp * ROUTE_PAD
    dest_t = pad_start[e_t] + rank_t
    tok = jnp.broadcast_to(jnp.arange(s, dtype=I32), (TOP_K, s))
    row_tok = jnp.zeros((cap,), I32).at[dest_t.reshape(-1)].set(tok.reshape(-1))
    xs = _sc_gather_rows(h1, row_tok)

    tm = tl["expert_tm"]
    n_items = cap // tm + n_exp
    per_e = (padded + tm - 1) // tm
    item_end = jnp.cumsum(per_e)
    item_ids = jnp.arange(n_items, dtype=I32)
    item_e = jnp.minimum(jnp.searchsorted(item_end, item_ids, side="right"), n_exp - 1).astype(I32)
    piece = item_ids - (item_end - per_e)[item_e]
    active = item_ids < item_end[-1]
    item_start = jnp.where(active, pad_start[item_e] + piece * tm, 0).astype(I32)
    item_rows = jnp.where(active, jnp.minimum(tm, padded[item_e] - piece * tm), 0).astype(I32)
    last_e = item_e[jnp.maximum(item_end[-1] - 1, 0)]
    item_e = jnp.where(active, item_e, last_e).astype(I32)

    y = _experts(item_e, item_start, item_rows, xs, w_gate_up, b_gate_up, w_down, b_down,
                 tm, tl["expert_tf"])
    yg = _sc_gather_rows(y, dest_t.reshape(-1)).reshape(TOP_K, s, d)
    out = _combine(yg, gate_t.T, h1, ln2_g, ln2_b, tl["combine_tc"])
    return out.reshape(b, s, d)
```

```python
import functools

import jax
import jax.numpy as jnp
from jax import lax
from jax.experimental import pallas as pl
from jax.experimental.pallas import tpu as pltpu
from jax.experimental.pallas import tpu_sc as plsc

F32 = jnp.float32
BF16 = jnp.bfloat16
I32 = jnp.int32

HEAD_DIM = 128
MOBA_BLOCK = 256
MOBA_TOPK = 3
CONV_WIDTH = 31
CONV_HALO = 32
SUBLANES = 8
TOP_K = 4
SWIGLU_ALPHA = 1.702
SWIGLU_LIMIT = 7.0
LN_EPS = 1e-5
DEPTH = 1
DEEPNORM_ALPHA = (2.0 * DEPTH) ** 0.25
ROUTE_PAD = 128
AUG_DIM = 256
MASK_NEG = -1e30
VMEM_LIMIT = 56 * 1024 * 1024


def _cparams(sem):
    return pltpu.CompilerParams(dimension_semantics=sem, vmem_limit_bytes=VMEM_LIMIT)


def _proj_nn_kernel(x_ref, w_ref, o_ref, wb_ref):
    @pl.when(pl.program_id(1) == 0)
    def _():
        wb_ref[...] = w_ref[...].astype(BF16)

    o_ref[...] = jnp.dot(x_ref[...].astype(BF16), wb_ref[...],
                         preferred_element_type=F32).astype(o_ref.dtype)


def _proj_nt_kernel(wt_ref, x_ref, o_ref, wb_ref):
    @pl.when(pl.program_id(1) == 0)
    def _():
        wb_ref[...] = wt_ref[...].astype(BF16)

    o_ref[...] = lax.dot_general(wb_ref[...], x_ref[...].astype(BF16),
                                 (((1,), (1,)), ((), ())),
                                 preferred_element_type=F32).astype(o_ref.dtype)


def _in_projection(x2, w_in, g, tm):
    s, d = x2.shape
    kgg = pl.pallas_call(
        _proj_nn_kernel,
        out_shape=jax.ShapeDtypeStruct((s, 3 * g), BF16),
        grid=(3, s // tm),
        in_specs=[pl.BlockSpec((tm, d), lambda n, m: (m, 0)),
                  pl.BlockSpec((d, g), lambda n, m: (0, jnp.where(n == 0, 1, n + 2)))],
        out_specs=pl.BlockSpec((tm, g), lambda n, m: (m, n)),
        scratch_shapes=[pltpu.VMEM((d, g), BF16)],
        compiler_params=_cparams(("arbitrary", "arbitrary")),
        name="proj_nn",
    )(x2, w_in)
    w_qv_t = jnp.concatenate([w_in[:, :g], w_in[:, 2 * g:3 * g]], axis=1).T
    qv_t = pl.pallas_call(
        _proj_nt_kernel,
        out_shape=jax.ShapeDtypeStruct((2 * g, s), BF16),
        grid=(2, s // tm),
        in_specs=[pl.BlockSpec((g, d), lambda n, m: (n, 0)),
                  pl.BlockSpec((tm, d), lambda n, m: (m, 0))],
        out_specs=pl.BlockSpec((g, tm), lambda n, m: (n, m)),
        scratch_shapes=[pltpu.VMEM((g, d), BF16)],
        compiler_params=_cparams(("arbitrary", "arbitrary")),
        name="proj_nt",
    )(w_qv_t, x2)
    return kgg, qv_t


def _kmean_kernel(k_ref, o_ref, *, blocks):
    for b in range(blocks):
        kb = k_ref[b * MOBA_BLOCK:(b + 1) * MOBA_BLOCK, :].astype(F32)
        o_ref[b:b + 1, :] = jnp.sum(kb, axis=0, keepdims=True) * (1.0 / MOBA_BLOCK)


def _block_means(kgg, g, nb):
    blocks = 8
    return pl.pallas_call(
        functools.partial(_kmean_kernel, blocks=blocks),
        out_shape=jax.ShapeDtypeStruct((nb, g), F32),
        grid=(nb // blocks,),
        in_specs=[pl.BlockSpec((blocks * MOBA_BLOCK, g), lambda i: (i, 0))],
        out_specs=pl.BlockSpec((blocks, g), lambda i: (i, 0)),
        compiler_params=_cparams(("arbitrary",)),
        name="moba_kmean",
    )(kgg)


def _select_kernel(slopes_ref, qt_ref, km_ref, k_ref, qa_ref, ka_ref, *, nb, tq):
    h = pl.program_id(0)
    t = pl.program_id(1)
    slope = slopes_ref[h]
    q_t = qt_ref[...].astype(F32)
    gate = jnp.dot(km_ref[...], q_t, preferred_element_type=F32,
                   precision=lax.Precision.HIGHEST)
    col = t * tq + lax.broadcasted_iota(I32, (nb, tq), 1)
    qblk = col // MOBA_BLOCK
    j = lax.broadcasted_iota(I32, (nb, tq), 0)
    neg_inf = jnp.float32(-jnp.inf)
    gcur = jnp.where(j < qblk, gate, neg_inf)
    sel = j == qblk
    for _ in range(MOBA_TOPK):
        m = jnp.max(gcur, axis=0, keepdims=True)
        is_max = jnp.logical_and(gcur == m, m > neg_inf)
        idx = jnp.min(jnp.where(is_max, j, nb), axis=0, keepdims=True)
        pick = j == idx
        sel = jnp.logical_or(sel, pick)
        gcur = jnp.where(pick, neg_inf, gcur)
    selbias = jnp.where(sel, 0.0, MASK_NEG)

    scale = HEAD_DIM ** -0.5
    qa_ref[0, 0:HEAD_DIM, :] = (q_t * scale).astype(BF16)
    qa_ref[0, HEAD_DIM:HEAD_DIM + nb, :] = selbias.astype(BF16)
    n_extra = AUG_DIM - HEAD_DIM - nb
    r = lax.broadcasted_iota(I32, (n_extra, tq), 0)
    qb = ((t * tq + lax.broadcasted_iota(I32, (n_extra, tq), 1)) // MOBA_BLOCK).astype(F32)
    extra = jnp.where(r == 0, slope,
                      jnp.where(r == 1, slope * MOBA_BLOCK,
                                jnp.where(r == 2, -slope * MOBA_BLOCK * qb, 0.0)))
    qa_ref[0, HEAD_DIM + nb:AUG_DIM, :] = extra.astype(BF16)

    n_aug = AUG_DIM - HEAD_DIM
    pos = t * tq + lax.broadcasted_iota(I32, (tq, n_aug), 0)
    kblk = pos // MOBA_BLOCK
    pib = pos % MOBA_BLOCK
    lane = lax.broadcasted_iota(I32, (tq, n_aug), 1)
    aug = jnp.where(lane < nb, (lane == kblk).astype(F32),
                    jnp.where(lane == nb, pib.astype(F32),
                              jnp.where(lane == nb + 1, kblk.astype(F32),
                                        jnp.where(lane == nb + 2, 1.0, 0.0))))
    ka_ref[0, :, 0:HEAD_DIM] = k_ref[...]
    ka_ref[0, :, HEAD_DIM:AUG_DIM] = aug.astype(BF16)


def _select(slopes, qv_t, kmean, kgg, n_heads, nb, tq):
    s = kgg.shape[0]
    return pl.pallas_call(
        functools.partial(_select_kernel, nb=nb, tq=tq),
        out_shape=(jax.ShapeDtypeStruct((n_heads, AUG_DIM, s), BF16),
                   jax.ShapeDtypeStruct((n_heads, s, AUG_DIM), BF16)),
        grid_spec=pltpu.PrefetchScalarGridSpec(
            num_scalar_prefetch=1,
            grid=(n_heads, s // tq),
            in_specs=[pl.BlockSpec((HEAD_DIM, tq), lambda h, t, sl: (h, t)),
                      pl.BlockSpec((nb, HEAD_DIM), lambda h, t, sl: (0, h)),
                      pl.BlockSpec((tq, HEAD_DIM), lambda h, t, sl: (t, h))],
            out_specs=[pl.BlockSpec((1, AUG_DIM, tq), lambda h, t, sl: (h, 0, t)),
                       pl.BlockSpec((1, tq, AUG_DIM), lambda h, t, sl: (h, t, 0))]),
        compiler_params=_cparams(("arbitrary", "arbitrary")),
        name="moba_select",
    )(slopes, qv_t, kmean, kgg)


def _attn_kernel(qa_ref, ka_ref, vt_ref, o_ref, *, heads):
    i = pl.program_id(1)
    tq = MOBA_BLOCK
    neg_inf = jnp.float32(-jnp.inf)

    def step(j, carries, diagonal):
        off = pl.multiple_of(j * MOBA_BLOCK, MOBA_BLOCK)
        scores = [jnp.dot(ka_ref[hh, pl.ds(off, MOBA_BLOCK), :], qa_ref[hh],
                          preferred_element_type=F32) for hh in range(heads)]
        stats = []
        for hh in range(heads):
            m, l, _ = carries[hh]
            s_t = scores[hh]
            if diagonal:
                key = lax.broadcasted_iota(I32, s_t.shape, 0)
                qry = lax.broadcasted_iota(I32, s_t.shape, 1)
                s_t = jnp.where(key <= qry, s_t, neg_inf)
            m_new = jnp.maximum(m, jnp.max(s_t, axis=0, keepdims=True))
            alpha = jnp.exp(m - m_new)
            p = jnp.exp(s_t - m_new)
            l_new = alpha * l + jnp.sum(p, axis=0, keepdims=True)
            stats.append((m_new, l_new, alpha, p.astype(BF16)))
        out = []
        for hh in range(heads):
            m_new, l_new, alpha, p = stats[hh]
            vb = vt_ref[hh * HEAD_DIM:(hh + 1) * HEAD_DIM, pl.ds(off, MOBA_BLOCK)]
            acc_new = alpha * carries[hh][2] + jnp.dot(vb, p, preferred_element_type=F32)
            out.append((m_new, l_new, acc_new))
        return tuple(out)

    init = tuple((jnp.full((1, tq), neg_inf, F32), jnp.zeros((1, tq), F32),
                  jnp.zeros((HEAD_DIM, tq), F32)) for _ in range(heads))
    carries = step(i, init, True)
    carries = lax.fori_loop(0, i, lambda j, c: step(j, c, False), carries)
    for hh in range(heads):
        _, l, acc = carries[hh]
        o_ref[:, hh * HEAD_DIM:(hh + 1) * HEAD_DIM] = (acc / l).T.astype(o_ref.dtype)


def _attention(q_aug, k_aug, qv_t, n_heads, nb, heads):
    s = k_aug.shape[1]
    groups = n_heads // heads
    once = pl.Buffered(1)
    return pl.pallas_call(
        functools.partial(_attn_kernel, heads=heads),
        out_shape=jax.ShapeDtypeStruct((s, n_heads * HEAD_DIM), BF16),
        grid=(groups, nb),
        in_specs=[pl.BlockSpec((heads, AUG_DIM, MOBA_BLOCK), lambda hg, i: (hg, 0, i)),
                  pl.BlockSpec((heads, s, AUG_DIM), lambda hg, i: (hg, 0, 0), pipeline_mode=once),
                  pl.BlockSpec((heads * HEAD_DIM, s), lambda hg, i: (groups + hg, 0),
                               pipeline_mode=once)],
        out_specs=pl.BlockSpec((MOBA_BLOCK, heads * HEAD_DIM), lambda hg, i: (i, hg)),
        compiler_params=_cparams(("arbitrary", "arbitrary")),
        name="moba_attention",
    )(q_aug, k_aug, qv_t)


def _conv_kernel(ga_ref, gb_ref, w_ref, b_ref, g_ref, beta_ref, o_ref, u_ref, ush_ref, *, ts, sub):
    t = pl.program_id(0)

    @pl.when(t == 0)
    def _():
        u_ref[0:CONV_HALO, :] = jnp.zeros((CONV_HALO, u_ref.shape[1]), F32)

    @pl.when(t > 0)
    def _():
        u_ref[0:CONV_HALO, :] = u_ref[ts:ts + CONV_HALO, :]

    gb = gb_ref[...].astype(F32)
    u_ref[CONV_HALO:CONV_HALO + ts, :] = ga_ref[...].astype(F32) * jax.nn.sigmoid(gb)
    for b in range(SUBLANES):
        ush_ref[b, 0:ts + CONV_HALO - b, :] = u_ref[b:ts + CONV_HALO, :]
    first = CONV_HALO - (CONV_WIDTH - 1)
    for r0 in range(0, ts, sub):
        acc = jnp.broadcast_to(b_ref[...], (sub, u_ref.shape[1]))
        for tap in range(CONV_WIDTH):
            off = first + tap
            aligned = r0 + off - off % SUBLANES
            acc = acc + w_ref[tap:tap + 1, :] * ush_ref[off % SUBLANES, aligned:aligned + sub, :]
        mu = jnp.mean(acc, axis=-1, keepdims=True)
        cen = acc - mu
        var = jnp.mean(cen * cen, axis=-1, keepdims=True)
        y = cen * lax.rsqrt(var + LN_EPS) * g_ref[...] + beta_ref[...]
        o_ref[r0:r0 + sub, :] = (y * jax.nn.sigmoid(y)).astype(o_ref.dtype)


def _conformer_conv(kgg, conv_w, conv_b, conv_ln_g, conv_ln_b, g, ts):
    s = kgg.shape[0]
    row = lambda v: v.reshape(1, g).astype(F32)
    return pl.pallas_call(
        functools.partial(_conv_kernel, ts=ts, sub=32),
        out_shape=jax.ShapeDtypeStruct((s, g), BF16),
        grid=(s // ts,),
        in_specs=[pl.BlockSpec((ts, g), lambda t: (t, 1)),
                  pl.BlockSpec((ts, g), lambda t: (t, 2)),
                  pl.BlockSpec((CONV_WIDTH, g), lambda t: (0, 0)),
                  pl.BlockSpec((1, g), lambda t: (0, 0)),
                  pl.BlockSpec((1, g), lambda t: (0, 0)),
                  pl.BlockSpec((1, g), lambda t: (0, 0))],
        out_specs=pl.BlockSpec((ts, g), lambda t: (t, 0)),
        scratch_shapes=[pltpu.VMEM((ts + CONV_HALO, g), F32),
                        pltpu.VMEM((SUBLANES, ts + CONV_HALO, g), F32)],
        compiler_params=_cparams(("arbitrary",)),
        name="conformer_conv",
    )(kgg, kgg, conv_w, row(conv_b), row(conv_ln_g), row(conv_ln_b))


def _layer_norm_rows(z, gain, bias):
    mu = jnp.mean(z, axis=-1, keepdims=True)
    cen = z - mu
    var = jnp.mean(cen * cen, axis=-1, keepdims=True)
    return cen * lax.rsqrt(var + LN_EPS) * gain + bias


HI_HALF = -65536


def _pack_halves(lo, hi):
    lo_bits = pltpu.bitcast(lo.astype(BF16).astype(F32), I32)
    hi_bits = pltpu.bitcast(hi.astype(BF16).astype(F32), I32)
    return lax.shift_right_logical(lo_bits, jnp.int32(16)) | (hi_bits & jnp.int32(HI_HALF))


def _unpack_halves(words):
    lo = pltpu.bitcast(lax.shift_left(words, jnp.int32(16)), F32)
    hi = pltpu.bitcast(words & jnp.int32(HI_HALF), F32)
    return lo, hi


def _outproj_kernel(attn_ref, conv_ref, wo_ref, x_ref, g_ref, b_ref, wr_ref, br_ref,
                    h_ref, hp_ref, lg_ref, *, g):
    mix = jnp.dot(attn_ref[...], wo_ref[0:g, :], preferred_element_type=F32)
    mix = mix + jnp.dot(conv_ref[...], wo_ref[g:2 * g, :], preferred_element_type=F32)
    h1 = _layer_norm_rows(DEEPNORM_ALPHA * x_ref[...] + mix, g_ref[...], b_ref[...])
    h_ref[...] = h1
    half = h1.shape[1] // 2
    hp_ref[...] = _pack_halves(h1[:, 0:half], h1[:, half:])
    h_hi = h1.astype(BF16)
    h_lo = (h1 - h_hi.astype(F32)).astype(BF16)
    wr = wr_ref[...]
    w_hi = wr.astype(BF16)
    w_lo = (wr - w_hi.astype(F32)).astype(BF16)
    nt = (((1,), (1,)), ((), ()))
    lg = lax.dot_general(w_hi, h_hi, nt, preferred_element_type=F32)
    lg = lg + lax.dot_general(w_hi, h_lo, nt, preferred_element_type=F32)
    lg = lg + lax.dot_general(w_lo, h_hi, nt, preferred_element_type=F32)
    lg_ref[...] = lg + br_ref[...]


def _out_projection(attn, conv, w_out, x2, ln_g, ln_b, w_router, b_router, tm):
    s, d = x2.shape
    g = attn.shape[1]
    e = w_router.shape[1]
    return pl.pallas_call(
        functools.partial(_outproj_kernel, g=g),
        out_shape=(jax.ShapeDtypeStruct((s, d), F32), jax.ShapeDtypeStruct((s, d // 2), I32),
                   jax.ShapeDtypeStruct((e, s), F32)),
        grid=(s // tm,),
        in_specs=[pl.BlockSpec((tm, g), lambda m: (m, 0)),
                  pl.BlockSpec((tm, g), lambda m: (m, 0)),
                  pl.BlockSpec((2 * g, d), lambda m: (0, 0)),
                  pl.BlockSpec((tm, d), lambda m: (m, 0)),
                  pl.BlockSpec((1, d), lambda m: (0, 0)),
                  pl.BlockSpec((1, d), lambda m: (0, 0)),
                  pl.BlockSpec((e, d), lambda m: (0, 0)),
                  pl.BlockSpec((e, 1), lambda m: (0, 0))],
        out_specs=[pl.BlockSpec((tm, d), lambda m: (m, 0)),
                   pl.BlockSpec((tm, d // 2), lambda m: (m, 0)),
                   pl.BlockSpec((e, tm), lambda m: (0, m))],
        compiler_params=_cparams(("arbitrary",)),
        name="outproj_ln1_router",
    )(attn, conv, w_out.astype(BF16), x2, ln_g.reshape(1, d), ln_b.reshape(1, d),
      w_router.T, b_router.reshape(e, 1))


def _route_kernel(lg_ref, e_ref, gate_ref, rank_ref, cnt_ref, carry_ref, *, n_exp, tr):
    t = pl.program_id(0)

    @pl.when(t == 0)
    def _():
        carry_ref[...] = jnp.zeros_like(carry_ref)

    neg_inf = jnp.float32(-jnp.inf)
    cur = lg_ref[...]
    j = lax.broadcasted_iota(I32, (n_exp, tr), 0)
    vals, picks = [], []
    for r in range(TOP_K):
        m = jnp.max(cur, axis=0, keepdims=True)
        idx = jnp.min(jnp.where(cur == m, j, n_exp), axis=0, keepdims=True)
        pick = j == idx
        e_ref[r:r + 1, :] = idx
        vals.append(m)
        picks.append(pick)
        cur = jnp.where(pick, neg_inf, cur)
    exps = [jnp.exp(v - vals[0]) for v in vals]
    den = exps[0]
    for r in range(1, TOP_K):
        den = den + exps[r]
    for r in range(TOP_K):
        gate_ref[r:r + 1, :] = exps[r] / den

    chosen = picks[0].astype(F32)
    for r in range(1, TOP_K):
        chosen = chosen + picks[r].astype(F32)
    a = lax.broadcasted_iota(I32, (tr, tr), 0)
    b = lax.broadcasted_iota(I32, (tr, tr), 1)
    upper = (a < b).astype(BF16)
    excl = jnp.dot(chosen.astype(BF16), upper, preferred_element_type=F32)
    base = carry_ref[:, 0:1]
    rank = excl + base
    for r in range(TOP_K):
        rank_ref[r:r + 1, :] = jnp.sum(jnp.where(picks[r], rank, 0.0), axis=0,
                                       keepdims=True).astype(I32)
    total = base + jnp.sum(chosen, axis=1, keepdims=True)
    carry_ref[...] = jnp.broadcast_to(total, carry_ref.shape)
    cnt_ref[...] = jnp.broadcast_to(total, cnt_ref.shape).astype(I32)


def _route(logits_t, tr):
    n_exp, s = logits_t.shape
    return pl.pallas_call(
        functools.partial(_route_kernel, n_exp=n_exp, tr=tr),
        out_shape=(jax.ShapeDtypeStruct((TOP_K, s), I32), jax.ShapeDtypeStruct((TOP_K, s), F32),
                   jax.ShapeDtypeStruct((TOP_K, s), I32), jax.ShapeDtypeStruct((n_exp, 128), I32)),
        grid=(s // tr,),
        in_specs=[pl.BlockSpec((n_exp, tr), lambda t: (0, t))],
        out_specs=[pl.BlockSpec((TOP_K, tr), lambda t: (0, t)),
                   pl.BlockSpec((TOP_K, tr), lambda t: (0, t)),
                   pl.BlockSpec((TOP_K, tr), lambda t: (0, t)),
                   pl.BlockSpec((n_exp, 128), lambda t: (0, 0))],
        scratch_shapes=[pltpu.VMEM((n_exp, 128), F32)],
        compiler_params=_cparams(("arbitrary",)),
        name="moe_route",
    )(logits_t)


SC_CORES = 2
SC_SUBCORES = 16
SC_LANES = 16
SC_GATHER_ROWS = 32
SC_SCAN = 2048


_SC_PARAMS = pltpu.CompilerParams(needs_layout_passes=False)


def _sc_mesh():
    return plsc.VectorSubcoreMesh(core_axis_name="c", subcore_axis_name="s",
                                  num_cores=SC_CORES, num_subcores=SC_SUBCORES)


def _sc_worker():
    return lax.axis_index("s") * SC_CORES + lax.axis_index("c")


def _sc_gather_loop(table_hbm, idx_all, out_hbm, bufs, sems, base, n_chunks):
    chunk = SC_GATHER_ROWS

    def gather(ci, b):
        return pltpu.make_async_copy(
            table_hbm.at[idx_all.at[pl.ds(ci * chunk, chunk)]], bufs[b], sems[b])

    gather(0, 0).start()

    @pl.loop(0, n_chunks, step=2)
    def _(ci):
        gather(ci + 1, 1).start()
        gather(ci, 0).wait()
        pltpu.sync_copy(bufs[0], out_hbm.at[pl.ds(base + ci * chunk, chunk)])

        @pl.when(ci + 2 < n_chunks)
        def _():
            gather(ci + 2, 0).start()

        gather(ci + 1, 1).wait()
        pltpu.sync_copy(bufs[1], out_hbm.at[pl.ds(base + (ci + 1) * chunk, chunk)])


def _sc_dest(ps_v, e_buf, r_buf, v):
    e_vec = e_buf[pl.ds(v * SC_LANES, SC_LANES)]
    return plsc.load_gather(ps_v, [e_vec]) + r_buf[pl.ds(v * SC_LANES, SC_LANES)]


def _sc_dispatch(table, e_flat, rank_flat, pad_start, cap, s):
    n_assign, width = e_flat.shape[0], table.shape[1]
    n_workers = SC_CORES * SC_SUBCORES
    per_worker = cap // n_workers
    chunk = SC_GATHER_ROWS
    assert cap % (n_workers * 2 * chunk) == 0 and n_assign % SC_SCAN == 0
    assert per_worker % SC_LANES == 0

    @functools.partial(
        pl.kernel, mesh=_sc_mesh(), out_type=jax.ShapeDtypeStruct((cap, width), table.dtype),
        scratch_types=[pltpu.VMEM((per_worker,), I32), pltpu.VMEM((pad_start.shape[0],), I32),
                       pltpu.VMEM((SC_SCAN,), I32), pltpu.VMEM((SC_SCAN,), I32),
                       pltpu.VMEM((chunk, width), table.dtype),
                       pltpu.VMEM((chunk, width), table.dtype),
                       pltpu.SemaphoreType.DMA, pltpu.SemaphoreType.DMA],
        compiler_params=_SC_PARAMS, name="sc_dispatch")
    def dispatch(table_hbm, e_hbm, r_hbm, ps_hbm, out_hbm, idx_all, ps_v, e_buf, r_buf,
                 buf0, buf1, sem0, sem1):
        base = _sc_worker() * per_worker
        lane = lax.iota(I32, SC_LANES)
        pltpu.sync_copy(ps_hbm, ps_v)

        @pl.loop(0, per_worker // SC_LANES)
        def _(k):
            idx_all[pl.ds(k * SC_LANES, SC_LANES)] = lax.rem(base + k * SC_LANES + lane, s)

        @pl.loop(0, n_assign // SC_SCAN)
        def _(c):
            pltpu.sync_copy(e_hbm.at[pl.ds(c * SC_SCAN, SC_SCAN)], e_buf)
            pltpu.sync_copy(r_hbm.at[pl.ds(c * SC_SCAN, SC_SCAN)], r_buf)

            @pl.loop(0, SC_SCAN // SC_LANES)
            def _(v):
                loc = _sc_dest(ps_v, e_buf, r_buf, v) - base
                mine = jnp.logical_and(loc >= 0, loc < per_worker)
                tok = lax.rem(c * SC_SCAN + v * SC_LANES + lane, s)
                plsc.store_scatter(idx_all, [jnp.where(mine, loc, 0)], tok, mask=mine)

        _sc_gather_loop(table_hbm, idx_all, out_hbm, (buf0, buf1), (sem0, sem1), base,
                        per_worker // chunk)

    return dispatch(table, e_flat, rank_flat, pad_start)


def _sc_combine_gather(y, e_flat, rank_flat, pad_start):
    n_assign, width = e_flat.shape[0], y.shape[1]
    n_workers = SC_CORES * SC_SUBCORES
    per_worker = n_assign // n_workers
    chunk = SC_GATHER_ROWS
    assert n_assign % (n_workers * 2 * chunk) == 0

    @functools.partial(
        pl.kernel, mesh=_sc_mesh(), out_type=jax.ShapeDtypeStruct((n_assign, width), y.dtype),
        scratch_types=[pltpu.VMEM((per_worker,), I32), pltpu.VMEM((pad_start.shape[0],), I32),
                       pltpu.VMEM((per_worker,), I32), pltpu.VMEM((per_worker,), I32),
                       pltpu.VMEM((chunk, width), y.dtype), pltpu.VMEM((chunk, width), y.dtype),
                       pltpu.SemaphoreType.DMA, pltpu.SemaphoreType.DMA],
        compiler_params=_SC_PARAMS, name="sc_combine_gather")
    def combine(y_hbm, e_hbm, r_hbm, ps_hbm, out_hbm, idx_all, ps_v, e_buf, r_buf,
                buf0, buf1, sem0, sem1):
        base = _sc_worker() * per_worker
        pltpu.sync_copy(ps_hbm, ps_v)
        pltpu.sync_copy(e_hbm.at[pl.ds(base, per_worker)], e_buf)
        pltpu.sync_copy(r_hbm.at[pl.ds(base, per_worker)], r_buf)

        @pl.loop(0, per_worker // SC_LANES)
        def _(v):
            idx_all[pl.ds(v * SC_LANES, SC_LANES)] = _sc_dest(ps_v, e_buf, r_buf, v)

        _sc_gather_loop(y_hbm, idx_all, out_hbm, (buf0, buf1), (sem0, sem1), base,
                        per_worker // chunk)

    return combine(y, e_flat, rank_flat, pad_start)


def _expert_kernel(item_e, item_start, item_rows, xs_ref, wg_ref, wu_ref, wd_ref,
                   bg_ref, bu_ref, bd_ref, y_ref, xin_ref, xb_ref, acc_ref,
                   sem_x, sem_y, *, nf, n_items, big):
    w = pl.program_id(0)
    f = pl.program_id(1)
    rows = item_rows[w]
    slot = w % 2
    half = xin_ref.shape[2]
    has_next = jnp.logical_and(w + 1 < n_items, item_rows[jnp.minimum(w + 1, n_items - 1)] > 0)

    def x_copy(item, sl, c):
        src = pl.multiple_of(item_start[item] + c * ROUTE_PAD, ROUTE_PAD)
        dst = pl.multiple_of(c * ROUTE_PAD, ROUTE_PAD)
        return pltpu.make_async_copy(xs_ref.at[pl.ds(src, ROUTE_PAD)],
                                     xin_ref.at[sl, pl.ds(dst, ROUTE_PAD)], sem_x.at[sl])

    def y_copy(item, sl, c):
        src = pl.multiple_of(c * ROUTE_PAD, ROUTE_PAD)
        dst = pl.multiple_of(item_start[item] + c * ROUTE_PAD, ROUTE_PAD)
        return pltpu.make_async_copy(xin_ref.at[sl, pl.ds(src, ROUTE_PAD)],
                                     y_ref.at[pl.ds(dst, ROUTE_PAD)], sem_y)

    def for_chunks(item, fn):
        def body(c, carry):
            fn(c)
            return carry
        lax.fori_loop(0, item_rows[item] // ROUTE_PAD, body, 0)

    def compute(r0, size, first):
        xc = xb_ref[pl.ds(r0, size), :]
        gate = jnp.dot(xc, wg_ref[0].astype(BF16), preferred_element_type=F32) + bg_ref[0]
        up = jnp.dot(xc, wu_ref[0].astype(BF16), preferred_element_type=F32) + bu_ref[0]
        gate = jnp.minimum(gate, SWIGLU_LIMIT)
        up = jnp.clip(up, -SWIGLU_LIMIT, SWIGLU_LIMIT)
        act = (up + 1.0) * (gate * jax.nn.sigmoid(SWIGLU_ALPHA * gate))
        part = jnp.dot(act.astype(BF16), wd_ref[0].astype(BF16), preferred_element_type=F32)
        if first:
            acc_ref[pl.ds(r0, size), :] = part + bd_ref[0]
        else:
            acc_ref[pl.ds(r0, size), :] += part

    def compute_all(first):
        n_big = rows // big

        def body(c, carry):
            compute(pl.multiple_of(c * big, big), big, first)
            return carry
        lax.fori_loop(0, n_big, body, 0)
        done = n_big * big
        size = big // 2
        while size >= ROUTE_PAD:
            take = ((rows - done) & size) != 0
            here = done

            @pl.when(take)
            def _(here=here, size=size):
                compute(pl.multiple_of(here, ROUTE_PAD), size, first)
            done = done + jnp.where(take, size, 0)
            size //= 2

    @pl.when(rows > 0)
    def _():
        @pl.when(f == 0)
        def _():
            @pl.when(w == 0)
            def _():
                for_chunks(w, lambda c: x_copy(w, slot, c).start())

            for_chunks(w, lambda c: x_copy(w, slot, c).wait())

            @pl.when(w > 0)
            def _():
                for_chunks(w - 1, lambda c: y_copy(w - 1, 1 - slot, c).wait())

            @pl.when(has_next)
            def _():
                for_chunks(w + 1, lambda c: x_copy(w + 1, 1 - slot, c).start())

            def unpack(c):
                r0 = pl.multiple_of(c * ROUTE_PAD, ROUTE_PAD)
                lo, hi = _unpack_halves(xin_ref[slot, pl.ds(r0, ROUTE_PAD), :])
                xb_ref[pl.ds(r0, ROUTE_PAD), 0:half] = lo.astype(BF16)
                xb_ref[pl.ds(r0, ROUTE_PAD), half:2 * half] = hi.astype(BF16)
            for_chunks(w, unpack)
            compute_all(True)

        @pl.when(f > 0)
        def _():
            compute_all(False)

        @pl.when(f == nf - 1)
        def _():
            def pack(c):
                r0 = pl.multiple_of(c * ROUTE_PAD, ROUTE_PAD)
                xin_ref[slot, pl.ds(r0, ROUTE_PAD), :] = _pack_halves(
                    acc_ref[pl.ds(r0, ROUTE_PAD), 0:half],
                    acc_ref[pl.ds(r0, ROUTE_PAD), half:2 * half])
                y_copy(w, slot, c).start()
            for_chunks(w, pack)

            @pl.when(jnp.logical_not(has_next))
            def _():
                for_chunks(w, lambda c: y_copy(w, slot, c).wait())


def _experts(item_e, item_start, item_rows, xs, w_gate_up, b_gate_up, w_down, b_down, tm, tf):
    cap, half = xs.shape
    d = 2 * half
    n_exp, _, two_f = w_gate_up.shape
    ff = two_f // 2
    nf = ff // tf
    n_items = item_e.shape[0]
    big = min(512, tm)
    assert big % ROUTE_PAD == 0 and (big & (big - 1)) == 0

    def fidx(w, f, rows):
        return jnp.where(rows[w] > 0, f, nf - 1)

    return pl.pallas_call(
        functools.partial(_expert_kernel, nf=nf, n_items=n_items, big=big),
        out_shape=jax.ShapeDtypeStruct((cap, half), I32),
        grid_spec=pltpu.PrefetchScalarGridSpec(
            num_scalar_prefetch=3,
            grid=(n_items, nf),
            in_specs=[
                pl.BlockSpec(memory_space=pl.ANY),
                pl.BlockSpec((1, d, tf), lambda w, f, ie, ist, ir: (ie[w], 0, fidx(w, f, ir))),
                pl.BlockSpec((1, d, tf), lambda w, f, ie, ist, ir: (ie[w], 0, nf + fidx(w, f, ir))),
                pl.BlockSpec((1, tf, d), lambda w, f, ie, ist, ir: (ie[w], fidx(w, f, ir), 0)),
                pl.BlockSpec((1, 1, tf), lambda w, f, ie, ist, ir: (ie[w], 0, fidx(w, f, ir))),
                pl.BlockSpec((1, 1, tf), lambda w, f, ie, ist, ir: (ie[w], 0, nf + fidx(w, f, ir))),
                pl.BlockSpec((1, 1, d), lambda w, f, ie, ist, ir: (ie[w], 0, 0)),
            ],
            out_specs=pl.BlockSpec(memory_space=pl.ANY),
            scratch_shapes=[pltpu.VMEM((2, tm, half), I32), pltpu.VMEM((tm, d), BF16),
                            pltpu.VMEM((tm, d), F32),
                            pltpu.SemaphoreType.DMA((2,)), pltpu.SemaphoreType.DMA]),
        compiler_params=_cparams(("arbitrary", "arbitrary")),
        name="moe_experts",
    )(item_e, item_start, item_rows, xs, w_gate_up, w_gate_up, w_down,
      b_gate_up.reshape(n_exp, 1, two_f), b_gate_up.reshape(n_exp, 1, two_f),
      b_down.reshape(n_exp, 1, d))


def _combine_kernel(yg_ref, gate_ref, h_ref, g_ref, b_ref, o_ref):
    gates = gate_ref[...]
    half = yg_ref.shape[2]
    z_lo = DEEPNORM_ALPHA * h_ref[:, 0:half]
    z_hi = DEEPNORM_ALPHA * h_ref[:, half:2 * half]
    for r in range(TOP_K):
        lo, hi = _unpack_halves(yg_ref[r])
        z_lo = z_lo + gates[:, r:r + 1] * lo
        z_hi = z_hi + gates[:, r:r + 1] * hi
    inv_d = 1.0 / (2 * half)
    mu = (jnp.sum(z_lo, axis=-1, keepdims=True) + jnp.sum(z_hi, axis=-1, keepdims=True)) * inv_d
    c_lo = z_lo - mu
    c_hi = z_hi - mu
    var = (jnp.sum(c_lo * c_lo, axis=-1, keepdims=True)
           + jnp.sum(c_hi * c_hi, axis=-1, keepdims=True)) * inv_d
    rstd = lax.rsqrt(var + LN_EPS)
    o_ref[:, 0:half] = c_lo * rstd * g_ref[:, 0:half] + b_ref[:, 0:half]
    o_ref[:, half:2 * half] = c_hi * rstd * g_ref[:, half:2 * half] + b_ref[:, half:2 * half]


def _combine(yg, gates, h1, ln_g, ln_b, tc):
    s, d = h1.shape
    return pl.pallas_call(
        _combine_kernel,
        out_shape=jax.ShapeDtypeStruct((s, d), F32),
        grid=(s // tc,),
        in_specs=[pl.BlockSpec((TOP_K, tc, d // 2), lambda t: (0, t, 0)),
                  pl.BlockSpec((tc, TOP_K), lambda t: (t, 0)),
                  pl.BlockSpec((tc, d), lambda t: (t, 0)),
                  pl.BlockSpec((1, d), lambda t: (0, 0)),
                  pl.BlockSpec((1, d), lambda t: (0, 0))],
        out_specs=pl.BlockSpec((tc, d), lambda t: (t, 0)),
        compiler_params=_cparams(("arbitrary",)),
        name="moe_combine_ln2",
    )(yg, gates, h1, ln_g.reshape(1, d), ln_b.reshape(1, d))


def _tiles(s, d, ff, n_heads):
    return dict(
        proj_tm=min(512, s), select_tq=min(512, s), conv_ts=min(256, s), outproj_tm=min(512, s),
        attn_heads=min(4, n_heads),
        route_tr=min(256, s), combine_tc=min(256, s),
        expert_tm=min(1536, max(ROUTE_PAD, (s * TOP_K // 16) // ROUTE_PAD * ROUTE_PAD)),
        expert_tf=min(256, ff))


def kernel(x, w_in, conv_w, conv_b, conv_ln_g, conv_ln_b, w_out, ln1_g, ln1_b,
           w_router, b_router, w_gate_up, b_gate_up, w_down, b_down, ln2_g, ln2_b):
    b, s, d = x.shape
    assert b == 1, "the kernels treat the sequence of the single batch element as the row axis"
    g = w_in.shape[1] // 5
    n_heads = g // HEAD_DIM
    nb = s // MOBA_BLOCK
    n_exp = w_router.shape[1]
    ff = w_down.shape[1]
    assert s % MOBA_BLOCK == 0 and nb % 8 == 0 and HEAD_DIM + nb + 3 <= AUG_DIM
    tl = _tiles(s, d, ff, n_heads)
    x2 = x.reshape(s, d)

    kgg, qv_t = _in_projection(x2, w_in, g, tl["proj_tm"])
    kmean = _block_means(kgg, g, nb)
    slopes = 2.0 ** (-(8.0 / n_heads) * jnp.arange(1, n_heads + 1, dtype=F32))
    q_aug, k_aug = _select(slopes, qv_t, kmean, kgg, n_heads, nb, tl["select_tq"])
    attn = _attention(q_aug, k_aug, qv_t, n_heads, nb, tl["attn_heads"])
    conv = _conformer_conv(kgg, conv_w, conv_b, conv_ln_g, conv_ln_b, g, tl["conv_ts"])
    h1, h1_packed, logits_t = _out_projection(attn, conv, w_out, x2, ln1_g, ln1_b,
                                              w_router, b_router, tl["outproj_tm"])

    e_t, gate_t, rank_t, counts = _route(logits_t, tl["route_tr"])
    counts = counts[:, 0]
    padded = (counts + ROUTE_PAD - 1) // ROUTE_PAD * ROUTE_PAD
    pad_end = jnp.cumsum(padded)
    pad_start = (pad_end - padded).astype(I32)
    cap = s * TOP_K + n_exp * ROUTE_PAD
    e_flat = e_t.reshape(-1)
    rank_flat = rank_t.reshape(-1)
    xs = _sc_dispatch(h1_packed, e_flat, rank_flat, pad_start, cap, s)

    tm = tl["expert_tm"]
    n_items = cap // tm + n_exp
    per_e = (padded + tm - 1) // tm
    item_end = jnp.cumsum(per_e)
    item_ids = jnp.arange(n_items, dtype=I32)
    item_e = jnp.minimum(jnp.searchsorted(item_end, item_ids, side="right"), n_exp - 1).astype(I32)
    piece = item_ids - (item_end - per_e)[item_e]
    active = item_ids < item_end[-1]
    item_start = jnp.where(active, pad_start[item_e] + piece * tm, 0).astype(I32)
    item_rows = jnp.where(active, jnp.minimum(tm, padded[item_e] - piece * tm), 0).astype(I32)
    last_e = item_e[jnp.maximum(item_end[-1] - 1, 0)]
    item_e = jnp.where(active, item_e, last_e).astype(I32)

    y = _experts(item_e, item_start, item_rows, xs, w_gate_up, b_gate_up, w_down, b_down,
                 tm, tl["expert_tf"])
    yg = _sc_combine_gather(y, e_flat, rank_flat, pad_start).reshape(TOP_K, s, d // 2)
    out = _combine(yg, gate_t.T, h1, ln2_g, ln2_b, tl["combine_tc"])
    return out.reshape(b, s, d)
```

```python
import functools

import jax
import jax.numpy as jnp
from jax import lax
from jax.experimental import pallas as pl
from jax.experimental.pallas import tpu as pltpu
from jax.experimental.pallas import tpu_sc as plsc

F32 = jnp.float32
BF16 = jnp.bfloat16
I32 = jnp.int32

HEAD_DIM = 128
MOBA_BLOCK = 256
MOBA_TOPK = 3
CONV_WIDTH = 31
CONV_HALO = 32
SUBLANES = 8
TOP_K = 4
SWIGLU_ALPHA = 1.702
SWIGLU_LIMIT = 7.0
LN_EPS = 1e-5
DEPTH = 1
DEEPNORM_ALPHA = (2.0 * DEPTH) ** 0.25
ROUTE_PAD = 128
AUG_DIM = 256
MASK_NEG = -1e30
ONES_ROWS = 16
VMEM_LIMIT = 56 * 1024 * 1024


def _cparams(sem):
    return pltpu.CompilerParams(dimension_semantics=sem, vmem_limit_bytes=VMEM_LIMIT)


def _proj_nn_kernel(x_ref, w_ref, o_ref, wb_ref):
    @pl.when(pl.program_id(1) == 0)
    def _():
        wb_ref[...] = w_ref[...].astype(BF16)

    o_ref[...] = jnp.dot(x_ref[...].astype(BF16), wb_ref[...],
                         preferred_element_type=F32).astype(o_ref.dtype)


def _proj_nt_kernel(wt_ref, x_ref, o_ref, wb_ref):
    @pl.when(pl.program_id(1) == 0)
    def _():
        wb_ref[...] = wt_ref[...].astype(BF16)

    o_ref[...] = lax.dot_general(wb_ref[...], x_ref[...].astype(BF16),
                                 (((1,), (1,)), ((), ())),
                                 preferred_element_type=F32).astype(o_ref.dtype)


def _in_projection(x2, w_in, g, tm):
    s, d = x2.shape
    kgg = pl.pallas_call(
        _proj_nn_kernel,
        out_shape=jax.ShapeDtypeStruct((s, 3 * g), BF16),
        grid=(3, s // tm),
        in_specs=[pl.BlockSpec((tm, d), lambda n, m: (m, 0)),
                  pl.BlockSpec((d, g), lambda n, m: (0, jnp.where(n == 0, 1, n + 2)))],
        out_specs=pl.BlockSpec((tm, g), lambda n, m: (m, n)),
        scratch_shapes=[pltpu.VMEM((d, g), BF16)],
        compiler_params=_cparams(("arbitrary", "arbitrary")),
        name="proj_nn",
    )(x2, w_in)
    w_qv_t = jnp.concatenate([w_in[:, :g], w_in[:, 2 * g:3 * g]], axis=1).T
    qv_t = pl.pallas_call(
        _proj_nt_kernel,
        out_shape=jax.ShapeDtypeStruct((2 * g, s), BF16),
        grid=(2, s // tm),
        in_specs=[pl.BlockSpec((g, d), lambda n, m: (n, 0)),
                  pl.BlockSpec((tm, d), lambda n, m: (m, 0))],
        out_specs=pl.BlockSpec((g, tm), lambda n, m: (n, m)),
        scratch_shapes=[pltpu.VMEM((g, d), BF16)],
        compiler_params=_cparams(("arbitrary", "arbitrary")),
        name="proj_nt",
    )(w_qv_t, x2)
    return kgg, qv_t


def _kmean_kernel(k_ref, o_ref, *, blocks):
    for b in range(blocks):
        kb = k_ref[b * MOBA_BLOCK:(b + 1) * MOBA_BLOCK, :].astype(F32)
        o_ref[b:b + 1, :] = jnp.sum(kb, axis=0, keepdims=True) * (1.0 / MOBA_BLOCK)


def _block_means(kgg, g, nb):
    blocks = 8
    return pl.pallas_call(
        functools.partial(_kmean_kernel, blocks=blocks),
        out_shape=jax.ShapeDtypeStruct((nb, g), F32),
        grid=(nb // blocks,),
        in_specs=[pl.BlockSpec((blocks * MOBA_BLOCK, g), lambda i: (i, 0))],
        out_specs=pl.BlockSpec((blocks, g), lambda i: (i, 0)),
        compiler_params=_cparams(("arbitrary",)),
        name="moba_kmean",
    )(kgg)


def _select_kernel(slopes_ref, qt_ref, km_ref, k_ref, qa_ref, ka_ref, *, nb, tq, n_heads):
    t = pl.program_id(0)
    col = t * tq + lax.broadcasted_iota(I32, (nb, tq), 1)
    qblk = col // MOBA_BLOCK
    j = lax.broadcasted_iota(I32, (nb, tq), 0)
    neg_inf = jnp.float32(-jnp.inf)
    past = j < qblk
    own = j == qblk
    n_extra = AUG_DIM - HEAD_DIM - nb
    r = lax.broadcasted_iota(I32, (n_extra, tq), 0)
    qb = ((t * tq + lax.broadcasted_iota(I32, (n_extra, tq), 1)) // MOBA_BLOCK).astype(F32)
    extra_unit = jnp.where(r == 0, 1.0,
                           jnp.where(r == 1, float(MOBA_BLOCK),
                                     jnp.where(r == 2, -float(MOBA_BLOCK) * qb, 0.0)))
    n_aug = AUG_DIM - HEAD_DIM
    pos = t * tq + lax.broadcasted_iota(I32, (tq, n_aug), 0)
    kblk = pos // MOBA_BLOCK
    pib = pos % MOBA_BLOCK
    lane = lax.broadcasted_iota(I32, (tq, n_aug), 1)
    aug = jnp.where(lane < nb, (lane == kblk).astype(F32),
                    jnp.where(lane == nb, pib.astype(F32),
                              jnp.where(lane == nb + 1, kblk.astype(F32),
                                        jnp.where(lane == nb + 2, 1.0, 0.0)))).astype(BF16)
    scale = HEAD_DIM ** -0.5
    for h in range(n_heads):
        cols = slice(h * HEAD_DIM, (h + 1) * HEAD_DIM)
        q_t = qt_ref[cols, :].astype(F32)
        gate = jnp.dot(km_ref[:, cols], q_t, preferred_element_type=F32,
                       precision=lax.Precision.HIGHEST)
        gcur = jnp.where(past, gate, neg_inf)
        sel = own
        for _ in range(MOBA_TOPK):
            m = jnp.max(gcur, axis=0, keepdims=True)
            is_max = jnp.logical_and(gcur == m, m > neg_inf)
            idx = jnp.min(jnp.where(is_max, j, nb), axis=0, keepdims=True)
            pick = j == idx
            sel = jnp.logical_or(sel, pick)
            gcur = jnp.where(pick, neg_inf, gcur)
        qa_ref[h, 0:HEAD_DIM, :] = (q_t * scale).astype(BF16)
        qa_ref[h, HEAD_DIM:HEAD_DIM + nb, :] = jnp.where(sel, 0.0, MASK_NEG).astype(BF16)
        qa_ref[h, HEAD_DIM + nb:AUG_DIM, :] = (slopes_ref[h] * extra_unit).astype(BF16)
        ka_ref[h, :, 0:HEAD_DIM] = k_ref[:, cols]
        ka_ref[h, :, HEAD_DIM:AUG_DIM] = aug


def _select(slopes, qv_t, kmean, kgg, n_heads, nb, tq):
    s = kgg.shape[0]
    g = n_heads * HEAD_DIM
    return pl.pallas_call(
        functools.partial(_select_kernel, nb=nb, tq=tq, n_heads=n_heads),
        out_shape=(jax.ShapeDtypeStruct((n_heads, AUG_DIM, s), BF16),
                   jax.ShapeDtypeStruct((n_heads, s, AUG_DIM), BF16)),
        grid_spec=pltpu.PrefetchScalarGridSpec(
            num_scalar_prefetch=1,
            grid=(s // tq,),
            in_specs=[pl.BlockSpec((g, tq), lambda t, sl: (0, t)),
                      pl.BlockSpec((nb, g), lambda t, sl: (0, 0)),
                      pl.BlockSpec((tq, g), lambda t, sl: (t, 0))],
            out_specs=[pl.BlockSpec((n_heads, AUG_DIM, tq), lambda t, sl: (0, 0, t)),
                       pl.BlockSpec((n_heads, tq, AUG_DIM), lambda t, sl: (0, t, 0))]),
        compiler_params=_cparams(("arbitrary",)),
        name="moba_select",
    )(slopes, qv_t, kmean, kgg)


def _attn_kernel(qa_ref, ka_ref, vt_ref, o_ref, *, heads):
    i = pl.program_id(1)
    tq = MOBA_BLOCK
    neg_inf = jnp.float32(-jnp.inf)

    def step(j, carries, diagonal, nkeys=MOBA_BLOCK):
        off = pl.multiple_of(j * MOBA_BLOCK, MOBA_BLOCK)
        scores = [jnp.dot(ka_ref[hh, pl.ds(off, nkeys), :], qa_ref[hh],
                          preferred_element_type=F32) for hh in range(heads)]
        stats = []
        for hh in range(heads):
            m = carries[hh][0]
            s_t = scores[hh]
            if diagonal:
                key = lax.broadcasted_iota(I32, s_t.shape, 0)
                qry = lax.broadcasted_iota(I32, s_t.shape, 1)
                s_t = jnp.where(key <= qry, s_t, neg_inf)
            m_new = jnp.maximum(m, jnp.max(s_t, axis=0, keepdims=True))
            stats.append((m_new, jnp.exp(m - m_new), jnp.exp(s_t - m_new).astype(BF16)))
        out = []
        ones = jnp.ones((ONES_ROWS, nkeys), BF16)
        for hh in range(heads):
            m_new, alpha, p = stats[hh]
            vb = vt_ref[hh * HEAD_DIM:(hh + 1) * HEAD_DIM, pl.ds(off, nkeys)]
            vb1 = jnp.concatenate([vb, ones], axis=0)
            acc_new = alpha * carries[hh][1] + jnp.dot(vb1, p, preferred_element_type=F32)
            out.append((m_new, acc_new))
        return tuple(out)

    init = tuple((jnp.full((1, tq), neg_inf, F32), jnp.zeros((HEAD_DIM + ONES_ROWS, tq), F32))
                 for _ in range(heads))
    carries = step(i, init, True)
    pair = 2 * MOBA_BLOCK
    carries = lax.fori_loop(0, i // 2, lambda c, cs: step(2 * c, cs, False, pair), carries)
    carries = lax.cond(i % 2 == 1, lambda cs: step(i - 1, cs, False), lambda cs: cs, carries)
    for hh in range(heads):
        acc = carries[hh][1]
        out_t = acc[0:HEAD_DIM, :] / acc[HEAD_DIM:HEAD_DIM + 1, :]
        o_ref[:, hh * HEAD_DIM:(hh + 1) * HEAD_DIM] = out_t.T.astype(o_ref.dtype)


def _attention(q_aug, k_aug, qv_t, n_heads, nb, heads):
    s = k_aug.shape[1]
    groups = n_heads // heads
    once = pl.Buffered(1)
    return pl.pallas_call(
        functools.partial(_attn_kernel, heads=heads),
        out_shape=jax.ShapeDtypeStruct((s, n_heads * HEAD_DIM), BF16),
        grid=(groups, nb),
        in_specs=[pl.BlockSpec((heads, AUG_DIM, MOBA_BLOCK), lambda hg, i: (hg, 0, i)),
                  pl.BlockSpec((heads, s, AUG_DIM), lambda hg, i: (hg, 0, 0), pipeline_mode=once),
                  pl.BlockSpec((heads * HEAD_DIM, s), lambda hg, i: (groups + hg, 0),
                               pipeline_mode=once)],
        out_specs=pl.BlockSpec((MOBA_BLOCK, heads * HEAD_DIM), lambda hg, i: (i, hg)),
        compiler_params=_cparams(("arbitrary", "arbitrary")),
        name="moba_attention",
    )(q_aug, k_aug, qv_t)


def _conv_kernel(ga_ref, gb_ref, w_ref, b_ref, g_ref, beta_ref, o_ref, u_ref, ush_ref, *, ts, sub):
    t = pl.program_id(0)

    @pl.when(t == 0)
    def _():
        u_ref[0:CONV_HALO, :] = jnp.zeros((CONV_HALO, u_ref.shape[1]), F32)

    @pl.when(t > 0)
    def _():
        u_ref[0:CONV_HALO, :] = u_ref[ts:ts + CONV_HALO, :]

    gb = gb_ref[...].astype(F32)
    u_ref[CONV_HALO:CONV_HALO + ts, :] = ga_ref[...].astype(F32) * jax.nn.sigmoid(gb)
    for b in range(SUBLANES):
        ush_ref[b, 0:ts + CONV_HALO - b, :] = u_ref[b:ts + CONV_HALO, :]
    first = CONV_HALO - (CONV_WIDTH - 1)
    for r0 in range(0, ts, sub):
        acc = jnp.broadcast_to(b_ref[...], (sub, u_ref.shape[1]))
        for tap in range(CONV_WIDTH):
            off = first + tap
            aligned = r0 + off - off % SUBLANES
            acc = acc + w_ref[tap:tap + 1, :] * ush_ref[off % SUBLANES, aligned:aligned + sub, :]
        mu = jnp.mean(acc, axis=-1, keepdims=True)
        cen = acc - mu
        var = jnp.mean(cen * cen, axis=-1, keepdims=True)
        y = cen * lax.rsqrt(var + LN_EPS) * g_ref[...] + beta_ref[...]
        o_ref[r0:r0 + sub, :] = (y * jax.nn.sigmoid(y)).astype(o_ref.dtype)


def _conformer_conv(kgg, conv_w, conv_b, conv_ln_g, conv_ln_b, g, ts):
    s = kgg.shape[0]
    row = lambda v: v.reshape(1, g).astype(F32)
    return pl.pallas_call(
        functools.partial(_conv_kernel, ts=ts, sub=32),
        out_shape=jax.ShapeDtypeStruct((s, g), BF16),
        grid=(s // ts,),
        in_specs=[pl.BlockSpec((ts, g), lambda t: (t, 1)),
                  pl.BlockSpec((ts, g), lambda t: (t, 2)),
                  pl.BlockSpec((CONV_WIDTH, g), lambda t: (0, 0)),
                  pl.BlockSpec((1, g), lambda t: (0, 0)),
                  pl.BlockSpec((1, g), lambda t: (0, 0)),
                  pl.BlockSpec((1, g), lambda t: (0, 0))],
        out_specs=pl.BlockSpec((ts, g), lambda t: (t, 0)),
        scratch_shapes=[pltpu.VMEM((ts + CONV_HALO, g), F32),
                        pltpu.VMEM((SUBLANES, ts + CONV_HALO, g), F32)],
        compiler_params=_cparams(("arbitrary",)),
        name="conformer_conv",
    )(kgg, kgg, conv_w, row(conv_b), row(conv_ln_g), row(conv_ln_b))


def _layer_norm_rows(z, gain, bias):
    mu = jnp.mean(z, axis=-1, keepdims=True)
    cen = z - mu
    var = jnp.mean(cen * cen, axis=-1, keepdims=True)
    return cen * lax.rsqrt(var + LN_EPS) * gain + bias


HI_HALF = -65536


def _pack_halves(lo, hi):
    lo_bits = pltpu.bitcast(lo.astype(BF16).astype(F32), I32)
    hi_bits = pltpu.bitcast(hi.astype(BF16).astype(F32), I32)
    return lax.shift_right_logical(lo_bits, jnp.int32(16)) | (hi_bits & jnp.int32(HI_HALF))


def _unpack_halves(words):
    lo = pltpu.bitcast(lax.shift_left(words, jnp.int32(16)), F32)
    hi = pltpu.bitcast(words & jnp.int32(HI_HALF), F32)
    return lo, hi


def _outproj_kernel(attn_ref, conv_ref, wo_ref, x_ref, g_ref, b_ref, wr_ref, br_ref,
                    h_ref, hp_ref, lg_ref, *, g):
    mix = jnp.dot(attn_ref[...], wo_ref[0:g, :], preferred_element_type=F32)
    mix = mix + jnp.dot(conv_ref[...], wo_ref[g:2 * g, :], preferred_element_type=F32)
    h1 = _layer_norm_rows(DEEPNORM_ALPHA * x_ref[...] + mix, g_ref[...], b_ref[...])
    h_ref[...] = h1
    half = h1.shape[1] // 2
    hp_ref[...] = _pack_halves(h1[:, 0:half], h1[:, half:])
    h_hi = h1.astype(BF16)
    h_lo = (h1 - h_hi.astype(F32)).astype(BF16)
    wr = wr_ref[...]
    w_hi = wr.astype(BF16)
    w_lo = (wr - w_hi.astype(F32)).astype(BF16)
    nt = (((1,), (1,)), ((), ()))
    lg = lax.dot_general(w_hi, h_hi, nt, preferred_element_type=F32)
    lg = lg + lax.dot_general(w_hi, h_lo, nt, preferred_element_type=F32)
    lg = lg + lax.dot_general(w_lo, h_hi, nt, preferred_element_type=F32)
    lg_ref[...] = lg + br_ref[...]


def _out_projection(attn, conv, w_out, x2, ln_g, ln_b, w_router, b_router, tm):
    s, d = x2.shape
    g = attn.shape[1]
    e = w_router.shape[1]
    return pl.pallas_call(
        functools.partial(_outproj_kernel, g=g),
        out_shape=(jax.ShapeDtypeStruct((s, d), F32), jax.ShapeDtypeStruct((s, d // 2), I32),
                   jax.ShapeDtypeStruct((e, s), F32)),
        grid=(s // tm,),
        in_specs=[pl.BlockSpec((tm, g), lambda m: (m, 0)),
                  pl.BlockSpec((tm, g), lambda m: (m, 0)),
                  pl.BlockSpec((2 * g, d), lambda m: (0, 0)),
                  pl.BlockSpec((tm, d), lambda m: (m, 0)),
                  pl.BlockSpec((1, d), lambda m: (0, 0)),
                  pl.BlockSpec((1, d), lambda m: (0, 0)),
                  pl.BlockSpec((e, d), lambda m: (0, 0)),
                  pl.BlockSpec((e, 1), lambda m: (0, 0))],
        out_specs=[pl.BlockSpec((tm, d), lambda m: (m, 0)),
                   pl.BlockSpec((tm, d // 2), lambda m: (m, 0)),
                   pl.BlockSpec((e, tm), lambda m: (0, m))],
        compiler_params=_cparams(("arbitrary",)),
        name="outproj_ln1_router",
    )(attn, conv, w_out.astype(BF16), x2, ln_g.reshape(1, d), ln_b.reshape(1, d),
      w_router.T, b_router.reshape(e, 1))


def _route_kernel(lg_ref, e_ref, gate_ref, rank_ref, cnt_ref, carry_ref, *, n_exp, tr):
    t = pl.program_id(0)

    @pl.when(t == 0)
    def _():
        carry_ref[...] = jnp.zeros_like(carry_ref)

    neg_inf = jnp.float32(-jnp.inf)
    cur = lg_ref[...]
    j = lax.broadcasted_iota(I32, (n_exp, tr), 0)
    vals, picks = [], []
    for r in range(TOP_K):
        m = jnp.max(cur, axis=0, keepdims=True)
        idx = jnp.min(jnp.where(cur == m, j, n_exp), axis=0, keepdims=True)
        pick = j == idx
        e_ref[r:r + 1, :] = idx
        vals.append(m)
        picks.append(pick)
        cur = jnp.where(pick, neg_inf, cur)
    exps = [jnp.exp(v - vals[0]) for v in vals]
    den = exps[0]
    for r in range(1, TOP_K):
        den = den + exps[r]
    for r in range(TOP_K):
        gate_ref[r:r + 1, :] = exps[r] / den

    chosen = picks[0].astype(F32)
    for r in range(1, TOP_K):
        chosen = chosen + picks[r].astype(F32)
    a = lax.broadcasted_iota(I32, (tr, tr), 0)
    b = lax.broadcasted_iota(I32, (tr, tr), 1)
    upper = (a < b).astype(BF16)
    excl = jnp.dot(chosen.astype(BF16), upper, preferred_element_type=F32)
    base = carry_ref[:, 0:1]
    rank = excl + base
    for r in range(TOP_K):
        rank_ref[r:r + 1, :] = jnp.sum(jnp.where(picks[r], rank, 0.0), axis=0,
                                       keepdims=True).astype(I32)
    total = base + jnp.sum(chosen, axis=1, keepdims=True)
    carry_ref[...] = jnp.broadcast_to(total, carry_ref.shape)
    cnt_ref[...] = jnp.broadcast_to(total, cnt_ref.shape).astype(I32)


def _route(logits_t, tr):
    n_exp, s = logits_t.shape
    return pl.pallas_call(
        functools.partial(_route_kernel, n_exp=n_exp, tr=tr),
        out_shape=(jax.ShapeDtypeStruct((TOP_K, s), I32), jax.ShapeDtypeStruct((TOP_K, s), F32),
                   jax.ShapeDtypeStruct((TOP_K, s), I32), jax.ShapeDtypeStruct((n_exp, 128), I32)),
        grid=(s // tr,),
        in_specs=[pl.BlockSpec((n_exp, tr), lambda t: (0, t))],
        out_specs=[pl.BlockSpec((TOP_K, tr), lambda t: (0, t)),
                   pl.BlockSpec((TOP_K, tr), lambda t: (0, t)),
                   pl.BlockSpec((TOP_K, tr), lambda t: (0, t)),
                   pl.BlockSpec((n_exp, 128), lambda t: (0, 0))],
        scratch_shapes=[pltpu.VMEM((n_exp, 128), F32)],
        compiler_params=_cparams(("arbitrary",)),
        name="moe_route",
    )(logits_t)


SC_CORES = 2
SC_SUBCORES = 16
SC_LANES = 16
SC_GATHER_ROWS = 32
SC_SCAN = 2048


_SC_PARAMS = pltpu.CompilerParams(needs_layout_passes=False)


def _sc_mesh():
    return plsc.VectorSubcoreMesh(core_axis_name="c", subcore_axis_name="s",
                                  num_cores=SC_CORES, num_subcores=SC_SUBCORES)


def _sc_worker():
    return lax.axis_index("s") * SC_CORES + lax.axis_index("c")


def _sc_gather_loop(table_hbm, idx_all, out_hbm, bufs, sems, base, n_chunks):
    chunk = SC_GATHER_ROWS

    def gather(ci, b):
        return pltpu.make_async_copy(
            table_hbm.at[idx_all.at[pl.ds(ci * chunk, chunk)]], bufs[b], sems[b])

    gather(0, 0).start()

    @pl.loop(0, n_chunks, step=2)
    def _(ci):
        gather(ci + 1, 1).start()
        gather(ci, 0).wait()
        pltpu.sync_copy(bufs[0], out_hbm.at[pl.ds(base + ci * chunk, chunk)])

        @pl.when(ci + 2 < n_chunks)
        def _():
            gather(ci + 2, 0).start()

        gather(ci + 1, 1).wait()
        pltpu.sync_copy(bufs[1], out_hbm.at[pl.ds(base + (ci + 1) * chunk, chunk)])


def _sc_dest(ps_v, e_buf, r_buf, v):
    e_vec = e_buf[pl.ds(v * SC_LANES, SC_LANES)]
    return plsc.load_gather(ps_v, [e_vec]) + r_buf[pl.ds(v * SC_LANES, SC_LANES)]


def _sc_dispatch(table, e_flat, rank_flat, pad_start, cap, s):
    n_assign, width = e_flat.shape[0], table.shape[1]
    n_workers = SC_CORES * SC_SUBCORES
    per_worker = cap // n_workers
    chunk = SC_GATHER_ROWS
    assert cap % (n_workers * 2 * chunk) == 0 and n_assign % SC_SCAN == 0
    assert per_worker % SC_LANES == 0

    @functools.partial(
        pl.kernel, mesh=_sc_mesh(), out_type=jax.ShapeDtypeStruct((cap, width), table.dtype),
        scratch_types=[pltpu.VMEM((per_worker,), I32), pltpu.VMEM((pad_start.shape[0],), I32),
                       pltpu.VMEM((SC_SCAN,), I32), pltpu.VMEM((SC_SCAN,), I32),
                       pltpu.VMEM((chunk, width), table.dtype),
                       pltpu.VMEM((chunk, width), table.dtype),
                       pltpu.SemaphoreType.DMA, pltpu.SemaphoreType.DMA],
        compiler_params=_SC_PARAMS, name="sc_dispatch")
    def dispatch(table_hbm, e_hbm, r_hbm, ps_hbm, out_hbm, idx_all, ps_v, e_buf, r_buf,
                 buf0, buf1, sem0, sem1):
        base = _sc_worker() * per_worker
        lane = lax.iota(I32, SC_LANES)
        pltpu.sync_copy(ps_hbm, ps_v)

        @pl.loop(0, per_worker // SC_LANES)
        def _(k):
            idx_all[pl.ds(k * SC_LANES, SC_LANES)] = lax.rem(base + k * SC_LANES + lane, s)

        @pl.loop(0, n_assign // SC_SCAN)
        def _(c):
            pltpu.sync_copy(e_hbm.at[pl.ds(c * SC_SCAN, SC_SCAN)], e_buf)
            pltpu.sync_copy(r_hbm.at[pl.ds(c * SC_SCAN, SC_SCAN)], r_buf)

            @pl.loop(0, SC_SCAN // SC_LANES)
            def _(v):
                loc = _sc_dest(ps_v, e_buf, r_buf, v) - base
                mine = jnp.logical_and(loc >= 0, loc < per_worker)
                tok = lax.rem(c * SC_SCAN + v * SC_LANES + lane, s)
                plsc.store_scatter(idx_all, [jnp.where(mine, loc, 0)], tok, mask=mine)

        _sc_gather_loop(table_hbm, idx_all, out_hbm, (buf0, buf1), (sem0, sem1), base,
                        per_worker // chunk)

    return dispatch(table, e_flat, rank_flat, pad_start)


def _sc_combine_gather(y, e_flat, rank_flat, pad_start):
    n_assign, width = e_flat.shape[0], y.shape[1]
    n_workers = SC_CORES * SC_SUBCORES
    per_worker = n_assign // n_workers
    chunk = SC_GATHER_ROWS
    assert n_assign % (n_workers * 2 * chunk) == 0

    @functools.partial(
        pl.kernel, mesh=_sc_mesh(), out_type=jax.ShapeDtypeStruct((n_assign, width), y.dtype),
        scratch_types=[pltpu.VMEM((per_worker,), I32), pltpu.VMEM((pad_start.shape[0],), I32),
                       pltpu.VMEM((per_worker,), I32), pltpu.VMEM((per_worker,), I32),
                       pltpu.VMEM((chunk, width), y.dtype), pltpu.VMEM((chunk, width), y.dtype),
                       pltpu.SemaphoreType.DMA, pltpu.SemaphoreType.DMA],
        compiler_params=_SC_PARAMS, name="sc_combine_gather")
    def combine(y_hbm, e_hbm, r_hbm, ps_hbm, out_hbm, idx_all, ps_v, e_buf, r_buf,
                buf0, buf1, sem0, sem1):
        base = _sc_worker() * per_worker
        pltpu.sync_copy(ps_hbm, ps_v)
        pltpu.sync_copy(e_hbm.at[pl.ds(base, per_worker)], e_buf)
        pltpu.sync_copy(r_hbm.at[pl.ds(base, per_worker)], r_buf)

        @pl.loop(0, per_worker // SC_LANES)
        def _(v):
            idx_all[pl.ds(v * SC_LANES, SC_LANES)] = _sc_dest(ps_v, e_buf, r_buf, v)

        _sc_gather_loop(y_hbm, idx_all, out_hbm, (buf0, buf1), (sem0, sem1), base,
                        per_worker // chunk)

    return combine(y, e_flat, rank_flat, pad_start)


def _expert_kernel(item_e, item_start, item_rows, xs_ref, wg_ref, wu_ref, wd_ref,
                   bg_ref, bu_ref, bd_ref, y_ref, xin_ref, acc_ref,
                   sem_x, sem_y, *, nf, n_items, big):
    w = pl.program_id(0)
    f = pl.program_id(1)
    rows = item_rows[w]
    slot = w % 2
    half = xin_ref.shape[2]
    has_next = jnp.logical_and(w + 1 < n_items, item_rows[jnp.minimum(w + 1, n_items - 1)] > 0)

    def x_copy(item, sl, c):
        src = pl.multiple_of(item_start[item] + c * ROUTE_PAD, ROUTE_PAD)
        dst = pl.multiple_of(c * ROUTE_PAD, ROUTE_PAD)
        return pltpu.make_async_copy(xs_ref.at[pl.ds(src, ROUTE_PAD)],
                                     xin_ref.at[sl, pl.ds(dst, ROUTE_PAD)], sem_x.at[sl])

    def y_copy(item, sl, c):
        src = pl.multiple_of(c * ROUTE_PAD, ROUTE_PAD)
        dst = pl.multiple_of(item_start[item] + c * ROUTE_PAD, ROUTE_PAD)
        return pltpu.make_async_copy(xin_ref.at[sl, pl.ds(src, ROUTE_PAD)],
                                     y_ref.at[pl.ds(dst, ROUTE_PAD)], sem_y)

    def for_chunks(item, fn):
        def body(c, carry):
            fn(c)
            return carry
        lax.fori_loop(0, item_rows[item] // ROUTE_PAD, body, 0)

    def compute(r0, size, first):
        lo, hi = _unpack_halves(xin_ref[slot, pl.ds(r0, size), :])
        lo = lo.astype(BF16)
        hi = hi.astype(BF16)

        def x_dot(w_ref):
            return (jnp.dot(lo, w_ref[0, 0:half, :].astype(BF16), preferred_element_type=F32)
                    + jnp.dot(hi, w_ref[0, half:2 * half, :].astype(BF16),
                              preferred_element_type=F32))

        gate = x_dot(wg_ref) + bg_ref[0]
        up = x_dot(wu_ref) + bu_ref[0]
        gate = jnp.minimum(gate, SWIGLU_LIMIT)
        up = jnp.clip(up, -SWIGLU_LIMIT, SWIGLU_LIMIT)
        act = (up + 1.0) * (gate * jax.nn.sigmoid(SWIGLU_ALPHA * gate))
        part = jnp.dot(act.astype(BF16), wd_ref[0].astype(BF16), preferred_element_type=F32)
        if first:
            acc_ref[pl.ds(r0, size), :] = part + bd_ref[0]
        else:
            acc_ref[pl.ds(r0, size), :] += part

    def compute_all(first):
        n_big = rows // big

        def body(c, carry):
            compute(pl.multiple_of(c * big, big), big, first)
            return carry
        lax.fori_loop(0, n_big, body, 0)
        done = n_big * big
        size = big // 2
        while size >= ROUTE_PAD:
            take = ((rows - done) & size) != 0
            here = done

            @pl.when(take)
            def _(here=here, size=size):
                compute(pl.multiple_of(here, ROUTE_PAD), size, first)
            done = done + jnp.where(take, size, 0)
            size //= 2

    @pl.when(rows > 0)
    def _():
        @pl.when(f == 0)
        def _():
            @pl.when(w == 0)
            def _():
                for_chunks(w, lambda c: x_copy(w, slot, c).start())

            for_chunks(w, lambda c: x_copy(w, slot, c).wait())

            @pl.when(w > 0)
            def _():
                for_chunks(w - 1, lambda c: y_copy(w - 1, 1 - slot, c).wait())

            @pl.when(has_next)
            def _():
                for_chunks(w + 1, lambda c: x_copy(w + 1, 1 - slot, c).start())

            compute_all(True)

        @pl.when(f > 0)
        def _():
            compute_all(False)

        @pl.when(f == nf - 1)
        def _():
            def pack(c):
                r0 = pl.multiple_of(c * ROUTE_PAD, ROUTE_PAD)
                xin_ref[slot, pl.ds(r0, ROUTE_PAD), :] = _pack_halves(
                    acc_ref[pl.ds(r0, ROUTE_PAD), 0:half],
                    acc_ref[pl.ds(r0, ROUTE_PAD), half:2 * half])
                y_copy(w, slot, c).start()
            for_chunks(w, pack)

            @pl.when(jnp.logical_not(has_next))
            def _():
                for_chunks(w, lambda c: y_copy(w, slot, c).wait())


def _experts(item_e, item_start, item_rows, n_active, xs, w_gate_up, b_gate_up, w_down, b_down,
             tm, tf):
    cap, half = xs.shape
    d = 2 * half
    n_exp, _, two_f = w_gate_up.shape
    ff = two_f // 2
    nf = ff // tf
    n_items = item_e.shape[0]
    big = min(512, tm)
    assert big % ROUTE_PAD == 0 and (big & (big - 1)) == 0

    def fidx(w, f, rows):
        return jnp.where(rows[w] > 0, f, nf - 1)

    return pl.pallas_call(
        functools.partial(_expert_kernel, nf=nf, n_items=n_items, big=big),
        out_shape=jax.ShapeDtypeStruct((cap, half), I32),
        grid_spec=pltpu.PrefetchScalarGridSpec(
            num_scalar_prefetch=3,
            grid=(n_active, nf),
            in_specs=[
                pl.BlockSpec(memory_space=pl.ANY),
                pl.BlockSpec((1, d, tf), lambda w, f, ie, ist, ir: (ie[w], 0, fidx(w, f, ir))),
                pl.BlockSpec((1, d, tf), lambda w, f, ie, ist, ir: (ie[w], 0, nf + fidx(w, f, ir))),
                pl.BlockSpec((1, tf, d), lambda w, f, ie, ist, ir: (ie[w], fidx(w, f, ir), 0)),
                pl.BlockSpec((1, 1, tf), lambda w, f, ie, ist, ir: (ie[w], 0, fidx(w, f, ir))),
                pl.BlockSpec((1, 1, tf), lambda w, f, ie, ist, ir: (ie[w], 0, nf + fidx(w, f, ir))),
                pl.BlockSpec((1, 1, d), lambda w, f, ie, ist, ir: (ie[w], 0, 0)),
            ],
            out_specs=pl.BlockSpec(memory_space=pl.ANY),
            scratch_shapes=[pltpu.VMEM((2, tm, half), I32), pltpu.VMEM((tm, d), F32),
                            pltpu.SemaphoreType.DMA((2,)), pltpu.SemaphoreType.DMA]),
        compiler_params=_cparams(("arbitrary", "arbitrary")),
        name="moe_experts",
    )(item_e, item_start, item_rows, xs, w_gate_up, w_gate_up, w_down,
      b_gate_up.reshape(n_exp, 1, two_f), b_gate_up.reshape(n_exp, 1, two_f),
      b_down.reshape(n_exp, 1, d))


def _combine_kernel(yg_ref, gate_ref, h_ref, g_ref, b_ref, o_ref):
    gates = gate_ref[...]
    half = yg_ref.shape[2]
    z_lo = DEEPNORM_ALPHA * h_ref[:, 0:half]
    z_hi = DEEPNORM_ALPHA * h_ref[:, half:2 * half]
    for r in range(TOP_K):
        lo, hi = _unpack_halves(yg_ref[r])
        z_lo = z_lo + gates[:, r:r + 1] * lo
        z_hi = z_hi + gates[:, r:r + 1] * hi
    inv_d = 1.0 / (2 * half)
    mu = (jnp.sum(z_lo, axis=-1, keepdims=True) + jnp.sum(z_hi, axis=-1, keepdims=True)) * inv_d
    c_lo = z_lo - mu
    c_hi = z_hi - mu
    var = (jnp.sum(c_lo * c_lo, axis=-1, keepdims=True)
           + jnp.sum(c_hi * c_hi, axis=-1, keepdims=True)) * inv_d
    rstd = lax.rsqrt(var + LN_EPS)
    o_ref[:, 0:half] = c_lo * rstd * g_ref[:, 0:half] + b_ref[:, 0:half]
    o_ref[:, half:2 * half] = c_hi * rstd * g_ref[:, half:2 * half] + b_ref[:, half:2 * half]


def _combine(yg, gates, h1, ln_g, ln_b, tc):
    s, d = h1.shape
    return pl.pallas_call(
        _combine_kernel,
        out_shape=jax.ShapeDtypeStruct((s, d), F32),
        grid=(s // tc,),
        in_specs=[pl.BlockSpec((TOP_K, tc, d // 2), lambda t: (0, t, 0)),
                  pl.BlockSpec((tc, TOP_K), lambda t: (t, 0)),
                  pl.BlockSpec((tc, d), lambda t: (t, 0)),
                  pl.BlockSpec((1, d), lambda t: (0, 0)),
                  pl.BlockSpec((1, d), lambda t: (0, 0))],
        out_specs=pl.BlockSpec((tc, d), lambda t: (t, 0)),
        compiler_params=_cparams(("arbitrary",)),
        name="moe_combine_ln2",
    )(yg, gates, h1, ln_g.reshape(1, d), ln_b.reshape(1, d))


def _tiles(s, d, ff, n_heads):
    return dict(
        proj_tm=min(512, s), select_tq=min(512, s), conv_ts=min(256, s), outproj_tm=min(512, s),
        attn_heads=min(4, n_heads),
        route_tr=min(256, s), combine_tc=min(256, s),
        expert_tm=min(1280, max(ROUTE_PAD, (s * TOP_K // 16) // ROUTE_PAD * ROUTE_PAD)),
        expert_tf=min(512, ff))


def kernel(x, w_in, conv_w, conv_b, conv_ln_g, conv_ln_b, w_out, ln1_g, ln1_b,
           w_router, b_router, w_gate_up, b_gate_up, w_down, b_down, ln2_g, ln2_b):
    b, s, d = x.shape
    assert b == 1, "the kernels treat the sequence of the single batch element as the row axis"
    g = w_in.shape[1] // 5
    n_heads = g // HEAD_DIM
    nb = s // MOBA_BLOCK
    n_exp = w_router.shape[1]
    ff = w_down.shape[1]
    assert s % MOBA_BLOCK == 0 and nb % 8 == 0 and HEAD_DIM + nb + 3 <= AUG_DIM
    tl = _tiles(s, d, ff, n_heads)
    x2 = x.reshape(s, d)

    kgg, qv_t = _in_projection(x2, w_in, g, tl["proj_tm"])
    kmean = _block_means(kgg, g, nb)
    slopes = 2.0 ** (-(8.0 / n_heads) * jnp.arange(1, n_heads + 1, dtype=F32))
    q_aug, k_aug = _select(slopes, qv_t, kmean, kgg, n_heads, nb, tl["select_tq"])
    attn = _attention(q_aug, k_aug, qv_t, n_heads, nb, tl["attn_heads"])
    conv = _conformer_conv(kgg, conv_w, conv_b, conv_ln_g, conv_ln_b, g, tl["conv_ts"])
    h1, h1_packed, logits_t = _out_projection(attn, conv, w_out, x2, ln1_g, ln1_b,
                                              w_router, b_router, tl["outproj_tm"])

    e_t, gate_t, rank_t, counts = _route(logits_t, tl["route_tr"])
    counts = counts[:, 0]
    padded = (counts + ROUTE_PAD - 1) // ROUTE_PAD * ROUTE_PAD
    pad_end = jnp.cumsum(padded)
    pad_start = (pad_end - padded).astype(I32)
    cap = s * TOP_K + n_exp * ROUTE_PAD
    e_flat = e_t.reshape(-1)
    rank_flat = rank_t.reshape(-1)
    xs = _sc_dispatch(h1_packed, e_flat, rank_flat, pad_start, cap, s)

    tm = tl["expert_tm"]
    n_items = cap // tm + n_exp
    per_e = (padded + tm - 1) // tm
    item_end = jnp.cumsum(per_e)
    item_ids = jnp.arange(n_items, dtype=I32)
    item_e = jnp.minimum(jnp.searchsorted(item_end, item_ids, side="right"), n_exp - 1).astype(I32)
    piece = item_ids - (item_end - per_e)[item_e]
    active = item_ids < item_end[-1]
    item_start = jnp.where(active, pad_start[item_e] + piece * tm, 0).astype(I32)
    item_rows = jnp.where(active, jnp.minimum(tm, padded[item_e] - piece * tm), 0).astype(I32)
    last_e = item_e[jnp.maximum(item_end[-1] - 1, 0)]
    item_e = jnp.where(active, item_e, last_e).astype(I32)

    y = _experts(item_e, item_start, item_rows, item_end[-1].astype(I32), xs,
                 w_gate_up, b_gate_up, w_down, b_down, tm, tl["expert_tf"])
    yg = _sc_combine_gather(y, e_flat, rank_flat, pad_start).reshape(TOP_K, s, d // 2)
    out = _combine(yg, gate_t.T, h1, ln2_g, ln2_b, tl["combine_tc"])
    return out.reshape(b, s, d)
```

```python
import functools

import jax
import jax.numpy as jnp
from jax import lax
from jax.experimental import pallas as pl
from jax.experimental.pallas import tpu as pltpu
from jax.experimental.pallas import tpu_sc as plsc

F32 = jnp.float32
BF16 = jnp.bfloat16
I32 = jnp.int32

HEAD_DIM = 128
MOBA_BLOCK = 256
MOBA_TOPK = 3
CONV_WIDTH = 31
CONV_HALO = 32
SUBLANES = 8
LANES = 128
TOP_K = 4
SWIGLU_ALPHA = 1.702
SWIGLU_LIMIT = 7.0
LN_EPS = 1e-5
DEPTH = 1
DEEPNORM_ALPHA = (2.0 * DEPTH) ** 0.25
ROUTE_PAD = 128
AUG_DIM = 256
MASK_NEG = -1e30
ONES_ROWS = 16
ATTN_UNIT = 4
VMEM_LIMIT = 56 * 1024 * 1024


def _cparams(sem):
    return pltpu.CompilerParams(dimension_semantics=sem, vmem_limit_bytes=VMEM_LIMIT)


def _proj_nn_kernel(x_ref, w_ref, o_ref, wb_ref):
    @pl.when(pl.program_id(1) == 0)
    def _():
        wb_ref[...] = w_ref[...].astype(BF16)

    o_ref[...] = jnp.dot(x_ref[...].astype(BF16), wb_ref[...],
                         preferred_element_type=F32).astype(o_ref.dtype)


def _proj_nt_kernel(wt_ref, x_ref, o_ref, wb_ref):
    @pl.when(pl.program_id(1) == 0)
    def _():
        wb_ref[...] = wt_ref[...].astype(BF16)

    o_ref[...] = lax.dot_general(wb_ref[...], x_ref[...].astype(BF16),
                                 (((1,), (1,)), ((), ())),
                                 preferred_element_type=F32).astype(o_ref.dtype)


def _in_projection(x2, w_in, g, tm):
    s, d = x2.shape
    kgg = pl.pallas_call(
        _proj_nn_kernel,
        out_shape=jax.ShapeDtypeStruct((s, 3 * g), BF16),
        grid=(3, s // tm),
        in_specs=[pl.BlockSpec((tm, d), lambda n, m: (m, 0)),
                  pl.BlockSpec((d, g), lambda n, m: (0, jnp.where(n == 0, 1, n + 2)))],
        out_specs=pl.BlockSpec((tm, g), lambda n, m: (m, n)),
        scratch_shapes=[pltpu.VMEM((d, g), BF16)],
        compiler_params=_cparams(("arbitrary", "arbitrary")),
        name="proj_nn",
    )(x2, w_in)
    w_qv_t = jnp.concatenate([w_in[:, :g], w_in[:, 2 * g:3 * g]], axis=1).T
    qv_t = pl.pallas_call(
        _proj_nt_kernel,
        out_shape=jax.ShapeDtypeStruct((2 * g, s), BF16),
        grid=(2, s // tm),
        in_specs=[pl.BlockSpec((g, d), lambda n, m: (n, 0)),
                  pl.BlockSpec((tm, d), lambda n, m: (m, 0))],
        out_specs=pl.BlockSpec((g, tm), lambda n, m: (n, m)),
        scratch_shapes=[pltpu.VMEM((g, d), BF16)],
        compiler_params=_cparams(("arbitrary", "arbitrary")),
        name="proj_nt",
    )(w_qv_t, x2)
    return kgg, qv_t


def _kmean_kernel(k_ref, o_ref, *, blocks):
    for b in range(blocks):
        kb = k_ref[b * MOBA_BLOCK:(b + 1) * MOBA_BLOCK, :].astype(F32)
        o_ref[b:b + 1, :] = jnp.sum(kb, axis=0, keepdims=True) * (1.0 / MOBA_BLOCK)


def _block_means(kgg, g, nb):
    blocks = 8
    return pl.pallas_call(
        functools.partial(_kmean_kernel, blocks=blocks),
        out_shape=jax.ShapeDtypeStruct((nb, g), F32),
        grid=(nb // blocks,),
        in_specs=[pl.BlockSpec((blocks * MOBA_BLOCK, g), lambda i: (i, 0))],
        out_specs=pl.BlockSpec((blocks, g), lambda i: (i, 0)),
        compiler_params=_cparams(("arbitrary",)),
        name="moba_kmean",
    )(kgg)


def _select_kernel(slopes_ref, qt_ref, km_ref, k_ref, qa_ref, ka_ref, *, nb, tq, n_heads):
    t = pl.program_id(0)
    col = t * tq + lax.broadcasted_iota(I32, (nb, tq), 1)
    qblk = col // MOBA_BLOCK
    j = lax.broadcasted_iota(I32, (nb, tq), 0)
    neg_inf = jnp.float32(-jnp.inf)
    past = j < qblk
    own = j == qblk
    n_extra = AUG_DIM - HEAD_DIM - nb
    r = lax.broadcasted_iota(I32, (n_extra, tq), 0)
    qb = ((t * tq + lax.broadcasted_iota(I32, (n_extra, tq), 1)) // MOBA_BLOCK).astype(F32)
    extra_unit = jnp.where(r == 0, 1.0,
                           jnp.where(r == 1, float(MOBA_BLOCK),
                                     jnp.where(r == 2, -float(MOBA_BLOCK) * qb, 0.0)))
    n_aug = AUG_DIM - HEAD_DIM
    pos = t * tq + lax.broadcasted_iota(I32, (tq, n_aug), 0)
    kblk = pos // MOBA_BLOCK
    pib = pos % MOBA_BLOCK
    lane = lax.broadcasted_iota(I32, (tq, n_aug), 1)
    aug = jnp.where(lane < nb, (lane == kblk).astype(F32),
                    jnp.where(lane == nb, pib.astype(F32),
                              jnp.where(lane == nb + 1, kblk.astype(F32),
                                        jnp.where(lane == nb + 2, 1.0, 0.0)))).astype(BF16)
    scale = HEAD_DIM ** -0.5
    for h in range(n_heads):
        cols = slice(h * HEAD_DIM, (h + 1) * HEAD_DIM)
        q_t = qt_ref[cols, :].astype(F32)
        gate = jnp.dot(km_ref[:, cols], q_t, preferred_element_type=F32,
                       precision=lax.Precision.HIGHEST)
        gcur = jnp.where(past, gate, neg_inf)
        sel = own
        for _ in range(MOBA_TOPK):
            m = jnp.max(gcur, axis=0, keepdims=True)
            is_max = jnp.logical_and(gcur == m, m > neg_inf)
            idx = jnp.min(jnp.where(is_max, j, nb), axis=0, keepdims=True)
            pick = j == idx
            sel = jnp.logical_or(sel, pick)
            gcur = jnp.where(pick, neg_inf, gcur)
        qa_ref[h, 0:HEAD_DIM, :] = (q_t * scale).astype(BF16)
        qa_ref[h, HEAD_DIM:HEAD_DIM + nb, :] = jnp.where(sel, 0.0, MASK_NEG).astype(BF16)
        qa_ref[h, HEAD_DIM + nb:AUG_DIM, :] = (slopes_ref[h] * extra_unit).astype(BF16)
        ka_ref[h, :, 0:HEAD_DIM] = k_ref[:, cols]
        ka_ref[h, :, HEAD_DIM:AUG_DIM] = aug


def _select(slopes, qv_t, kmean, kgg, n_heads, nb, tq):
    s = kgg.shape[0]
    g = n_heads * HEAD_DIM
    return pl.pallas_call(
        functools.partial(_select_kernel, nb=nb, tq=tq, n_heads=n_heads),
        out_shape=(jax.ShapeDtypeStruct((n_heads, AUG_DIM, s), BF16),
                   jax.ShapeDtypeStruct((n_heads, s, AUG_DIM), BF16)),
        grid_spec=pltpu.PrefetchScalarGridSpec(
            num_scalar_prefetch=1,
            grid=(s // tq,),
            in_specs=[pl.BlockSpec((g, tq), lambda t, sl: (0, t)),
                      pl.BlockSpec((nb, g), lambda t, sl: (0, 0)),
                      pl.BlockSpec((tq, g), lambda t, sl: (t, 0))],
            out_specs=[pl.BlockSpec((n_heads, AUG_DIM, tq), lambda t, sl: (0, 0, t)),
                       pl.BlockSpec((n_heads, tq, AUG_DIM), lambda t, sl: (0, t, 0))]),
        compiler_params=_cparams(("arbitrary",)),
        name="moba_select",
    )(slopes, qv_t, kmean, kgg)


def _attn_kernel(qa_ref, ka_ref, vt_ref, o_ref, sa_ref, sb_ref, *, heads):
    i = pl.program_id(1)
    tq = MOBA_BLOCK
    unit_keys = ATTN_UNIT * MOBA_BLOCK
    neg_inf = jnp.float32(-jnp.inf)

    def scores_of(j, nkeys):
        off = pl.multiple_of(j * MOBA_BLOCK, MOBA_BLOCK)
        return [jnp.dot(ka_ref[hh, pl.ds(off, nkeys), :], qa_ref[hh],
                        preferred_element_type=F32) for hh in range(heads)]

    def scores_into(ref, unit):
        for hh, s_t in enumerate(scores_of(ATTN_UNIT * unit, unit_keys)):
            ref[hh] = s_t

    def update(j, scores, carries, diagonal, nkeys):
        off = pl.multiple_of(j * MOBA_BLOCK, MOBA_BLOCK)
        stats = []
        for hh in range(heads):
            m = carries[hh][0]
            s_t = scores[hh]
            if diagonal:
                key = lax.broadcasted_iota(I32, s_t.shape, 0)
                qry = lax.broadcasted_iota(I32, s_t.shape, 1)
                s_t = jnp.where(key <= qry, s_t, neg_inf)
            m_new = jnp.maximum(m, jnp.max(s_t, axis=0, keepdims=True))
            stats.append((m_new, jnp.exp(m - m_new), jnp.exp(s_t - m_new).astype(BF16)))
        out = []
        ones = jnp.ones((ONES_ROWS, nkeys), BF16)
        for hh in range(heads):
            m_new, alpha, p = stats[hh]
            vb = vt_ref[hh * HEAD_DIM:(hh + 1) * HEAD_DIM, pl.ds(off, nkeys)]
            vb1 = jnp.concatenate([vb, ones], axis=0)
            acc_new = alpha * carries[hh][1] + jnp.dot(vb1, p, preferred_element_type=F32)
            out.append((m_new, acc_new))
        return tuple(out)

    init = tuple((jnp.full((1, tq), neg_inf, F32), jnp.zeros((HEAD_DIM + ONES_ROWS, tq), F32))
                 for _ in range(heads))
    carries = update(i, scores_of(i, MOBA_BLOCK), init, True, MOBA_BLOCK)

    n_units = i // ATTN_UNIT
    last_unit = jnp.maximum(n_units - 1, 0)

    @pl.when(n_units > 0)
    def _():
        scores_into(sa_ref, 0)

    def from_ref(ref):
        return [ref[hh] for hh in range(heads)]

    def two_units(k, cs):
        scores_into(sb_ref, 2 * k + 1)
        cs = update(2 * ATTN_UNIT * k, from_ref(sa_ref), cs, False, unit_keys)
        scores_into(sa_ref, jnp.minimum(2 * k + 2, last_unit))
        return update(2 * ATTN_UNIT * k + ATTN_UNIT, from_ref(sb_ref), cs, False, unit_keys)

    carries = lax.fori_loop(0, n_units // 2, two_units, carries)
    carries = lax.cond(
        n_units % 2 == 1,
        lambda cs: update(ATTN_UNIT * last_unit, from_ref(sa_ref), cs, False, unit_keys),
        lambda cs: cs, carries)
    done = n_units * ATTN_UNIT
    blocks = ATTN_UNIT // 2
    while blocks >= 1:
        take = ((i - done) & blocks) != 0
        carries = lax.cond(
            take,
            lambda cs, done=done, blocks=blocks: update(
                done, scores_of(done, blocks * MOBA_BLOCK), cs, False, blocks * MOBA_BLOCK),
            lambda cs: cs, carries)
        done = done + jnp.where(take, blocks, 0)
        blocks //= 2
    for hh in range(heads):
        acc = carries[hh][1]
        out_t = acc[0:HEAD_DIM, :] / acc[HEAD_DIM:HEAD_DIM + 1, :]
        o_ref[:, hh * HEAD_DIM:(hh + 1) * HEAD_DIM] = out_t.T.astype(o_ref.dtype)


def _attention(q_aug, k_aug, qv_t, n_heads, nb, heads):
    s = k_aug.shape[1]
    groups = n_heads // heads
    once = pl.Buffered(1)
    return pl.pallas_call(
        functools.partial(_attn_kernel, heads=heads),
        out_shape=jax.ShapeDtypeStruct((s, n_heads * HEAD_DIM), BF16),
        grid=(groups, nb),
        in_specs=[pl.BlockSpec((heads, AUG_DIM, MOBA_BLOCK), lambda hg, i: (hg, 0, i)),
                  pl.BlockSpec((heads, s, AUG_DIM), lambda hg, i: (hg, 0, 0), pipeline_mode=once),
                  pl.BlockSpec((heads * HEAD_DIM, s), lambda hg, i: (groups + hg, 0),
                               pipeline_mode=once)],
        out_specs=pl.BlockSpec((MOBA_BLOCK, heads * HEAD_DIM), lambda hg, i: (i, hg)),
        scratch_shapes=[pltpu.VMEM((heads, ATTN_UNIT * MOBA_BLOCK, MOBA_BLOCK), F32),
                        pltpu.VMEM((heads, ATTN_UNIT * MOBA_BLOCK, MOBA_BLOCK), F32)],
        compiler_params=_cparams(("arbitrary", "arbitrary")),
        name="moba_attention",
    )(q_aug, k_aug, qv_t)


def _conv_kernel(ga_ref, gb_ref, w_ref, b_ref, g_ref, beta_ref, o_ref, u_ref, ush_ref, *, ts, sub):
    t = pl.program_id(0)

    @pl.when(t == 0)
    def _():
        u_ref[0:CONV_HALO, :] = jnp.zeros((CONV_HALO, u_ref.shape[1]), F32)

    @pl.when(t > 0)
    def _():
        u_ref[0:CONV_HALO, :] = u_ref[ts:ts + CONV_HALO, :]

    gb = gb_ref[...].astype(F32)
    u_ref[CONV_HALO:CONV_HALO + ts, :] = ga_ref[...].astype(F32) * jax.nn.sigmoid(gb)
    for b in range(SUBLANES):
        ush_ref[b, 0:ts + CONV_HALO - b, :] = u_ref[b:ts + CONV_HALO, :]
    first = CONV_HALO - (CONV_WIDTH - 1)
    for r0 in range(0, ts, sub):
        acc = jnp.broadcast_to(b_ref[...], (sub, u_ref.shape[1]))
        for tap in range(CONV_WIDTH):
            off = first + tap
            aligned = r0 + off - off % SUBLANES
            acc = acc + w_ref[tap:tap + 1, :] * ush_ref[off % SUBLANES, aligned:aligned + sub, :]
        mu = jnp.mean(acc, axis=-1, keepdims=True)
        cen = acc - mu
        var = jnp.mean(cen * cen, axis=-1, keepdims=True)
        y = cen * lax.rsqrt(var + LN_EPS) * g_ref[...] + beta_ref[...]
        o_ref[r0:r0 + sub, :] = (y * jax.nn.sigmoid(y)).astype(o_ref.dtype)


def _conformer_conv(kgg, conv_w, conv_b, conv_ln_g, conv_ln_b, g, ts):
    s = kgg.shape[0]
    row = lambda v: v.reshape(1, g).astype(F32)
    return pl.pallas_call(
        functools.partial(_conv_kernel, ts=ts, sub=32),
        out_shape=jax.ShapeDtypeStruct((s, g), BF16),
        grid=(s // ts,),
        in_specs=[pl.BlockSpec((ts, g), lambda t: (t, 1)),
                  pl.BlockSpec((ts, g), lambda t: (t, 2)),
                  pl.BlockSpec((CONV_WIDTH, g), lambda t: (0, 0)),
                  pl.BlockSpec((1, g), lambda t: (0, 0)),
                  pl.BlockSpec((1, g), lambda t: (0, 0)),
                  pl.BlockSpec((1, g), lambda t: (0, 0))],
        out_specs=pl.BlockSpec((ts, g), lambda t: (t, 0)),
        scratch_shapes=[pltpu.VMEM((ts + CONV_HALO, g), F32),
                        pltpu.VMEM((SUBLANES, ts + CONV_HALO, g), F32)],
        compiler_params=_cparams(("arbitrary",)),
        name="conformer_conv",
    )(kgg, kgg, conv_w, row(conv_b), row(conv_ln_g), row(conv_ln_b))


def _layer_norm_rows(z, gain, bias):
    mu = jnp.mean(z, axis=-1, keepdims=True)
    cen = z - mu
    var = jnp.mean(cen * cen, axis=-1, keepdims=True)
    return cen * lax.rsqrt(var + LN_EPS) * gain + bias


HI_HALF = -65536


def _pack_halves(lo, hi):
    lo_bits = pltpu.bitcast(lo.astype(BF16).astype(F32), I32)
    hi_bits = pltpu.bitcast(hi.astype(BF16).astype(F32), I32)
    return lax.shift_right_logical(lo_bits, jnp.int32(16)) | (hi_bits & jnp.int32(HI_HALF))


def _unpack_halves(words):
    lo = pltpu.bitcast(lax.shift_left(words, jnp.int32(16)), F32)
    hi = pltpu.bitcast(words & jnp.int32(HI_HALF), F32)
    return lo, hi


def _outproj_kernel(attn_ref, conv_ref, wo_ref, x_ref, g_ref, b_ref, wr_ref, br_ref,
                    h_ref, hp_ref, lg_ref, *, g):
    tm = x_ref.shape[0]
    sub = min(tm, 256)
    half = x_ref.shape[1] // 2
    n_exp = lg_ref.shape[0]
    wr = wr_ref[...]
    w_hi = wr.astype(BF16)
    w_lo = (wr - w_hi.astype(F32)).astype(BF16)
    mixes = []
    for r0 in range(0, tm, sub):
        mix = jnp.dot(attn_ref[r0:r0 + sub, :], wo_ref[0:g, :], preferred_element_type=F32)
        mixes.append(mix + jnp.dot(conv_ref[r0:r0 + sub, :], wo_ref[g:2 * g, :],
                                   preferred_element_type=F32))
    for k, r0 in enumerate(range(0, tm, sub)):
        h1 = _layer_norm_rows(DEEPNORM_ALPHA * x_ref[r0:r0 + sub, :] + mixes[k],
                              g_ref[...], b_ref[...])
        h_ref[r0:r0 + sub, :] = h1
        hp_ref[r0:r0 + sub, :] = _pack_halves(h1[:, 0:half], h1[:, half:])
        h_hi = h1.astype(BF16)
        h_lo = (h1 - h_hi.astype(F32)).astype(BF16)
        lg = jnp.dot(h_hi, w_hi, preferred_element_type=F32)
        lg = lg + jnp.dot(h_lo, w_hi, preferred_element_type=F32)
        lg = lg + jnp.dot(h_hi, w_lo, preferred_element_type=F32)
        lg_ref[:, r0:r0 + sub] = lg.T[0:n_exp, :] + br_ref[...]


def _out_projection(attn, conv, w_out, x2, ln_g, ln_b, w_router, b_router, tm):
    s, d = x2.shape
    g = attn.shape[1]
    e = w_router.shape[1]
    return pl.pallas_call(
        functools.partial(_outproj_kernel, g=g),
        out_shape=(jax.ShapeDtypeStruct((s, d), F32), jax.ShapeDtypeStruct((s, d // 2), I32),
                   jax.ShapeDtypeStruct((e, s), F32)),
        grid=(s // tm,),
        in_specs=[pl.BlockSpec((tm, g), lambda m: (m, 0)),
                  pl.BlockSpec((tm, g), lambda m: (m, 0)),
                  pl.BlockSpec((2 * g, d), lambda m: (0, 0)),
                  pl.BlockSpec((tm, d), lambda m: (m, 0)),
                  pl.BlockSpec((1, d), lambda m: (0, 0)),
                  pl.BlockSpec((1, d), lambda m: (0, 0)),
                  pl.BlockSpec((d, LANES), lambda m: (0, 0)),
                  pl.BlockSpec((e, 1), lambda m: (0, 0))],
        out_specs=[pl.BlockSpec((tm, d), lambda m: (m, 0)),
                   pl.BlockSpec((tm, d // 2), lambda m: (m, 0)),
                   pl.BlockSpec((e, tm), lambda m: (0, m))],
        compiler_params=_cparams(("arbitrary",)),
        name="outproj_ln1_router",
    )(attn, conv, w_out.astype(BF16), x2, ln_g.reshape(1, d), ln_b.reshape(1, d),
      jnp.pad(w_router, ((0, 0), (0, LANES - e))), b_router.reshape(e, 1))


def _route_kernel(lg_ref, e_ref, gate_ref, rank_ref, cnt_ref, carry_ref, *, n_exp, tr):
    t = pl.program_id(0)

    @pl.when(t == 0)
    def _():
        carry_ref[...] = jnp.zeros_like(carry_ref)

    neg_inf = jnp.float32(-jnp.inf)
    cur = lg_ref[...]
    j = lax.broadcasted_iota(I32, (n_exp, tr), 0)
    vals, picks = [], []
    for r in range(TOP_K):
        m = jnp.max(cur, axis=0, keepdims=True)
        idx = jnp.min(jnp.where(cur == m, j, n_exp), axis=0, keepdims=True)
        pick = j == idx
        e_ref[r:r + 1, :] = idx
        vals.append(m)
        picks.append(pick)
        cur = jnp.where(pick, neg_inf, cur)
    exps = [jnp.exp(v - vals[0]) for v in vals]
    den = exps[0]
    for r in range(1, TOP_K):
        den = den + exps[r]
    for r in range(TOP_K):
        gate_ref[r:r + 1, :] = exps[r] / den

    chosen = picks[0].astype(F32)
    for r in range(1, TOP_K):
        chosen = chosen + picks[r].astype(F32)
    a = lax.broadcasted_iota(I32, (tr, tr), 0)
    b = lax.broadcasted_iota(I32, (tr, tr), 1)
    upper = (a < b).astype(BF16)
    excl = jnp.dot(chosen.astype(BF16), upper, preferred_element_type=F32)
    base = carry_ref[:, 0:1]
    rank = excl + base
    for r in range(TOP_K):
        rank_ref[r:r + 1, :] = jnp.sum(jnp.where(picks[r], rank, 0.0), axis=0,
                                       keepdims=True).astype(I32)
    total = base + jnp.sum(chosen, axis=1, keepdims=True)
    carry_ref[...] = jnp.broadcast_to(total, carry_ref.shape)
    cnt_ref[...] = jnp.broadcast_to(total, cnt_ref.shape).astype(I32)


def _route(logits_t, tr):
    n_exp, s = logits_t.shape
    return pl.pallas_call(
        functools.partial(_route_kernel, n_exp=n_exp, tr=tr),
        out_shape=(jax.ShapeDtypeStruct((TOP_K, s), I32), jax.ShapeDtypeStruct((TOP_K, s), F32),
                   jax.ShapeDtypeStruct((TOP_K, s), I32), jax.ShapeDtypeStruct((n_exp, 128), I32)),
        grid=(s // tr,),
        in_specs=[pl.BlockSpec((n_exp, tr), lambda t: (0, t))],
        out_specs=[pl.BlockSpec((TOP_K, tr), lambda t: (0, t)),
                   pl.BlockSpec((TOP_K, tr), lambda t: (0, t)),
                   pl.BlockSpec((TOP_K, tr), lambda t: (0, t)),
                   pl.BlockSpec((n_exp, 128), lambda t: (0, 0))],
        scratch_shapes=[pltpu.VMEM((n_exp, 128), F32)],
        compiler_params=_cparams(("arbitrary",)),
        name="moe_route",
    )(logits_t)


SC_CORES = 2
SC_SUBCORES = 16
SC_LANES = 16
SC_GATHER_ROWS = 32
SC_SCAN = 2048


_SC_PARAMS = pltpu.CompilerParams(needs_layout_passes=False)


def _sc_mesh():
    return plsc.VectorSubcoreMesh(core_axis_name="c", subcore_axis_name="s",
                                  num_cores=SC_CORES, num_subcores=SC_SUBCORES)


def _sc_worker():
    return lax.axis_index("s") * SC_CORES + lax.axis_index("c")


def _sc_gather_loop(table_hbm, idx_all, out_hbm, bufs, sems, base, n_chunks):
    chunk = SC_GATHER_ROWS

    def gather(ci, b):
        return pltpu.make_async_copy(
            table_hbm.at[idx_all.at[pl.ds(ci * chunk, chunk)]], bufs[b], sems[b])

    gather(0, 0).start()

    @pl.loop(0, n_chunks, step=2)
    def _(ci):
        gather(ci + 1, 1).start()
        gather(ci, 0).wait()
        pltpu.sync_copy(bufs[0], out_hbm.at[pl.ds(base + ci * chunk, chunk)])

        @pl.when(ci + 2 < n_chunks)
        def _():
            gather(ci + 2, 0).start()

        gather(ci + 1, 1).wait()
        pltpu.sync_copy(bufs[1], out_hbm.at[pl.ds(base + (ci + 1) * chunk, chunk)])


def _sc_dest(ps_v, e_buf, r_buf, v):
    e_vec = e_buf[pl.ds(v * SC_LANES, SC_LANES)]
    return plsc.load_gather(ps_v, [e_vec]) + r_buf[pl.ds(v * SC_LANES, SC_LANES)]


def _sc_dispatch(table, e_flat, rank_flat, pad_start, cap, s):
    n_assign, width = e_flat.shape[0], table.shape[1]
    n_workers = SC_CORES * SC_SUBCORES
    per_worker = cap // n_workers
    chunk = SC_GATHER_ROWS
    assert cap % (n_workers * 2 * chunk) == 0 and n_assign % SC_SCAN == 0
    assert per_worker % SC_LANES == 0

    @functools.partial(
        pl.kernel, mesh=_sc_mesh(), out_type=jax.ShapeDtypeStruct((cap, width), table.dtype),
        scratch_types=[pltpu.VMEM((per_worker,), I32), pltpu.VMEM((pad_start.shape[0],), I32),
                       pltpu.VMEM((SC_SCAN,), I32), pltpu.VMEM((SC_SCAN,), I32),
                       pltpu.VMEM((chunk, width), table.dtype),
                       pltpu.VMEM((chunk, width), table.dtype),
                       pltpu.SemaphoreType.DMA, pltpu.SemaphoreType.DMA],
        compiler_params=_SC_PARAMS, name="sc_dispatch")
    def dispatch(table_hbm, e_hbm, r_hbm, ps_hbm, out_hbm, idx_all, ps_v, e_buf, r_buf,
                 buf0, buf1, sem0, sem1):
        base = _sc_worker() * per_worker
        lane = lax.iota(I32, SC_LANES)
        pltpu.sync_copy(ps_hbm, ps_v)

        @pl.loop(0, per_worker // SC_LANES)
        def _(k):
            idx_all[pl.ds(k * SC_LANES, SC_LANES)] = lax.rem(base + k * SC_LANES + lane, s)

        @pl.loop(0, n_assign // SC_SCAN)
        def _(c):
            pltpu.sync_copy(e_hbm.at[pl.ds(c * SC_SCAN, SC_SCAN)], e_buf)
            pltpu.sync_copy(r_hbm.at[pl.ds(c * SC_SCAN, SC_SCAN)], r_buf)

            @pl.loop(0, SC_SCAN // SC_LANES)
            def _(v):
                loc = _sc_dest(ps_v, e_buf, r_buf, v) - base
                mine = jnp.logical_and(loc >= 0, loc < per_worker)
                tok = lax.rem(c * SC_SCAN + v * SC_LANES + lane, s)
                plsc.store_scatter(idx_all, [jnp.where(mine, loc, 0)], tok, mask=mine)

        _sc_gather_loop(table_hbm, idx_all, out_hbm, (buf0, buf1), (sem0, sem1), base,
                        per_worker // chunk)

    return dispatch(table, e_flat, rank_flat, pad_start)


def _sc_combine_gather(y, e_flat, rank_flat, pad_start):
    n_assign, width = e_flat.shape[0], y.shape[1]
    n_workers = SC_CORES * SC_SUBCORES
    per_worker = n_assign // n_workers
    chunk = SC_GATHER_ROWS
    assert n_assign % (n_workers * 2 * chunk) == 0

    @functools.partial(
        pl.kernel, mesh=_sc_mesh(), out_type=jax.ShapeDtypeStruct((n_assign, width), y.dtype),
        scratch_types=[pltpu.VMEM((per_worker,), I32), pltpu.VMEM((pad_start.shape[0],), I32),
                       pltpu.VMEM((per_worker,), I32), pltpu.VMEM((per_worker,), I32),
                       pltpu.VMEM((chunk, width), y.dtype), pltpu.VMEM((chunk, width), y.dtype),
                       pltpu.SemaphoreType.DMA, pltpu.SemaphoreType.DMA],
        compiler_params=_SC_PARAMS, name="sc_combine_gather")
    def combine(y_hbm, e_hbm, r_hbm, ps_hbm, out_hbm, idx_all, ps_v, e_buf, r_buf,
                buf0, buf1, sem0, sem1):
        base = _sc_worker() * per_worker
        pltpu.sync_copy(ps_hbm, ps_v)
        pltpu.sync_copy(e_hbm.at[pl.ds(base, per_worker)], e_buf)
        pltpu.sync_copy(r_hbm.at[pl.ds(base, per_worker)], r_buf)

        @pl.loop(0, per_worker // SC_LANES)
        def _(v):
            idx_all[pl.ds(v * SC_LANES, SC_LANES)] = _sc_dest(ps_v, e_buf, r_buf, v)

        _sc_gather_loop(y_hbm, idx_all, out_hbm, (buf0, buf1), (sem0, sem1), base,
                        per_worker // chunk)

    return combine(y, e_flat, rank_flat, pad_start)


def _expert_kernel(item_e, item_start, item_rows, xs_ref, wg_ref, wu_ref, wd_ref,
                   bg_ref, bu_ref, bd_ref, y_ref, xin_ref, acc_ref,
                   sem_x, sem_y, *, nf, n_items, big):
    w = pl.program_id(0)
    f = pl.program_id(1)
    rows = item_rows[w]
    slot = w % 2
    half = xin_ref.shape[2]
    has_next = jnp.logical_and(w + 1 < n_items, item_rows[jnp.minimum(w + 1, n_items - 1)] > 0)

    def x_copy(item, sl, c):
        src = pl.multiple_of(item_start[item] + c * ROUTE_PAD, ROUTE_PAD)
        dst = pl.multiple_of(c * ROUTE_PAD, ROUTE_PAD)
        return pltpu.make_async_copy(xs_ref.at[pl.ds(src, ROUTE_PAD)],
                                     xin_ref.at[sl, pl.ds(dst, ROUTE_PAD)], sem_x.at[sl])

    def y_copy(item, sl, c):
        src = pl.multiple_of(c * ROUTE_PAD, ROUTE_PAD)
        dst = pl.multiple_of(item_start[item] + c * ROUTE_PAD, ROUTE_PAD)
        return pltpu.make_async_copy(xin_ref.at[sl, pl.ds(src, ROUTE_PAD)],
                                     y_ref.at[pl.ds(dst, ROUTE_PAD)], sem_y)

    def for_chunks(item, fn):
        def body(c, carry):
            fn(c)
            return carry
        lax.fori_loop(0, item_rows[item] // ROUTE_PAD, body, 0)

    def compute(r0, size, first):
        lo, hi = _unpack_halves(xin_ref[slot, pl.ds(r0, size), :])
        lo = lo.astype(BF16)
        hi = hi.astype(BF16)

        def x_dot(w_ref):
            return (jnp.dot(lo, w_ref[0, 0:half, :].astype(BF16), preferred_element_type=F32)
                    + jnp.dot(hi, w_ref[0, half:2 * half, :].astype(BF16),
                              preferred_element_type=F32))

        gate = x_dot(wg_ref) + bg_ref[0]
        up = x_dot(wu_ref) + bu_ref[0]
        gate = jnp.minimum(gate, SWIGLU_LIMIT)
        up = jnp.clip(up, -SWIGLU_LIMIT, SWIGLU_LIMIT)
        act = (up + 1.0) * (gate * jax.nn.sigmoid(SWIGLU_ALPHA * gate))
        part = jnp.dot(act.astype(BF16), wd_ref[0].astype(BF16), preferred_element_type=F32)
        if first:
            acc_ref[pl.ds(r0, size), :] = part + bd_ref[0]
        else:
            acc_ref[pl.ds(r0, size), :] += part

    def compute_all(first):
        n_big = rows // big

        def body(c, carry):
            compute(pl.multiple_of(c * big, big), big, first)
            return carry
        lax.fori_loop(0, n_big, body, 0)
        done = n_big * big
        size = big // 2
        while size >= ROUTE_PAD:
            take = ((rows - done) & size) != 0
            here = done

            @pl.when(take)
            def _(here=here, size=size):
                compute(pl.multiple_of(here, ROUTE_PAD), size, first)
            done = done + jnp.where(take, size, 0)
            size //= 2

    @pl.when(rows > 0)
    def _():
        @pl.when(f == 0)
        def _():
            @pl.when(w == 0)
            def _():
                for_chunks(w, lambda c: x_copy(w, slot, c).start())

            for_chunks(w, lambda c: x_copy(w, slot, c).wait())

            @pl.when(w > 0)
            def _():
                for_chunks(w - 1, lambda c: y_copy(w - 1, 1 - slot, c).wait())

            @pl.when(has_next)
            def _():
                for_chunks(w + 1, lambda c: x_copy(w + 1, 1 - slot, c).start())

            compute_all(True)

        @pl.when(f > 0)
        def _():
            compute_all(False)

        @pl.when(f == nf - 1)
        def _():
            def pack(c):
                r0 = pl.multiple_of(c * ROUTE_PAD, ROUTE_PAD)
                xin_ref[slot, pl.ds(r0, ROUTE_PAD), :] = _pack_halves(
                    acc_ref[pl.ds(r0, ROUTE_PAD), 0:half],
                    acc_ref[pl.ds(r0, ROUTE_PAD), half:2 * half])
                y_copy(w, slot, c).start()
            for_chunks(w, pack)

            @pl.when(jnp.logical_not(has_next))
            def _():
                for_chunks(w, lambda c: y_copy(w, slot, c).wait())


def _experts(item_e, item_start, item_rows, n_active, xs, w_gate_up, b_gate_up, w_down, b_down,
             tm, tf):
    cap, half = xs.shape
    d = 2 * half
    n_exp, _, two_f = w_gate_up.shape
    ff = two_f // 2
    nf = ff // tf
    n_items = item_e.shape[0]
    big = min(512, tm)
    assert big % ROUTE_PAD == 0 and (big & (big - 1)) == 0

    def fidx(w, f, rows):
        return jnp.where(rows[w] > 0, f, nf - 1)

    return pl.pallas_call(
        functools.partial(_expert_kernel, nf=nf, n_items=n_items, big=big),
        out_shape=jax.ShapeDtypeStruct((cap, half), I32),
        grid_spec=pltpu.PrefetchScalarGridSpec(
            num_scalar_prefetch=3,
            grid=(n_active, nf),
            in_specs=[
                pl.BlockSpec(memory_space=pl.ANY),
                pl.BlockSpec((1, d, tf), lambda w, f, ie, ist, ir: (ie[w], 0, fidx(w, f, ir))),
                pl.BlockSpec((1, d, tf), lambda w, f, ie, ist, ir: (ie[w], 0, nf + fidx(w, f, ir))),
                pl.BlockSpec((1, tf, d), lambda w, f, ie, ist, ir: (ie[w], fidx(w, f, ir), 0)),
                pl.BlockSpec((1, 1, tf), lambda w, f, ie, ist, ir: (ie[w], 0, fidx(w, f, ir))),
                pl.BlockSpec((1, 1, tf), lambda w, f, ie, ist, ir: (ie[w], 0, nf + fidx(w, f, ir))),
                pl.BlockSpec((1, 1, d), lambda w, f, ie, ist, ir: (ie[w], 0, 0)),
            ],
            out_specs=pl.BlockSpec(memory_space=pl.ANY),
            scratch_shapes=[pltpu.VMEM((2, tm, half), I32), pltpu.VMEM((tm, d), F32),
                            pltpu.SemaphoreType.DMA((2,)), pltpu.SemaphoreType.DMA]),
        compiler_params=_cparams(("arbitrary", "arbitrary")),
        name="moe_experts",
    )(item_e, item_start, item_rows, xs, w_gate_up, w_gate_up, w_down,
      b_gate_up.reshape(n_exp, 1, two_f), b_gate_up.reshape(n_exp, 1, two_f),
      b_down.reshape(n_exp, 1, d))


def _combine_kernel(yg_ref, gate_ref, h_ref, g_ref, b_ref, o_ref):
    gates = gate_ref[...]
    half = yg_ref.shape[2]
    z_lo = DEEPNORM_ALPHA * h_ref[:, 0:half]
    z_hi = DEEPNORM_ALPHA * h_ref[:, half:2 * half]
    for r in range(TOP_K):
        lo, hi = _unpack_halves(yg_ref[r])
        z_lo = z_lo + gates[:, r:r + 1] * lo
        z_hi = z_hi + gates[:, r:r + 1] * hi
    inv_d = 1.0 / (2 * half)
    mu = (jnp.sum(z_lo, axis=-1, keepdims=True) + jnp.sum(z_hi, axis=-1, keepdims=True)) * inv_d
    c_lo = z_lo - mu
    c_hi = z_hi - mu
    var = (jnp.sum(c_lo * c_lo, axis=-1, keepdims=True)
           + jnp.sum(c_hi * c_hi, axis=-1, keepdims=True)) * inv_d
    rstd = lax.rsqrt(var + LN_EPS)
    o_ref[:, 0:half] = c_lo * rstd * g_ref[:, 0:half] + b_ref[:, 0:half]
    o_ref[:, half:2 * half] = c_hi * rstd * g_ref[:, half:2 * half] + b_ref[:, half:2 * half]


def _combine(yg, gates, h1, ln_g, ln_b, tc):
    s, d = h1.shape
    return pl.pallas_call(
        _combine_kernel,
        out_shape=jax.ShapeDtypeStruct((s, d), F32),
        grid=(s // tc,),
        in_specs=[pl.BlockSpec((TOP_K, tc, d // 2), lambda t: (0, t, 0)),
                  pl.BlockSpec((tc, TOP_K), lambda t: (t, 0)),
                  pl.BlockSpec((tc, d), lambda t: (t, 0)),
                  pl.BlockSpec((1, d), lambda t: (0, 0)),
                  pl.BlockSpec((1, d), lambda t: (0, 0))],
        out_specs=pl.BlockSpec((tc, d), lambda t: (t, 0)),
        compiler_params=_cparams(("arbitrary",)),
        name="moe_combine_ln2",
    )(yg, gates, h1, ln_g.reshape(1, d), ln_b.reshape(1, d))


def _tiles(s, d, ff, n_heads):
    return dict(
        proj_tm=min(512, s), select_tq=min(512, s), conv_ts=min(256, s), outproj_tm=min(512, s),
        attn_heads=min(4, n_heads),
        route_tr=min(256, s), combine_tc=min(256, s),
        expert_tm=min(1280, max(ROUTE_PAD, (s * TOP_K // 16) // ROUTE_PAD * ROUTE_PAD)),
        expert_tf=min(512, ff))


def kernel(x, w_in, conv_w, conv_b, conv_ln_g, conv_ln_b, w_out, ln1_g, ln1_b,
           w_router, b_router, w_gate_up, b_gate_up, w_down, b_down, ln2_g, ln2_b):
    b, s, d = x.shape
    assert b == 1, "the kernels treat the sequence of the single batch element as the row axis"
    g = w_in.shape[1] // 5
    n_heads = g // HEAD_DIM
    nb = s // MOBA_BLOCK
    n_exp = w_router.shape[1]
    ff = w_down.shape[1]
    assert s % MOBA_BLOCK == 0 and nb % 8 == 0 and HEAD_DIM + nb + 3 <= AUG_DIM
    tl = _tiles(s, d, ff, n_heads)
    x2 = x.reshape(s, d)

    kgg, qv_t = _in_projection(x2, w_in, g, tl["proj_tm"])
    kmean = _block_means(kgg, g, nb)
    slopes = 2.0 ** (-(8.0 / n_heads) * jnp.arange(1, n_heads + 1, dtype=F32))
    q_aug, k_aug = _select(slopes, qv_t, kmean, kgg, n_heads, nb, tl["select_tq"])
    attn = _attention(q_aug, k_aug, qv_t, n_heads, nb, tl["attn_heads"])
    conv = _conformer_conv(kgg, conv_w, conv_b, conv_ln_g, conv_ln_b, g, tl["conv_ts"])
    h1, h1_packed, logits_t = _out_projection(attn, conv, w_out, x2, ln1_g, ln1_b,
                                              w_router, b_router, tl["outproj_tm"])

    e_t, gate_t, rank_t, counts = _route(logits_t, tl["route_tr"])
    counts = counts[:, 0]
    padded = (counts + ROUTE_PAD - 1) // ROUTE_PAD * ROUTE_PAD
    pad_end = jnp.cumsum(padded)
    pad_start = (pad_end - padded).astype(I32)
    cap = s * TOP_K + n_exp * ROUTE_PAD
    e_flat = e_t.reshape(-1)
    rank_flat = rank_t.reshape(-1)
    xs = _sc_dispatch(h1_packed, e_flat, rank_flat, pad_start, cap, s)

    tm = tl["expert_tm"]
    n_items = cap // tm + n_exp
    per_e = (padded + tm - 1) // tm
    item_end = jnp.cumsum(per_e)
    item_ids = jnp.arange(n_items, dtype=I32)
    item_e = jnp.minimum(jnp.sum(item_ids[:, None] >= item_end[None, :], axis=1),
                         n_exp - 1).astype(I32)
    piece = item_ids - (item_end - per_e)[item_e]
    active = item_ids < item_end[-1]
    item_start = jnp.where(active, pad_start[item_e] + piece * tm, 0).astype(I32)
    item_rows = jnp.where(active, jnp.minimum(tm, padded[item_e] - piece * tm), 0).astype(I32)
    last_e = item_e[jnp.maximum(item_end[-1] - 1, 0)]
    item_e = jnp.where(active, item_e, last_e).astype(I32)

    y = _experts(item_e, item_start, item_rows, item_end[-1].astype(I32), xs,
                 w_gate_up, b_gate_up, w_down, b_down, tm, tl["expert_tf"])
    yg = _sc_combine_gather(y, e_flat, rank_flat, pad_start).reshape(TOP_K, s, d // 2)
    out = _combine(yg, gate_t.T, h1, ln2_g, ln2_b, tl["combine_tc"])
    return out.reshape(b, s, d)
```

```python
import functools

import jax
import jax.numpy as jnp
from jax import lax
from jax.experimental import pallas as pl
from jax.experimental.pallas import tpu as pltpu
from jax.experimental.pallas import tpu_sc as plsc

F32 = jnp.float32
BF16 = jnp.bfloat16
I32 = jnp.int32

HEAD_DIM = 128
MOBA_BLOCK = 256
MOBA_TOPK = 3
CONV_WIDTH = 31
CONV_HALO = 32
SUBLANES = 8
LANES = 128
TOP_K = 4
SWIGLU_ALPHA = 1.702
SWIGLU_LIMIT = 7.0
LN_EPS = 1e-5
DEPTH = 1
DEEPNORM_ALPHA = (2.0 * DEPTH) ** 0.25
ROUTE_PAD = 128
AUG_DIM = 256
MASK_NEG = -1e30
ONES_ROWS = 16
ATTN_UNIT = 4
VMEM_LIMIT = 56 * 1024 * 1024


def _cparams(sem):
    return pltpu.CompilerParams(dimension_semantics=sem, vmem_limit_bytes=VMEM_LIMIT)


def _proj_nn_kernel(x_ref, w_ref, o_ref, wb_ref):
    @pl.when(pl.program_id(1) == 0)
    def _():
        wb_ref[...] = w_ref[...].astype(BF16)

    o_ref[...] = jnp.dot(x_ref[...].astype(BF16), wb_ref[...],
                         preferred_element_type=F32).astype(o_ref.dtype)


def _in_projection(x2, w_in, g, tm):
    s, d = x2.shape
    return pl.pallas_call(
        _proj_nn_kernel,
        out_shape=jax.ShapeDtypeStruct((s, 3 * g), BF16),
        grid=(3, s // tm),
        in_specs=[pl.BlockSpec((tm, d), lambda n, m: (m, 0)),
                  pl.BlockSpec((d, g), lambda n, m: (0, jnp.where(n == 0, 1, n + 2)))],
        out_specs=pl.BlockSpec((tm, g), lambda n, m: (m, n)),
        scratch_shapes=[pltpu.VMEM((d, g), BF16)],
        compiler_params=_cparams(("arbitrary", "arbitrary")),
        name="proj_nn",
    )(x2, w_in)


def _kmean_kernel(k_ref, o_ref, *, blocks):
    for b in range(blocks):
        kb = k_ref[b * MOBA_BLOCK:(b + 1) * MOBA_BLOCK, :].astype(F32)
        o_ref[b:b + 1, :] = jnp.sum(kb, axis=0, keepdims=True) * (1.0 / MOBA_BLOCK)


def _block_means(kgg, g, nb):
    blocks = 8
    return pl.pallas_call(
        functools.partial(_kmean_kernel, blocks=blocks),
        out_shape=jax.ShapeDtypeStruct((nb, g), F32),
        grid=(nb // blocks,),
        in_specs=[pl.BlockSpec((blocks * MOBA_BLOCK, g), lambda i: (i, 0))],
        out_specs=pl.BlockSpec((blocks, g), lambda i: (i, 0)),
        compiler_params=_cparams(("arbitrary",)),
        name="moba_kmean",
    )(kgg)


def _select_kernel(slopes_ref, qt_ref, km_ref, k_ref, qa_ref, ka_ref, *, nb, tq, n_heads):
    t = pl.program_id(0)
    col = t * tq + lax.broadcasted_iota(I32, (nb, tq), 1)
    qblk = col // MOBA_BLOCK
    j = lax.broadcasted_iota(I32, (nb, tq), 0)
    neg_inf = jnp.float32(-jnp.inf)
    past = j < qblk
    own = j == qblk
    n_extra = AUG_DIM - HEAD_DIM - nb
    r = lax.broadcasted_iota(I32, (n_extra, tq), 0)
    qb = ((t * tq + lax.broadcasted_iota(I32, (n_extra, tq), 1)) // MOBA_BLOCK).astype(F32)
    extra_unit = jnp.where(r == 0, 1.0,
                           jnp.where(r == 1, float(MOBA_BLOCK),
                                     jnp.where(r == 2, -float(MOBA_BLOCK) * qb, 0.0)))
    n_aug = AUG_DIM - HEAD_DIM
    pos = t * tq + lax.broadcasted_iota(I32, (tq, n_aug), 0)
    kblk = pos // MOBA_BLOCK
    pib = pos % MOBA_BLOCK
    lane = lax.broadcasted_iota(I32, (tq, n_aug), 1)
    aug = jnp.where(lane < nb, (lane == kblk).astype(F32),
                    jnp.where(lane == nb, pib.astype(F32),
                              jnp.where(lane == nb + 1, kblk.astype(F32),
                                        jnp.where(lane == nb + 2, 1.0, 0.0)))).astype(BF16)
    scale = HEAD_DIM ** -0.5
    for h in range(n_heads):
        cols = slice(h * HEAD_DIM, (h + 1) * HEAD_DIM)
        q_t = qt_ref[cols, :].astype(F32)
        gate = jnp.dot(km_ref[:, cols], q_t, preferred_element_type=F32,
                       precision=lax.Precision.HIGHEST)
        gcur = jnp.where(past, gate, neg_inf)
        sel = own
        for _ in range(MOBA_TOPK):
            m = jnp.max(gcur, axis=0, keepdims=True)
            is_max = jnp.logical_and(gcur == m, m > neg_inf)
            idx = jnp.min(jnp.where(is_max, j, nb), axis=0, keepdims=True)
            pick = j == idx
            sel = jnp.logical_or(sel, pick)
            gcur = jnp.where(pick, neg_inf, gcur)
        qa_ref[h, 0:HEAD_DIM, :] = (q_t * scale).astype(BF16)
        qa_ref[h, HEAD_DIM:HEAD_DIM + nb, :] = jnp.where(sel, 0.0, MASK_NEG).astype(BF16)
        qa_ref[h, HEAD_DIM + nb:AUG_DIM, :] = (slopes_ref[h] * extra_unit).astype(BF16)
        ka_ref[h, :, 0:HEAD_DIM] = k_ref[:, cols]
        ka_ref[h, :, HEAD_DIM:AUG_DIM] = aug


def _select(slopes, qv_t, kmean, kgg, n_heads, nb, tq):
    s = kgg.shape[0]
    g = n_heads * HEAD_DIM
    return pl.pallas_call(
        functools.partial(_select_kernel, nb=nb, tq=tq, n_heads=n_heads),
        out_shape=(jax.ShapeDtypeStruct((n_heads, AUG_DIM, s), BF16),
                   jax.ShapeDtypeStruct((n_heads, s, AUG_DIM), BF16)),
        grid_spec=pltpu.PrefetchScalarGridSpec(
            num_scalar_prefetch=1,
            grid=(s // tq,),
            in_specs=[pl.BlockSpec((g, tq), lambda t, sl: (0, t)),
                      pl.BlockSpec((nb, g), lambda t, sl: (0, 0)),
                      pl.BlockSpec((tq, g), lambda t, sl: (t, 0))],
            out_specs=[pl.BlockSpec((n_heads, AUG_DIM, tq), lambda t, sl: (0, 0, t)),
                       pl.BlockSpec((n_heads, tq, AUG_DIM), lambda t, sl: (0, t, 0))]),
        compiler_params=_cparams(("arbitrary",)),
        name="moba_select",
    )(slopes, qv_t, kmean, kgg)


def _attn_kernel(qa_ref, ka_ref, vt_ref, o_ref, sa_ref, sb_ref, *, heads):
    i = pl.program_id(1)
    tq = MOBA_BLOCK
    unit_keys = ATTN_UNIT * MOBA_BLOCK
    neg_inf = jnp.float32(-jnp.inf)

    def scores_of(j, nkeys):
        off = pl.multiple_of(j * MOBA_BLOCK, MOBA_BLOCK)
        return [jnp.dot(ka_ref[hh, pl.ds(off, nkeys), :], qa_ref[hh],
                        preferred_element_type=F32) for hh in range(heads)]

    def scores_into(ref, unit):
        for hh, s_t in enumerate(scores_of(ATTN_UNIT * unit, unit_keys)):
            ref[hh] = s_t

    def update(j, scores, carries, diagonal, nkeys):
        off = pl.multiple_of(j * MOBA_BLOCK, MOBA_BLOCK)
        stats = []
        for hh in range(heads):
            m = carries[hh][0]
            s_t = scores[hh]
            if diagonal:
                key = lax.broadcasted_iota(I32, s_t.shape, 0) - (nkeys - MOBA_BLOCK)
                qry = lax.broadcasted_iota(I32, s_t.shape, 1)
                s_t = jnp.where(key <= qry, s_t, neg_inf)
            m_new = jnp.maximum(m, jnp.max(s_t, axis=0, keepdims=True))
            stats.append((m_new, jnp.exp(m - m_new), jnp.exp(s_t - m_new).astype(BF16)))
        out = []
        ones = jnp.ones((ONES_ROWS, nkeys), BF16)
        for hh in range(heads):
            m_new, alpha, p = stats[hh]
            vb = vt_ref[hh * HEAD_DIM:(hh + 1) * HEAD_DIM, pl.ds(off, nkeys)]
            vb1 = jnp.concatenate([vb, ones], axis=0)
            acc_new = alpha * carries[hh][1] + jnp.dot(vb1, p, preferred_element_type=F32)
            out.append((m_new, acc_new))
        return tuple(out)

    init = tuple((jnp.full((1, tq), neg_inf, F32), jnp.zeros((HEAD_DIM + ONES_ROWS, tq), F32))
                 for _ in range(heads))
    def tail_update(r):
        nkeys = (r + 1) * MOBA_BLOCK
        return lambda cs: update(i - r, scores_of(i - r, nkeys), cs, True, nkeys)

    carries = lax.switch(i % ATTN_UNIT, [tail_update(r) for r in range(ATTN_UNIT)], init)

    n_units = i // ATTN_UNIT
    last_unit = jnp.maximum(n_units - 1, 0)

    @pl.when(n_units > 0)
    def _():
        scores_into(sa_ref, 0)

    def from_ref(ref):
        return [ref[hh] for hh in range(heads)]

    def two_units(k, cs):
        scores_into(sb_ref, 2 * k + 1)
        cs = update(2 * ATTN_UNIT * k, from_ref(sa_ref), cs, False, unit_keys)
        scores_into(sa_ref, jnp.minimum(2 * k + 2, last_unit))
        return update(2 * ATTN_UNIT * k + ATTN_UNIT, from_ref(sb_ref), cs, False, unit_keys)

    carries = lax.fori_loop(0, n_units // 2, two_units, carries)
    carries = lax.cond(
        n_units % 2 == 1,
        lambda cs: update(ATTN_UNIT * last_unit, from_ref(sa_ref), cs, False, unit_keys),
        lambda cs: cs, carries)
    for hh in range(heads):
        acc = carries[hh][1]
        out_t = acc[0:HEAD_DIM, :] / acc[HEAD_DIM:HEAD_DIM + 1, :]
        o_ref[:, hh * HEAD_DIM:(hh + 1) * HEAD_DIM] = out_t.T.astype(o_ref.dtype)


def _attention(q_aug, k_aug, qv_t, n_heads, nb, heads):
    s = k_aug.shape[1]
    groups = n_heads // heads
    once = pl.Buffered(1)
    return pl.pallas_call(
        functools.partial(_attn_kernel, heads=heads),
        out_shape=jax.ShapeDtypeStruct((s, n_heads * HEAD_DIM), BF16),
        grid=(groups, nb),
        in_specs=[pl.BlockSpec((heads, AUG_DIM, MOBA_BLOCK), lambda hg, i: (hg, 0, i)),
                  pl.BlockSpec((heads, s, AUG_DIM), lambda hg, i: (hg, 0, 0), pipeline_mode=once),
                  pl.BlockSpec((heads * HEAD_DIM, s), lambda hg, i: (groups + hg, 0),
                               pipeline_mode=once)],
        out_specs=pl.BlockSpec((MOBA_BLOCK, heads * HEAD_DIM), lambda hg, i: (i, hg)),
        scratch_shapes=[pltpu.VMEM((heads, ATTN_UNIT * MOBA_BLOCK, MOBA_BLOCK), F32),
                        pltpu.VMEM((heads, ATTN_UNIT * MOBA_BLOCK, MOBA_BLOCK), F32)],
        compiler_params=_cparams(("arbitrary", "arbitrary")),
        name="moba_attention",
    )(q_aug, k_aug, qv_t)


def _qv_conv_kernel(wt_ref, x_ref, ga_ref, gb_ref, w_ref, b_ref, g_ref, beta_ref, qv_ref, o_ref,
                    u_ref, ush_ref, *, ts, sub):
    t = pl.program_id(0)

    @pl.when(t == 0)
    def _():
        u_ref[0:CONV_HALO, :] = jnp.zeros((CONV_HALO, u_ref.shape[1]), F32)

    @pl.when(t > 0)
    def _():
        u_ref[0:CONV_HALO, :] = u_ref[ts:ts + CONV_HALO, :]

    x_b = x_ref[...].astype(BF16)
    n_groups = ts // sub
    slab = wt_ref.shape[0] // n_groups

    def project(k, anchor):
        rows = slice(k * slab, (k + 1) * slab)
        x_k = x_b if anchor is None else x_b + anchor
        qv_ref[rows, :] = lax.dot_general(wt_ref[rows, :], x_k, (((1,), (1,)), ((), ())),
                                          preferred_element_type=F32).astype(qv_ref.dtype)

    gb = gb_ref[...].astype(F32)
    u_ref[CONV_HALO:CONV_HALO + ts, :] = ga_ref[...].astype(F32) * jax.nn.sigmoid(gb)
    for b in range(SUBLANES):
        ush_ref[b, 0:ts + CONV_HALO - b, :] = u_ref[b:ts + CONV_HALO, :]
    first = CONV_HALO - (CONV_WIDTH - 1)
    anchor = None
    for r0 in range(0, ts, sub):
        project(r0 // sub, anchor)
        acc = jnp.broadcast_to(b_ref[...], (sub, u_ref.shape[1]))
        for tap in range(CONV_WIDTH):
            off = first + tap
            aligned = r0 + off - off % SUBLANES
            acc = acc + w_ref[tap:tap + 1, :] * ush_ref[off % SUBLANES, aligned:aligned + sub, :]
        mu = jnp.mean(acc, axis=-1, keepdims=True)
        cen = acc - mu
        var = jnp.mean(cen * cen, axis=-1, keepdims=True)
        y = cen * lax.rsqrt(var + LN_EPS) * g_ref[...] + beta_ref[...]
        o_ref[r0:r0 + sub, :] = (y * jax.nn.sigmoid(y)).astype(o_ref.dtype)
        bits = pltpu.bitcast(y[0:1, 0:1], I32)
        zero = lax.shift_right_logical(lax.shift_right_logical(bits, jnp.int32(31)), jnp.int32(1))
        anchor = zero.astype(BF16)


def _qv_projection_and_conv(x2, w_in, kgg, conv_w, conv_b, conv_ln_g, conv_ln_b, g, ts):
    s, d = x2.shape
    row = lambda v: v.reshape(1, g).astype(F32)
    w_qv_t = jnp.concatenate([w_in[:, :g], w_in[:, 2 * g:3 * g]], axis=1).T.astype(BF16)
    once = pl.Buffered(1)
    return pl.pallas_call(
        functools.partial(_qv_conv_kernel, ts=ts, sub=32),
        out_shape=(jax.ShapeDtypeStruct((2 * g, s), BF16), jax.ShapeDtypeStruct((s, g), BF16)),
        grid=(s // ts,),
        in_specs=[pl.BlockSpec((2 * g, d), lambda t: (0, 0), pipeline_mode=once),
                  pl.BlockSpec((ts, d), lambda t: (t, 0)),
                  pl.BlockSpec((ts, g), lambda t: (t, 1)),
                  pl.BlockSpec((ts, g), lambda t: (t, 2)),
                  pl.BlockSpec((CONV_WIDTH, g), lambda t: (0, 0)),
                  pl.BlockSpec((1, g), lambda t: (0, 0)),
                  pl.BlockSpec((1, g), lambda t: (0, 0)),
                  pl.BlockSpec((1, g), lambda t: (0, 0))],
        out_specs=[pl.BlockSpec((2 * g, ts), lambda t: (0, t)),
                   pl.BlockSpec((ts, g), lambda t: (t, 0))],
        scratch_shapes=[pltpu.VMEM((ts + CONV_HALO, g), F32),
                        pltpu.VMEM((SUBLANES, ts + CONV_HALO, g), F32)],
        compiler_params=_cparams(("arbitrary",)),
        name="qv_proj_conformer_conv",
    )(w_qv_t, x2, kgg, kgg, conv_w, row(conv_b), row(conv_ln_g), row(conv_ln_b))


def _layer_norm_rows(z, gain, bias):
    mu = jnp.mean(z, axis=-1, keepdims=True)
    cen = z - mu
    var = jnp.mean(cen * cen, axis=-1, keepdims=True)
    return cen * lax.rsqrt(var + LN_EPS) * gain + bias


HI_HALF = -65536


def _pack_halves(lo, hi):
    lo_bits = pltpu.bitcast(lo.astype(BF16).astype(F32), I32)
    hi_bits = pltpu.bitcast(hi.astype(BF16).astype(F32), I32)
    return lax.shift_right_logical(lo_bits, jnp.int32(16)) | (hi_bits & jnp.int32(HI_HALF))


def _unpack_halves(words):
    lo = pltpu.bitcast(lax.shift_left(words, jnp.int32(16)), F32)
    hi = pltpu.bitcast(words & jnp.int32(HI_HALF), F32)
    return lo, hi


def _outproj_kernel(attn_ref, conv_ref, wo_ref, x_ref, g_ref, b_ref, wr_ref, br_ref,
                    h_ref, hp_ref, lg_ref, *, g):
    tm = x_ref.shape[0]
    sub = min(tm, 256)
    half = x_ref.shape[1] // 2
    n_exp = lg_ref.shape[0]
    wr = wr_ref[...]
    w_hi = wr.astype(BF16)
    w_lo = (wr - w_hi.astype(F32)).astype(BF16)
    mixes = []
    for r0 in range(0, tm, sub):
        mix = jnp.dot(attn_ref[r0:r0 + sub, :], wo_ref[0:g, :], preferred_element_type=F32)
        mixes.append(mix + jnp.dot(conv_ref[r0:r0 + sub, :], wo_ref[g:2 * g, :],
                                   preferred_element_type=F32))
    for k, r0 in enumerate(range(0, tm, sub)):
        h1 = _layer_norm_rows(DEEPNORM_ALPHA * x_ref[r0:r0 + sub, :] + mixes[k],
                              g_ref[...], b_ref[...])
        h_ref[r0:r0 + sub, :] = h1
        hp_ref[r0:r0 + sub, :] = _pack_halves(h1[:, 0:half], h1[:, half:])
        h_hi = h1.astype(BF16)
        h_lo = (h1 - h_hi.astype(F32)).astype(BF16)
        lg = jnp.dot(h_hi, w_hi, preferred_element_type=F32)
        lg = lg + jnp.dot(h_lo, w_hi, preferred_element_type=F32)
        lg = lg + jnp.dot(h_hi, w_lo, preferred_element_type=F32)
        lg_ref[:, r0:r0 + sub] = lg.T[0:n_exp, :] + br_ref[...]


def _out_projection(attn, conv, w_out, x2, ln_g, ln_b, w_router, b_router, tm):
    s, d = x2.shape
    g = attn.shape[1]
    e = w_router.shape[1]
    return pl.pallas_call(
        functools.partial(_outproj_kernel, g=g),
        out_shape=(jax.ShapeDtypeStruct((s, d), F32), jax.ShapeDtypeStruct((s, d // 2), I32),
                   jax.ShapeDtypeStruct((e, s), F32)),
        grid=(s // tm,),
        in_specs=[pl.BlockSpec((tm, g), lambda m: (m, 0)),
                  pl.BlockSpec((tm, g), lambda m: (m, 0)),
                  pl.BlockSpec((2 * g, d), lambda m: (0, 0)),
                  pl.BlockSpec((tm, d), lambda m: (m, 0)),
                  pl.BlockSpec((1, d), lambda m: (0, 0)),
                  pl.BlockSpec((1, d), lambda m: (0, 0)),
                  pl.BlockSpec((d, LANES), lambda m: (0, 0)),
                  pl.BlockSpec((e, 1), lambda m: (0, 0))],
        out_specs=[pl.BlockSpec((tm, d), lambda m: (m, 0)),
                   pl.BlockSpec((tm, d // 2), lambda m: (m, 0)),
                   pl.BlockSpec((e, tm), lambda m: (0, m))],
        compiler_params=_cparams(("arbitrary",)),
        name="outproj_ln1_router",
    )(attn, conv, w_out.astype(BF16), x2, ln_g.reshape(1, d), ln_b.reshape(1, d),
      jnp.pad(w_router, ((0, 0), (0, LANES - e))), b_router.reshape(e, 1))


def _route_kernel(lg_ref, e_ref, gate_ref, rank_ref, cnt_ref, carry_ref, *, n_exp, tr):
    t = pl.program_id(0)

    @pl.when(t == 0)
    def _():
        carry_ref[...] = jnp.zeros_like(carry_ref)

    neg_inf = jnp.float32(-jnp.inf)
    cur = lg_ref[...]
    j = lax.broadcasted_iota(I32, (n_exp, tr), 0)
    vals, picks = [], []
    for r in range(TOP_K):
        m = jnp.max(cur, axis=0, keepdims=True)
        idx = jnp.min(jnp.where(cur == m, j, n_exp), axis=0, keepdims=True)
        pick = j == idx
        e_ref[r:r + 1, :] = idx
        vals.append(m)
        picks.append(pick)
        cur = jnp.where(pick, neg_inf, cur)
    exps = [jnp.exp(v - vals[0]) for v in vals]
    den = exps[0]
    for r in range(1, TOP_K):
        den = den + exps[r]
    for r in range(TOP_K):
        gate_ref[r:r + 1, :] = exps[r] / den

    chosen = picks[0].astype(F32)
    for r in range(1, TOP_K):
        chosen = chosen + picks[r].astype(F32)
    a = lax.broadcasted_iota(I32, (tr, tr), 0)
    b = lax.broadcasted_iota(I32, (tr, tr), 1)
    upper = (a < b).astype(BF16)
    excl = jnp.dot(chosen.astype(BF16), upper, preferred_element_type=F32)
    base = carry_ref[:, 0:1]
    rank = excl + base
    for r in range(TOP_K):
        rank_ref[r:r + 1, :] = jnp.sum(jnp.where(picks[r], rank, 0.0), axis=0,
                                       keepdims=True).astype(I32)
    total = base + jnp.sum(chosen, axis=1, keepdims=True)
    carry_ref[...] = jnp.broadcast_to(total, carry_ref.shape)
    cnt_ref[...] = jnp.broadcast_to(total, cnt_ref.shape).astype(I32)


def _route(logits_t, tr):
    n_exp, s = logits_t.shape
    return pl.pallas_call(
        functools.partial(_route_kernel, n_exp=n_exp, tr=tr),
        out_shape=(jax.ShapeDtypeStruct((TOP_K, s), I32), jax.ShapeDtypeStruct((TOP_K, s), F32),
                   jax.ShapeDtypeStruct((TOP_K, s), I32), jax.ShapeDtypeStruct((n_exp, 128), I32)),
        grid=(s // tr,),
        in_specs=[pl.BlockSpec((n_exp, tr), lambda t: (0, t))],
        out_specs=[pl.BlockSpec((TOP_K, tr), lambda t: (0, t)),
                   pl.BlockSpec((TOP_K, tr), lambda t: (0, t)),
                   pl.BlockSpec((TOP_K, tr), lambda t: (0, t)),
                   pl.BlockSpec((n_exp, 128), lambda t: (0, 0))],
        scratch_shapes=[pltpu.VMEM((n_exp, 128), F32)],
        compiler_params=_cparams(("arbitrary",)),
        name="moe_route",
    )(logits_t)


SC_CORES = 2
SC_SUBCORES = 16
SC_LANES = 16
SC_GATHER_ROWS = 32
SC_SCAN = 2048


_SC_PARAMS = pltpu.CompilerParams(needs_layout_passes=False)


def _sc_mesh():
    return plsc.VectorSubcoreMesh(core_axis_name="c", subcore_axis_name="s",
                                  num_cores=SC_CORES, num_subcores=SC_SUBCORES)


def _sc_worker():
    return lax.axis_index("s") * SC_CORES + lax.axis_index("c")


def _sc_gather_loop(table_hbm, idx_all, out_hbm, bufs, sems, base, n_chunks):
    chunk = SC_GATHER_ROWS

    def gather(ci, b):
        return pltpu.make_async_copy(
            table_hbm.at[idx_all.at[pl.ds(ci * chunk, chunk)]], bufs[b], sems[b])

    gather(0, 0).start()

    @pl.loop(0, n_chunks, step=2)
    def _(ci):
        gather(ci + 1, 1).start()
        gather(ci, 0).wait()
        pltpu.sync_copy(bufs[0], out_hbm.at[pl.ds(base + ci * chunk, chunk)])

        @pl.when(ci + 2 < n_chunks)
        def _():
            gather(ci + 2, 0).start()

        gather(ci + 1, 1).wait()
        pltpu.sync_copy(bufs[1], out_hbm.at[pl.ds(base + (ci + 1) * chunk, chunk)])


def _sc_dest(ps_v, e_buf, r_buf, v):
    e_vec = e_buf[pl.ds(v * SC_LANES, SC_LANES)]
    return plsc.load_gather(ps_v, [e_vec]) + r_buf[pl.ds(v * SC_LANES, SC_LANES)]


def _sc_dispatch(table, e_flat, rank_flat, pad_start, cap, s):
    n_assign, width = e_flat.shape[0], table.shape[1]
    n_workers = SC_CORES * SC_SUBCORES
    per_worker = cap // n_workers
    chunk = SC_GATHER_ROWS
    assert cap % (n_workers * 2 * chunk) == 0 and n_assign % SC_SCAN == 0
    assert per_worker % SC_LANES == 0

    @functools.partial(
        pl.kernel, mesh=_sc_mesh(), out_type=jax.ShapeDtypeStruct((cap, width), table.dtype),
        scratch_types=[pltpu.VMEM((per_worker,), I32), pltpu.VMEM((pad_start.shape[0],), I32),
                       pltpu.VMEM((SC_SCAN,), I32), pltpu.VMEM((SC_SCAN,), I32),
                       pltpu.VMEM((chunk, width), table.dtype),
                       pltpu.VMEM((chunk, width), table.dtype),
                       pltpu.SemaphoreType.DMA, pltpu.SemaphoreType.DMA],
        compiler_params=_SC_PARAMS, name="sc_dispatch")
    def dispatch(table_hbm, e_hbm, r_hbm, ps_hbm, out_hbm, idx_all, ps_v, e_buf, r_buf,
                 buf0, buf1, sem0, sem1):
        base = _sc_worker() * per_worker
        lane = lax.iota(I32, SC_LANES)
        pltpu.sync_copy(ps_hbm, ps_v)

        @pl.loop(0, per_worker // SC_LANES)
        def _(k):
            idx_all[pl.ds(k * SC_LANES, SC_LANES)] = lax.rem(base + k * SC_LANES + lane, s)

        @pl.loop(0, n_assign // SC_SCAN)
        def _(c):
            pltpu.sync_copy(e_hbm.at[pl.ds(c * SC_SCAN, SC_SCAN)], e_buf)
            pltpu.sync_copy(r_hbm.at[pl.ds(c * SC_SCAN, SC_SCAN)], r_buf)

            @pl.loop(0, SC_SCAN // SC_LANES)
            def _(v):
                loc = _sc_dest(ps_v, e_buf, r_buf, v) - base
                mine = jnp.logical_and(loc >= 0, loc < per_worker)
                tok = lax.rem(c * SC_SCAN + v * SC_LANES + lane, s)
                plsc.store_scatter(idx_all, [jnp.where(mine, loc, 0)], tok, mask=mine)

        _sc_gather_loop(table_hbm, idx_all, out_hbm, (buf0, buf1), (sem0, sem1), base,
                        per_worker // chunk)

    return dispatch(table, e_flat, rank_flat, pad_start)


def _sc_combine_gather(y, e_flat, rank_flat, pad_start):
    n_assign, width = e_flat.shape[0], y.shape[1]
    n_workers = SC_CORES * SC_SUBCORES
    per_worker = n_assign // n_workers
    chunk = SC_GATHER_ROWS
    assert n_assign % (n_workers * 2 * chunk) == 0

    @functools.partial(
        pl.kernel, mesh=_sc_mesh(), out_type=jax.ShapeDtypeStruct((n_assign, width), y.dtype),
        scratch_types=[pltpu.VMEM((per_worker,), I32), pltpu.VMEM((pad_start.shape[0],), I32),
                       pltpu.VMEM((per_worker,), I32), pltpu.VMEM((per_worker,), I32),
                       pltpu.VMEM((chunk, width), y.dtype), pltpu.VMEM((chunk, width), y.dtype),
                       pltpu.SemaphoreType.DMA, pltpu.SemaphoreType.DMA],
        compiler_params=_SC_PARAMS, name="sc_combine_gather")
    def combine(y_hbm, e_hbm, r_hbm, ps_hbm, out_hbm, idx_all, ps_v, e_buf, r_buf,
                buf0, buf1, sem0, sem1):
        base = _sc_worker() * per_worker
        pltpu.sync_copy(ps_hbm, ps_v)
        pltpu.sync_copy(e_hbm.at[pl.ds(base, per_worker)], e_buf)
        pltpu.sync_copy(r_hbm.at[pl.ds(base, per_worker)], r_buf)

        @pl.loop(0, per_worker // SC_LANES)
        def _(v):
            idx_all[pl.ds(v * SC_LANES, SC_LANES)] = _sc_dest(ps_v, e_buf, r_buf, v)

        _sc_gather_loop(y_hbm, idx_all, out_hbm, (buf0, buf1), (sem0, sem1), base,
                        per_worker // chunk)

    return combine(y, e_flat, rank_flat, pad_start)


def _expert_kernel(item_e, item_start, item_rows, xs_ref, wg_ref, wu_ref, wd_ref,
                   bg_ref, bu_ref, bd_ref, y_ref, xin_ref, acc_ref,
                   sem_x, sem_y, *, nf, n_items, big):
    w = pl.program_id(0)
    f = pl.program_id(1)
    rows = item_rows[w]
    slot = w % 2
    half = xin_ref.shape[2]
    has_next = jnp.logical_and(w + 1 < n_items, item_rows[jnp.minimum(w + 1, n_items - 1)] > 0)

    def x_copy(item, sl, c):
        src = pl.multiple_of(item_start[item] + c * ROUTE_PAD, ROUTE_PAD)
        dst = pl.multiple_of(c * ROUTE_PAD, ROUTE_PAD)
        return pltpu.make_async_copy(xs_ref.at[pl.ds(src, ROUTE_PAD)],
                                     xin_ref.at[sl, pl.ds(dst, ROUTE_PAD)], sem_x.at[sl])

    def y_copy(item, sl, c):
        src = pl.multiple_of(c * ROUTE_PAD, ROUTE_PAD)
        dst = pl.multiple_of(item_start[item] + c * ROUTE_PAD, ROUTE_PAD)
        return pltpu.make_async_copy(xin_ref.at[sl, pl.ds(src, ROUTE_PAD)],
                                     y_ref.at[pl.ds(dst, ROUTE_PAD)], sem_y)

    def for_chunks(item, fn):
        def body(c, carry):
            fn(c)
            return carry
        lax.fori_loop(0, item_rows[item] // ROUTE_PAD, body, 0)

    def compute(r0, size, first):
        lo, hi = _unpack_halves(xin_ref[slot, pl.ds(r0, size), :])
        lo = lo.astype(BF16)
        hi = hi.astype(BF16)

        def x_dot(w_ref):
            return (jnp.dot(lo, w_ref[0, 0:half, :].astype(BF16), preferred_element_type=F32)
                    + jnp.dot(hi, w_ref[0, half:2 * half, :].astype(BF16),
                              preferred_element_type=F32))

        gate = x_dot(wg_ref) + bg_ref[0]
        up = x_dot(wu_ref) + bu_ref[0]
        gate = jnp.minimum(gate, SWIGLU_LIMIT)
        up = jnp.clip(up, -SWIGLU_LIMIT, SWIGLU_LIMIT)
        act = (up + 1.0) * (gate * jax.nn.sigmoid(SWIGLU_ALPHA * gate))
        part = jnp.dot(act.astype(BF16), wd_ref[0].astype(BF16), preferred_element_type=F32)
        if first:
            acc_ref[pl.ds(r0, size), :] = part + bd_ref[0]
        else:
            acc_ref[pl.ds(r0, size), :] += part

    def compute_all(first):
        n_big = rows // big

        def body(c, carry):
            compute(pl.multiple_of(c * big, big), big, first)
            return carry
        lax.fori_loop(0, n_big, body, 0)
        done = n_big * big
        size = big // 2
        while size >= ROUTE_PAD:
            take = ((rows - done) & size) != 0
            here = done

            @pl.when(take)
            def _(here=here, size=size):
                compute(pl.multiple_of(here, ROUTE_PAD), size, first)
            done = done + jnp.where(take, size, 0)
            size //= 2

    @pl.when(rows > 0)
    def _():
        @pl.when(f == 0)
        def _():
            @pl.when(w == 0)
            def _():
                for_chunks(w, lambda c: x_copy(w, slot, c).start())

            for_chunks(w, lambda c: x_copy(w, slot, c).wait())

            @pl.when(w > 0)
            def _():
                for_chunks(w - 1, lambda c: y_copy(w - 1, 1 - slot, c).wait())

            @pl.when(has_next)
            def _():
                for_chunks(w + 1, lambda c: x_copy(w + 1, 1 - slot, c).start())

            compute_all(True)

        @pl.when(f > 0)
        def _():
            compute_all(False)

        @pl.when(f == nf - 1)
        def _():
            def pack(c):
                r0 = pl.multiple_of(c * ROUTE_PAD, ROUTE_PAD)
                xin_ref[slot, pl.ds(r0, ROUTE_PAD), :] = _pack_halves(
                    acc_ref[pl.ds(r0, ROUTE_PAD), 0:half],
                    acc_ref[pl.ds(r0, ROUTE_PAD), half:2 * half])
                y_copy(w, slot, c).start()
            for_chunks(w, pack)

            @pl.when(jnp.logical_not(has_next))
            def _():
                for_chunks(w, lambda c: y_copy(w, slot, c).wait())


def _experts(item_e, item_start, item_rows, n_active, xs, w_gate_up, b_gate_up, w_down, b_down,
             tm, tf):
    cap, half = xs.shape
    d = 2 * half
    n_exp, _, two_f = w_gate_up.shape
    ff = two_f // 2
    nf = ff // tf
    n_items = item_e.shape[0]
    big = min(512, tm)
    assert big % ROUTE_PAD == 0 and (big & (big - 1)) == 0

    def fidx(w, f, rows):
        return jnp.where(rows[w] > 0, f, nf - 1)

    return pl.pallas_call(
        functools.partial(_expert_kernel, nf=nf, n_items=n_items, big=big),
        out_shape=jax.ShapeDtypeStruct((cap, half), I32),
        grid_spec=pltpu.PrefetchScalarGridSpec(
            num_scalar_prefetch=3,
            grid=(n_active, nf),
            in_specs=[
                pl.BlockSpec(memory_space=pl.ANY),
                pl.BlockSpec((1, d, tf), lambda w, f, ie, ist, ir: (ie[w], 0, fidx(w, f, ir))),
                pl.BlockSpec((1, d, tf), lambda w, f, ie, ist, ir: (ie[w], 0, nf + fidx(w, f, ir))),
                pl.BlockSpec((1, tf, d), lambda w, f, ie, ist, ir: (ie[w], fidx(w, f, ir), 0)),
                pl.BlockSpec((1, 1, tf), lambda w, f, ie, ist, ir: (ie[w], 0, fidx(w, f, ir))),
                pl.BlockSpec((1, 1, tf), lambda w, f, ie, ist, ir: (ie[w], 0, nf + fidx(w, f, ir))),
                pl.BlockSpec((1, 1, d), lambda w, f, ie, ist, ir: (ie[w], 0, 0)),
            ],
            out_specs=pl.BlockSpec(memory_space=pl.ANY),
            scratch_shapes=[pltpu.VMEM((2, tm, half), I32), pltpu.VMEM((tm, d), F32),
                            pltpu.SemaphoreType.DMA((2,)), pltpu.SemaphoreType.DMA]),
        compiler_params=_cparams(("arbitrary", "arbitrary")),
        name="moe_experts",
    )(item_e, item_start, item_rows, xs, w_gate_up, w_gate_up, w_down,
      b_gate_up.reshape(n_exp, 1, two_f), b_gate_up.reshape(n_exp, 1, two_f),
      b_down.reshape(n_exp, 1, d))


def _combine_kernel(yg_ref, gate_ref, h_ref, g_ref, b_ref, o_ref):
    gates = gate_ref[...]
    half = yg_ref.shape[2]
    z_lo = DEEPNORM_ALPHA * h_ref[:, 0:half]
    z_hi = DEEPNORM_ALPHA * h_ref[:, half:2 * half]
    for r in range(TOP_K):
        lo, hi = _unpack_halves(yg_ref[r])
        z_lo = z_lo + gates[:, r:r + 1] * lo
        z_hi = z_hi + gates[:, r:r + 1] * hi
    inv_d = 1.0 / (2 * half)
    mu = (jnp.sum(z_lo, axis=-1, keepdims=True) + jnp.sum(z_hi, axis=-1, keepdims=True)) * inv_d
    c_lo = z_lo - mu
    c_hi = z_hi - mu
    var = (jnp.sum(c_lo * c_lo, axis=-1, keepdims=True)
           + jnp.sum(c_hi * c_hi, axis=-1, keepdims=True)) * inv_d
    rstd = lax.rsqrt(var + LN_EPS)
    o_ref[:, 0:half] = c_lo * rstd * g_ref[:, 0:half] + b_ref[:, 0:half]
    o_ref[:, half:2 * half] = c_hi * rstd * g_ref[:, half:2 * half] + b_ref[:, half:2 * half]


def _combine(yg, gates, h1, ln_g, ln_b, tc):
    s, d = h1.shape
    return pl.pallas_call(
        _combine_kernel,
        out_shape=jax.ShapeDtypeStruct((s, d), F32),
        grid=(s // tc,),
        in_specs=[pl.BlockSpec((TOP_K, tc, d // 2), lambda t: (0, t, 0)),
                  pl.BlockSpec((tc, TOP_K), lambda t: (t, 0)),
                  pl.BlockSpec((tc, d), lambda t: (t, 0)),
                  pl.BlockSpec((1, d), lambda t: (0, 0)),
                  pl.BlockSpec((1, d), lambda t: (0, 0))],
        out_specs=pl.BlockSpec((tc, d), lambda t: (t, 0)),
        compiler_params=_cparams(("arbitrary",)),
        name="moe_combine_ln2",
    )(yg, gates, h1, ln_g.reshape(1, d), ln_b.reshape(1, d))


def _tiles(s, d, ff, n_heads):
    return dict(
        proj_tm=min(512, s), select_tq=min(512, s), conv_ts=min(256, s), outproj_tm=min(512, s),
        attn_heads=min(4, n_heads),
        route_tr=min(256, s), combine_tc=min(256, s),
        expert_tm=min(1280, max(ROUTE_PAD, (s * TOP_K // 16) // ROUTE_PAD * ROUTE_PAD)),
        expert_tf=min(512, ff))


def kernel(x, w_in, conv_w, conv_b, conv_ln_g, conv_ln_b, w_out, ln1_g, ln1_b,
           w_router, b_router, w_gate_up, b_gate_up, w_down, b_down, ln2_g, ln2_b):
    b, s, d = x.shape
    assert b == 1, "the kernels treat the sequence of the single batch element as the row axis"
    g = w_in.shape[1] // 5
    n_heads = g // HEAD_DIM
    nb = s // MOBA_BLOCK
    n_exp = w_router.shape[1]
    ff = w_down.shape[1]
    assert s % MOBA_BLOCK == 0 and nb % 8 == 0 and HEAD_DIM + nb + 3 <= AUG_DIM
    tl = _tiles(s, d, ff, n_heads)
    x2 = x.reshape(s, d)

    kgg = _in_projection(x2, w_in, g, tl["proj_tm"])
    qv_t, conv = _qv_projection_and_conv(x2, w_in, kgg, conv_w, conv_b, conv_ln_g, conv_ln_b, g,
                                         tl["conv_ts"])
    kmean = _block_means(kgg, g, nb)
    slopes = 2.0 ** (-(8.0 / n_heads) * jnp.arange(1, n_heads + 1, dtype=F32))
    q_aug, k_aug = _select(slopes, qv_t, kmean, kgg, n_heads, nb, tl["select_tq"])
    attn = _attention(q_aug, k_aug, qv_t, n_heads, nb, tl["attn_heads"])
    h1, h1_packed, logits_t = _out_projection(attn, conv, w_out, x2, ln1_g, ln1_b,
                                              w_router, b_router, tl["outproj_tm"])

    e_t, gate_t, rank_t, counts = _route(logits_t, tl["route_tr"])
    counts = counts[:, 0]
    padded = (counts + ROUTE_PAD - 1) // ROUTE_PAD * ROUTE_PAD
    pad_end = jnp.cumsum(padded)
    pad_start = (pad_end - padded).astype(I32)
    cap = s * TOP_K + n_exp * ROUTE_PAD
    e_flat = e_t.reshape(-1)
    rank_flat = rank_t.reshape(-1)
    xs = _sc_dispatch(h1_packed, e_flat, rank_flat, pad_start, cap, s)

    tm = tl["expert_tm"]
    n_items = cap // tm + n_exp
    per_e = (padded + tm - 1) // tm
    item_end = jnp.cumsum(per_e)
    item_ids = jnp.arange(n_items, dtype=I32)
    item_e = jnp.minimum(jnp.sum(item_ids[:, None] >= item_end[None, :], axis=1),
                         n_exp - 1).astype(I32)
    piece = item_ids - (item_end - per_e)[item_e]
    active = item_ids < item_end[-1]
    item_start = jnp.where(active, pad_start[item_e] + piece * tm, 0).astype(I32)
    item_rows = jnp.where(active, jnp.minimum(tm, padded[item_e] - piece * tm), 0).astype(I32)
    last_e = item_e[jnp.maximum(item_end[-1] - 1, 0)]
    item_e = jnp.where(active, item_e, last_e).astype(I32)

    y = _experts(item_e, item_start, item_rows, item_end[-1].astype(I32), xs,
                 w_gate_up, b_gate_up, w_down, b_down, tm, tl["expert_tf"])
    yg = _sc_combine_gather(y, e_flat, rank_flat, pad_start).reshape(TOP_K, s, d // 2)
    out = _combine(yg, gate_t.T, h1, ln2_g, ln2_b, tl["combine_tc"])
    return out.reshape(b, s, d)
```

```python
import functools

import jax
import jax.numpy as jnp
from jax import lax
from jax.experimental import pallas as pl
from jax.experimental.pallas import tpu as pltpu
from jax.experimental.pallas import tpu_sc as plsc

F32 = jnp.float32
BF16 = jnp.bfloat16
I32 = jnp.int32

HEAD_DIM = 128
MOBA_BLOCK = 256
MOBA_TOPK = 3
CONV_WIDTH = 31
CONV_HALO = 32
SUBLANES = 8
LANES = 128
TOP_K = 4
SWIGLU_ALPHA = 1.702
SWIGLU_LIMIT = 7.0
LN_EPS = 1e-5
DEPTH = 1
DEEPNORM_ALPHA = (2.0 * DEPTH) ** 0.25
ROUTE_PAD = 128
AUG_DIM = 256
MASK_NEG = -1e30
ONES_ROWS = 16
ATTN_UNIT = 4
VMEM_LIMIT = 56 * 1024 * 1024


def _cparams(sem):
    return pltpu.CompilerParams(dimension_semantics=sem, vmem_limit_bytes=VMEM_LIMIT)


def _proj_nn_kernel(x_ref, w_ref, o_ref, wb_ref):
    @pl.when(pl.program_id(1) == 0)
    def _():
        wb_ref[...] = w_ref[...].astype(BF16)

    o_ref[...] = jnp.dot(x_ref[...].astype(BF16), wb_ref[...],
                         preferred_element_type=F32).astype(o_ref.dtype)


def _in_projection(x2, w_in, g, tm):
    s, d = x2.shape
    return pl.pallas_call(
        _proj_nn_kernel,
        out_shape=jax.ShapeDtypeStruct((s, 3 * g), BF16),
        grid=(3, s // tm),
        in_specs=[pl.BlockSpec((tm, d), lambda n, m: (m, 0)),
                  pl.BlockSpec((d, g), lambda n, m: (0, jnp.where(n == 0, 1, n + 2)))],
        out_specs=pl.BlockSpec((tm, g), lambda n, m: (m, n)),
        scratch_shapes=[pltpu.VMEM((d, g), BF16)],
        compiler_params=_cparams(("arbitrary", "arbitrary")),
        name="proj_nn",
    )(x2, w_in)


def _kmean_kernel(k_ref, o_ref, *, blocks):
    for b in range(blocks):
        kb = k_ref[b * MOBA_BLOCK:(b + 1) * MOBA_BLOCK, :].astype(F32)
        o_ref[b:b + 1, :] = jnp.sum(kb, axis=0, keepdims=True) * (1.0 / MOBA_BLOCK)


def _block_means(kgg, g, nb):
    blocks = 8
    return pl.pallas_call(
        functools.partial(_kmean_kernel, blocks=blocks),
        out_shape=jax.ShapeDtypeStruct((nb, g), F32),
        grid=(nb // blocks,),
        in_specs=[pl.BlockSpec((blocks * MOBA_BLOCK, g), lambda i: (i, 0))],
        out_specs=pl.BlockSpec((blocks, g), lambda i: (i, 0)),
        compiler_params=_cparams(("arbitrary",)),
        name="moba_kmean",
    )(kgg)


def _select_kernel(slopes_ref, qt_ref, km_ref, k_ref, qa_ref, ka_ref, *, nb, tq, n_heads):
    t = pl.program_id(0)
    col = t * tq + lax.broadcasted_iota(I32, (nb, tq), 1)
    qblk = col // MOBA_BLOCK
    j = lax.broadcasted_iota(I32, (nb, tq), 0)
    neg_inf = jnp.float32(-jnp.inf)
    past = j < qblk
    own = j == qblk
    n_extra = AUG_DIM - HEAD_DIM - nb
    r = lax.broadcasted_iota(I32, (n_extra, tq), 0)
    qb = ((t * tq + lax.broadcasted_iota(I32, (n_extra, tq), 1)) // MOBA_BLOCK).astype(F32)
    extra_unit = jnp.where(r == 0, 1.0,
                           jnp.where(r == 1, float(MOBA_BLOCK),
                                     jnp.where(r == 2, -float(MOBA_BLOCK) * qb, 0.0)))
    n_aug = AUG_DIM - HEAD_DIM
    pos = t * tq + lax.broadcasted_iota(I32, (tq, n_aug), 0)
    kblk = pos // MOBA_BLOCK
    pib = pos % MOBA_BLOCK
    lane = lax.broadcasted_iota(I32, (tq, n_aug), 1)
    aug = jnp.where(lane < nb, (lane == kblk).astype(F32),
                    jnp.where(lane == nb, pib.astype(F32),
                              jnp.where(lane == nb + 1, kblk.astype(F32),
                                        jnp.where(lane == nb + 2, 1.0, 0.0)))).astype(BF16)
    scale = HEAD_DIM ** -0.5
    for h in range(n_heads):
        cols = slice(h * HEAD_DIM, (h + 1) * HEAD_DIM)
        q_t = qt_ref[cols, :].astype(F32)
        gate = jnp.dot(km_ref[:, cols], q_t, preferred_element_type=F32,
                       precision=lax.Precision.HIGHEST)
        gcur = jnp.where(past, gate, neg_inf)
        sel = own
        for _ in range(MOBA_TOPK):
            m = jnp.max(gcur, axis=0, keepdims=True)
            is_max = jnp.logical_and(gcur == m, m > neg_inf)
            idx = jnp.min(jnp.where(is_max, j, nb), axis=0, keepdims=True)
            pick = j == idx
            sel = jnp.logical_or(sel, pick)
            gcur = jnp.where(pick, neg_inf, gcur)
        qa_ref[h, 0:HEAD_DIM, :] = (q_t * scale).astype(BF16)
        qa_ref[h, HEAD_DIM:HEAD_DIM + nb, :] = jnp.where(sel, 0.0, MASK_NEG).astype(BF16)
        qa_ref[h, HEAD_DIM + nb:AUG_DIM, :] = (slopes_ref[h] * extra_unit).astype(BF16)
        ka_ref[h, :, 0:HEAD_DIM] = k_ref[:, cols]
        ka_ref[h, :, HEAD_DIM:AUG_DIM] = aug


def _select(slopes, qv_t, kmean, kgg, n_heads, nb, tq):
    s = kgg.shape[0]
    g = n_heads * HEAD_DIM
    return pl.pallas_call(
        functools.partial(_select_kernel, nb=nb, tq=tq, n_heads=n_heads),
        out_shape=(jax.ShapeDtypeStruct((n_heads, AUG_DIM, s), BF16),
                   jax.ShapeDtypeStruct((n_heads, s, AUG_DIM), BF16)),
        grid_spec=pltpu.PrefetchScalarGridSpec(
            num_scalar_prefetch=1,
            grid=(s // tq,),
            in_specs=[pl.BlockSpec((g, tq), lambda t, sl: (0, t)),
                      pl.BlockSpec((nb, g), lambda t, sl: (0, 0)),
                      pl.BlockSpec((tq, g), lambda t, sl: (t, 0))],
            out_specs=[pl.BlockSpec((n_heads, AUG_DIM, tq), lambda t, sl: (0, 0, t)),
                       pl.BlockSpec((n_heads, tq, AUG_DIM), lambda t, sl: (0, t, 0))]),
        compiler_params=_cparams(("arbitrary",)),
        name="moba_select",
    )(slopes, qv_t, kmean, kgg)


def _attn_kernel(qa_ref, ka_ref, vt_ref, o_ref, sa_ref, sb_ref, *, heads):
    i = pl.program_id(1)
    tq = MOBA_BLOCK
    unit_keys = ATTN_UNIT * MOBA_BLOCK
    neg_inf = jnp.float32(-jnp.inf)

    def scores_of(j, nkeys):
        off = pl.multiple_of(j * MOBA_BLOCK, MOBA_BLOCK)
        return [jnp.dot(ka_ref[hh, pl.ds(off, nkeys), :], qa_ref[hh],
                        preferred_element_type=F32) for hh in range(heads)]

    def scores_into(ref, unit):
        for hh, s_t in enumerate(scores_of(ATTN_UNIT * unit, unit_keys)):
            ref[hh] = s_t

    def update(j, scores, carries, diagonal, nkeys):
        off = pl.multiple_of(j * MOBA_BLOCK, MOBA_BLOCK)
        stats = []
        for hh in range(heads):
            m = carries[hh][0]
            s_t = scores[hh]
            if diagonal:
                key = lax.broadcasted_iota(I32, s_t.shape, 0) - (nkeys - MOBA_BLOCK)
                qry = lax.broadcasted_iota(I32, s_t.shape, 1)
                s_t = jnp.where(key <= qry, s_t, neg_inf)
            m_new = jnp.maximum(m, jnp.max(s_t, axis=0, keepdims=True))
            stats.append((m_new, jnp.exp(m - m_new), jnp.exp(s_t - m_new).astype(BF16)))
        out = []
        ones = jnp.ones((ONES_ROWS, nkeys), BF16)
        for hh in range(heads):
            m_new, alpha, p = stats[hh]
            vb = vt_ref[hh * HEAD_DIM:(hh + 1) * HEAD_DIM, pl.ds(off, nkeys)]
            vb1 = jnp.concatenate([vb, ones], axis=0)
            acc_new = alpha * carries[hh][1] + jnp.dot(vb1, p, preferred_element_type=F32)
            out.append((m_new, acc_new))
        return tuple(out)

    init = tuple((jnp.full((1, tq), neg_inf, F32), jnp.zeros((HEAD_DIM + ONES_ROWS, tq), F32))
                 for _ in range(heads))
    def tail_update(r):
        nkeys = (r + 1) * MOBA_BLOCK
        return lambda cs: update(i - r, scores_of(i - r, nkeys), cs, True, nkeys)

    carries = lax.switch(i % ATTN_UNIT, [tail_update(r) for r in range(ATTN_UNIT)], init)

    n_units = i // ATTN_UNIT
    last_unit = jnp.maximum(n_units - 1, 0)

    @pl.when(n_units > 0)
    def _():
        scores_into(sa_ref, 0)

    def from_ref(ref):
        return [ref[hh] for hh in range(heads)]

    def two_units(k, cs):
        scores_into(sb_ref, 2 * k + 1)
        cs = update(2 * ATTN_UNIT * k, from_ref(sa_ref), cs, False, unit_keys)
        scores_into(sa_ref, jnp.minimum(2 * k + 2, last_unit))
        return update(2 * ATTN_UNIT * k + ATTN_UNIT, from_ref(sb_ref), cs, False, unit_keys)

    carries = lax.fori_loop(0, n_units // 2, two_units, carries)
    carries = lax.cond(
        n_units % 2 == 1,
        lambda cs: update(ATTN_UNIT * last_unit, from_ref(sa_ref), cs, False, unit_keys),
        lambda cs: cs, carries)
    for hh in range(heads):
        acc = carries[hh][1]
        out_t = acc[0:HEAD_DIM, :] / acc[HEAD_DIM:HEAD_DIM + 1, :]
        o_ref[:, hh * HEAD_DIM:(hh + 1) * HEAD_DIM] = out_t.T.astype(o_ref.dtype)


def _attention(q_aug, k_aug, qv_t, n_heads, nb, heads):
    s = k_aug.shape[1]
    groups = n_heads // heads
    once = pl.Buffered(1)
    return pl.pallas_call(
        functools.partial(_attn_kernel, heads=heads),
        out_shape=jax.ShapeDtypeStruct((s, n_heads * HEAD_DIM), BF16),
        grid=(groups, nb),
        in_specs=[pl.BlockSpec((heads, AUG_DIM, MOBA_BLOCK), lambda hg, i: (hg, 0, i)),
                  pl.BlockSpec((heads, s, AUG_DIM), lambda hg, i: (hg, 0, 0), pipeline_mode=once),
                  pl.BlockSpec((heads * HEAD_DIM, s), lambda hg, i: (groups + hg, 0),
                               pipeline_mode=once)],
        out_specs=pl.BlockSpec((MOBA_BLOCK, heads * HEAD_DIM), lambda hg, i: (i, hg)),
        scratch_shapes=[pltpu.VMEM((heads, ATTN_UNIT * MOBA_BLOCK, MOBA_BLOCK), F32),
                        pltpu.VMEM((heads, ATTN_UNIT * MOBA_BLOCK, MOBA_BLOCK), F32)],
        compiler_params=_cparams(("arbitrary", "arbitrary")),
        name="moba_attention",
    )(q_aug, k_aug, qv_t)


def _qv_conv_kernel(wt_ref, x_ref, ga_ref, gb_ref, w_ref, b_ref, g_ref, beta_ref, qv_ref, o_ref,
                    u_ref, ush_ref, *, ts, sub):
    t = pl.program_id(0)

    @pl.when(t == 0)
    def _():
        u_ref[0:CONV_HALO, :] = jnp.zeros((CONV_HALO, u_ref.shape[1]), F32)

    @pl.when(t > 0)
    def _():
        u_ref[0:CONV_HALO, :] = u_ref[ts:ts + CONV_HALO, :]

    x_b = x_ref[...].astype(BF16)
    n_slabs = min(8, ts // sub)
    groups_per_slab = (ts // sub) // n_slabs
    slab = wt_ref.shape[0] // n_slabs

    def project(k, anchor):
        rows = slice(k * slab, (k + 1) * slab)
        x_k = x_b if anchor is None else x_b + anchor
        qv_ref[rows, :] = lax.dot_general(wt_ref[rows, :], x_k, (((1,), (1,)), ((), ())),
                                          preferred_element_type=F32).astype(qv_ref.dtype)

    gb = gb_ref[...].astype(F32)
    u_ref[CONV_HALO:CONV_HALO + ts, :] = ga_ref[...].astype(F32) * jax.nn.sigmoid(gb)
    for b in range(SUBLANES):
        ush_ref[b, 0:ts + CONV_HALO - b, :] = u_ref[b:ts + CONV_HALO, :]
    first = CONV_HALO - (CONV_WIDTH - 1)
    anchor = None
    for r0 in range(0, ts, sub):
        if (r0 // sub) % groups_per_slab == 0:
            project(r0 // sub // groups_per_slab, anchor)
        acc = jnp.broadcast_to(b_ref[...], (sub, u_ref.shape[1]))
        for tap in range(CONV_WIDTH):
            off = first + tap
            aligned = r0 + off - off % SUBLANES
            acc = acc + w_ref[tap:tap + 1, :] * ush_ref[off % SUBLANES, aligned:aligned + sub, :]
        mu = jnp.mean(acc, axis=-1, keepdims=True)
        cen = acc - mu
        var = jnp.mean(cen * cen, axis=-1, keepdims=True)
        y = cen * lax.rsqrt(var + LN_EPS) * g_ref[...] + beta_ref[...]
        o_ref[r0:r0 + sub, :] = (y * jax.nn.sigmoid(y)).astype(o_ref.dtype)
        bits = pltpu.bitcast(y[0:1, 0:1], I32)
        zero = lax.shift_right_logical(lax.shift_right_logical(bits, jnp.int32(31)), jnp.int32(1))
        anchor = zero.astype(BF16)


def _qv_projection_and_conv(x2, w_in, kgg, conv_w, conv_b, conv_ln_g, conv_ln_b, g, ts):
    s, d = x2.shape
    row = lambda v: v.reshape(1, g).astype(F32)
    w_qv_t = jnp.concatenate([w_in[:, :g].astype(BF16), w_in[:, 2 * g:3 * g].astype(BF16)],
                             axis=1).T
    once = pl.Buffered(1)
    return pl.pallas_call(
        functools.partial(_qv_conv_kernel, ts=ts, sub=32),
        out_shape=(jax.ShapeDtypeStruct((2 * g, s), BF16), jax.ShapeDtypeStruct((s, g), BF16)),
        grid=(s // ts,),
        in_specs=[pl.BlockSpec((2 * g, d), lambda t: (0, 0), pipeline_mode=once),
                  pl.BlockSpec((ts, d), lambda t: (t, 0)),
                  pl.BlockSpec((ts, g), lambda t: (t, 1)),
                  pl.BlockSpec((ts, g), lambda t: (t, 2)),
                  pl.BlockSpec((CONV_WIDTH, g), lambda t: (0, 0)),
                  pl.BlockSpec((1, g), lambda t: (0, 0)),
                  pl.BlockSpec((1, g), lambda t: (0, 0)),
                  pl.BlockSpec((1, g), lambda t: (0, 0))],
        out_specs=[pl.BlockSpec((2 * g, ts), lambda t: (0, t)),
                   pl.BlockSpec((ts, g), lambda t: (t, 0))],
        scratch_shapes=[pltpu.VMEM((ts + CONV_HALO, g), F32),
                        pltpu.VMEM((SUBLANES, ts + CONV_HALO, g), F32)],
        compiler_params=_cparams(("arbitrary",)),
        name="qv_proj_conformer_conv",
    )(w_qv_t, x2, kgg, kgg, conv_w, row(conv_b), row(conv_ln_g), row(conv_ln_b))


def _layer_norm_rows(z, gain, bias):
    mu = jnp.mean(z, axis=-1, keepdims=True)
    cen = z - mu
    var = jnp.mean(cen * cen, axis=-1, keepdims=True)
    return cen * lax.rsqrt(var + LN_EPS) * gain + bias


HI_HALF = -65536


def _pack_halves(lo, hi):
    lo_bits = pltpu.bitcast(lo.astype(BF16).astype(F32), I32)
    hi_bits = pltpu.bitcast(hi.astype(BF16).astype(F32), I32)
    return lax.shift_right_logical(lo_bits, jnp.int32(16)) | (hi_bits & jnp.int32(HI_HALF))


def _unpack_halves(words):
    lo = pltpu.bitcast(lax.shift_left(words, jnp.int32(16)), F32)
    hi = pltpu.bitcast(words & jnp.int32(HI_HALF), F32)
    return lo, hi


def _outproj_kernel(attn_ref, conv_ref, wo_ref, x_ref, g_ref, b_ref, wr_ref, br_ref,
                    h_ref, hp_ref, lg_ref, *, g):
    tm = x_ref.shape[0]
    sub = min(tm, 256)
    half = x_ref.shape[1] // 2
    n_exp = lg_ref.shape[0]
    wr = wr_ref[...]
    w_hi = wr.astype(BF16)
    w_lo = (wr - w_hi.astype(F32)).astype(BF16)
    mixes = []
    for r0 in range(0, tm, sub):
        mix = jnp.dot(attn_ref[r0:r0 + sub, :], wo_ref[0:g, :], preferred_element_type=F32)
        mixes.append(mix + jnp.dot(conv_ref[r0:r0 + sub, :], wo_ref[g:2 * g, :],
                                   preferred_element_type=F32))
    for k, r0 in enumerate(range(0, tm, sub)):
        h1 = _layer_norm_rows(DEEPNORM_ALPHA * x_ref[r0:r0 + sub, :] + mixes[k],
                              g_ref[...], b_ref[...])
        h_ref[r0:r0 + sub, :] = h1
        hp_ref[r0:r0 + sub, :] = _pack_halves(h1[:, 0:half], h1[:, half:])
        h_hi = h1.astype(BF16)
        h_lo = (h1 - h_hi.astype(F32)).astype(BF16)
        lg = jnp.dot(h_hi, w_hi, preferred_element_type=F32)
        lg = lg + jnp.dot(h_lo, w_hi, preferred_element_type=F32)
        lg = lg + jnp.dot(h_hi, w_lo, preferred_element_type=F32)
        lg_ref[:, r0:r0 + sub] = lg.T[0:n_exp, :] + br_ref[...]


def _out_projection(attn, conv, w_out, x2, ln_g, ln_b, w_router, b_router, tm):
    s, d = x2.shape
    g = attn.shape[1]
    e = w_router.shape[1]
    return pl.pallas_call(
        functools.partial(_outproj_kernel, g=g),
        out_shape=(jax.ShapeDtypeStruct((s, d), F32), jax.ShapeDtypeStruct((s, d // 2), I32),
                   jax.ShapeDtypeStruct((e, s), F32)),
        grid=(s // tm,),
        in_specs=[pl.BlockSpec((tm, g), lambda m: (m, 0)),
                  pl.BlockSpec((tm, g), lambda m: (m, 0)),
                  pl.BlockSpec((2 * g, d), lambda m: (0, 0)),
                  pl.BlockSpec((tm, d), lambda m: (m, 0)),
                  pl.BlockSpec((1, d), lambda m: (0, 0)),
                  pl.BlockSpec((1, d), lambda m: (0, 0)),
                  pl.BlockSpec((d, LANES), lambda m: (0, 0)),
                  pl.BlockSpec((e, 1), lambda m: (0, 0))],
        out_specs=[pl.BlockSpec((tm, d), lambda m: (m, 0)),
                   pl.BlockSpec((tm, d // 2), lambda m: (m, 0)),
                   pl.BlockSpec((e, tm), lambda m: (0, m))],
        compiler_params=_cparams(("arbitrary",)),
        name="outproj_ln1_router",
    )(attn, conv, w_out.astype(BF16), x2, ln_g.reshape(1, d), ln_b.reshape(1, d),
      jnp.pad(w_router, ((0, 0), (0, LANES - e))), b_router.reshape(e, 1))


def _route_kernel(lg_ref, e_ref, gate_ref, rank_ref, cnt_ref, carry_ref, *, n_exp, tr):
    t = pl.program_id(0)

    @pl.when(t == 0)
    def _():
        carry_ref[...] = jnp.zeros_like(carry_ref)

    neg_inf = jnp.float32(-jnp.inf)
    cur = lg_ref[...]
    j = lax.broadcasted_iota(I32, (n_exp, tr), 0)
    vals, picks = [], []
    for r in range(TOP_K):
        m = jnp.max(cur, axis=0, keepdims=True)
        idx = jnp.min(jnp.where(cur == m, j, n_exp), axis=0, keepdims=True)
        pick = j == idx
        e_ref[r:r + 1, :] = idx
        vals.append(m)
        picks.append(pick)
        cur = jnp.where(pick, neg_inf, cur)
    exps = [jnp.exp(v - vals[0]) for v in vals]
    den = exps[0]
    for r in range(1, TOP_K):
        den = den + exps[r]
    for r in range(TOP_K):
        gate_ref[r:r + 1, :] = exps[r] / den

    chosen = picks[0].astype(F32)
    for r in range(1, TOP_K):
        chosen = chosen + picks[r].astype(F32)
    a = lax.broadcasted_iota(I32, (tr, tr), 0)
    b = lax.broadcasted_iota(I32, (tr, tr), 1)
    upper = (a < b).astype(BF16)
    excl = jnp.dot(chosen.astype(BF16), upper, preferred_element_type=F32)
    base = carry_ref[:, 0:1]
    rank = excl + base
    for r in range(TOP_K):
        rank_ref[r:r + 1, :] = jnp.sum(jnp.where(picks[r], rank, 0.0), axis=0,
                                       keepdims=True).astype(I32)
    total = base + jnp.sum(chosen, axis=1, keepdims=True)
    carry_ref[...] = jnp.broadcast_to(total, carry_ref.shape)
    cnt_ref[...] = jnp.broadcast_to(total, cnt_ref.shape).astype(I32)


def _route(logits_t, tr):
    n_exp, s = logits_t.shape
    return pl.pallas_call(
        functools.partial(_route_kernel, n_exp=n_exp, tr=tr),
        out_shape=(jax.ShapeDtypeStruct((TOP_K, s), I32), jax.ShapeDtypeStruct((TOP_K, s), F32),
                   jax.ShapeDtypeStruct((TOP_K, s), I32), jax.ShapeDtypeStruct((n_exp, 128), I32)),
        grid=(s // tr,),
        in_specs=[pl.BlockSpec((n_exp, tr), lambda t: (0, t))],
        out_specs=[pl.BlockSpec((TOP_K, tr), lambda t: (0, t)),
                   pl.BlockSpec((TOP_K, tr), lambda t: (0, t)),
                   pl.BlockSpec((TOP_K, tr), lambda t: (0, t)),
                   pl.BlockSpec((n_exp, 128), lambda t: (0, 0))],
        scratch_shapes=[pltpu.VMEM((n_exp, 128), F32)],
        compiler_params=_cparams(("arbitrary",)),
        name="moe_route",
    )(logits_t)


SC_CORES = 2
SC_SUBCORES = 16
SC_LANES = 16
SC_GATHER_ROWS = 32
SC_SCAN = 2048


_SC_PARAMS = pltpu.CompilerParams(needs_layout_passes=False)


def _sc_mesh():
    return plsc.VectorSubcoreMesh(core_axis_name="c", subcore_axis_name="s",
                                  num_cores=SC_CORES, num_subcores=SC_SUBCORES)


def _sc_worker():
    return lax.axis_index("s") * SC_CORES + lax.axis_index("c")


def _sc_gather_loop(table_hbm, idx_all, out_hbm, bufs, sems, base, n_chunks):
    chunk = SC_GATHER_ROWS

    def gather(ci, b):
        return pltpu.make_async_copy(
            table_hbm.at[idx_all.at[pl.ds(ci * chunk, chunk)]], bufs[b], sems[b])

    gather(0, 0).start()

    @pl.loop(0, n_chunks, step=2)
    def _(ci):
        gather(ci + 1, 1).start()
        gather(ci, 0).wait()
        pltpu.sync_copy(bufs[0], out_hbm.at[pl.ds(base + ci * chunk, chunk)])

        @pl.when(ci + 2 < n_chunks)
        def _():
            gather(ci + 2, 0).start()

        gather(ci + 1, 1).wait()
        pltpu.sync_copy(bufs[1], out_hbm.at[pl.ds(base + (ci + 1) * chunk, chunk)])


def _sc_dest(ps_v, e_buf, r_buf, v):
    e_vec = e_buf[pl.ds(v * SC_LANES, SC_LANES)]
    return plsc.load_gather(ps_v, [e_vec]) + r_buf[pl.ds(v * SC_LANES, SC_LANES)]


def _sc_dispatch(table, e_flat, rank_flat, pad_start, cap, s):
    n_assign, width = e_flat.shape[0], table.shape[1]
    n_workers = SC_CORES * SC_SUBCORES
    per_worker = cap // n_workers
    chunk = SC_GATHER_ROWS
    assert cap % (n_workers * 2 * chunk) == 0 and n_assign % SC_SCAN == 0
    assert per_worker % SC_LANES == 0

    @functools.partial(
        pl.kernel, mesh=_sc_mesh(), out_type=jax.ShapeDtypeStruct((cap, width), table.dtype),
        scratch_types=[pltpu.VMEM((per_worker,), I32), pltpu.VMEM((pad_start.shape[0],), I32),
                       pltpu.VMEM((SC_SCAN,), I32), pltpu.VMEM((SC_SCAN,), I32),
                       pltpu.VMEM((chunk, width), table.dtype),
                       pltpu.VMEM((chunk, width), table.dtype),
                       pltpu.SemaphoreType.DMA, pltpu.SemaphoreType.DMA],
        compiler_params=_SC_PARAMS, name="sc_dispatch")
    def dispatch(table_hbm, e_hbm, r_hbm, ps_hbm, out_hbm, idx_all, ps_v, e_buf, r_buf,
                 buf0, buf1, sem0, sem1):
        base = _sc_worker() * per_worker
        lane = lax.iota(I32, SC_LANES)
        pltpu.sync_copy(ps_hbm, ps_v)

        @pl.loop(0, per_worker // SC_LANES)
        def _(k):
            idx_all[pl.ds(k * SC_LANES, SC_LANES)] = lax.rem(base + k * SC_LANES + lane, s)

        @pl.loop(0, n_assign // SC_SCAN)
        def _(c):
            pltpu.sync_copy(e_hbm.at[pl.ds(c * SC_SCAN, SC_SCAN)], e_buf)
            pltpu.sync_copy(r_hbm.at[pl.ds(c * SC_SCAN, SC_SCAN)], r_buf)

            @pl.loop(0, SC_SCAN // SC_LANES)
            def _(v):
                loc = _sc_dest(ps_v, e_buf, r_buf, v) - base
                mine = jnp.logical_and(loc >= 0, loc < per_worker)
                tok = lax.rem(c * SC_SCAN + v * SC_LANES + lane, s)
                plsc.store_scatter(idx_all, [jnp.where(mine, loc, 0)], tok, mask=mine)

        _sc_gather_loop(table_hbm, idx_all, out_hbm, (buf0, buf1), (sem0, sem1), base,
                        per_worker // chunk)

    return dispatch(table, e_flat, rank_flat, pad_start)


def _sc_combine_gather(y, e_flat, rank_flat, pad_start):
    n_assign, width = e_flat.shape[0], y.shape[1]
    n_workers = SC_CORES * SC_SUBCORES
    per_worker = n_assign // n_workers
    chunk = SC_GATHER_ROWS
    assert n_assign % (n_workers * 2 * chunk) == 0

    @functools.partial(
        pl.kernel, mesh=_sc_mesh(), out_type=jax.ShapeDtypeStruct((n_assign, width), y.dtype),
        scratch_types=[pltpu.VMEM((per_worker,), I32), pltpu.VMEM((pad_start.shape[0],), I32),
                       pltpu.VMEM((per_worker,), I32), pltpu.VMEM((per_worker,), I32),
                       pltpu.VMEM((chunk, width), y.dtype), pltpu.VMEM((chunk, width), y.dtype),
                       pltpu.SemaphoreType.DMA, pltpu.SemaphoreType.DMA],
        compiler_params=_SC_PARAMS, name="sc_combine_gather")
    def combine(y_hbm, e_hbm, r_hbm, ps_hbm, out_hbm, idx_all, ps_v, e_buf, r_buf,
                buf0, buf1, sem0, sem1):
        base = _sc_worker() * per_worker
        pltpu.sync_copy(ps_hbm, ps_v)
        pltpu.sync_copy(e_hbm.at[pl.ds(base, per_worker)], e_buf)
        pltpu.sync_copy(r_hbm.at[pl.ds(base, per_worker)], r_buf)

        @pl.loop(0, per_worker // SC_LANES)
        def _(v):
            idx_all[pl.ds(v * SC_LANES, SC_LANES)] = _sc_dest(ps_v, e_buf, r_buf, v)

        _sc_gather_loop(y_hbm, idx_all, out_hbm, (buf0, buf1), (sem0, sem1), base,
                        per_worker // chunk)

    return combine(y, e_flat, rank_flat, pad_start)


def _expert_kernel(item_e, item_start, item_rows, xs_ref, wg_ref, wu_ref, wd_ref,
                   bg_ref, bu_ref, bd_ref, y_ref, xin_ref, acc_ref,
                   sem_x, sem_y, *, nf, n_items, big):
    w = pl.program_id(0)
    f = pl.program_id(1)
    rows = item_rows[w]
    slot = w % 2
    half = xin_ref.shape[2]
    has_next = jnp.logical_and(w + 1 < n_items, item_rows[jnp.minimum(w + 1, n_items - 1)] > 0)

    def x_copy(item, sl, c):
        src = pl.multiple_of(item_start[item] + c * ROUTE_PAD, ROUTE_PAD)
        dst = pl.multiple_of(c * ROUTE_PAD, ROUTE_PAD)
        return pltpu.make_async_copy(xs_ref.at[pl.ds(src, ROUTE_PAD)],
                                     xin_ref.at[sl, pl.ds(dst, ROUTE_PAD)], sem_x.at[sl])

    def y_copy(item, sl, c):
        src = pl.multiple_of(c * ROUTE_PAD, ROUTE_PAD)
        dst = pl.multiple_of(item_start[item] + c * ROUTE_PAD, ROUTE_PAD)
        return pltpu.make_async_copy(xin_ref.at[sl, pl.ds(src, ROUTE_PAD)],
                                     y_ref.at[pl.ds(dst, ROUTE_PAD)], sem_y)

    def for_chunks(item, fn):
        def body(c, carry):
            fn(c)
            return carry
        lax.fori_loop(0, item_rows[item] // ROUTE_PAD, body, 0)

    def compute(r0, size, first):
        lo, hi = _unpack_halves(xin_ref[slot, pl.ds(r0, size), :])
        lo = lo.astype(BF16)
        hi = hi.astype(BF16)

        def x_dot(w_ref):
            return (jnp.dot(lo, w_ref[0, 0:half, :].astype(BF16), preferred_element_type=F32)
                    + jnp.dot(hi, w_ref[0, half:2 * half, :].astype(BF16),
                              preferred_element_type=F32))

        gate = x_dot(wg_ref) + bg_ref[0]
        up = x_dot(wu_ref) + bu_ref[0]
        gate = jnp.minimum(gate, SWIGLU_LIMIT)
        up = jnp.clip(up, -SWIGLU_LIMIT, SWIGLU_LIMIT)
        act = (up + 1.0) * (gate * jax.nn.sigmoid(SWIGLU_ALPHA * gate))
        part = jnp.dot(act.astype(BF16), wd_ref[0].astype(BF16), preferred_element_type=F32)
        if first:
            acc_ref[pl.ds(r0, size), :] = part + bd_ref[0]
        else:
            acc_ref[pl.ds(r0, size), :] += part

    def compute_all(first):
        n_big = rows // big

        def body(c, carry):
            compute(pl.multiple_of(c * big, big), big, first)
            return carry
        lax.fori_loop(0, n_big, body, 0)
        done = n_big * big
        size = big // 2
        while size >= ROUTE_PAD:
            take = ((rows - done) & size) != 0
            here = done

            @pl.when(take)
            def _(here=here, size=size):
                compute(pl.multiple_of(here, ROUTE_PAD), size, first)
            done = done + jnp.where(take, size, 0)
            size //= 2

    @pl.when(rows > 0)
    def _():
        @pl.when(f == 0)
        def _():
            @pl.when(w == 0)
            def _():
                for_chunks(w, lambda c: x_copy(w, slot, c).start())

            for_chunks(w, lambda c: x_copy(w, slot, c).wait())

            @pl.when(w > 0)
            def _():
                for_chunks(w - 1, lambda c: y_copy(w - 1, 1 - slot, c).wait())

            @pl.when(has_next)
            def _():
                for_chunks(w + 1, lambda c: x_copy(w + 1, 1 - slot, c).start())

            compute_all(True)

        @pl.when(f > 0)
        def _():
            compute_all(False)

        @pl.when(f == nf - 1)
        def _():
            def pack(c):
                r0 = pl.multiple_of(c * ROUTE_PAD, ROUTE_PAD)
                xin_ref[slot, pl.ds(r0, ROUTE_PAD), :] = _pack_halves(
                    acc_ref[pl.ds(r0, ROUTE_PAD), 0:half],
                    acc_ref[pl.ds(r0, ROUTE_PAD), half:2 * half])
                y_copy(w, slot, c).start()
            for_chunks(w, pack)

            @pl.when(jnp.logical_not(has_next))
            def _():
                for_chunks(w, lambda c: y_copy(w, slot, c).wait())


def _experts(item_e, item_start, item_rows, n_active, xs, w_gate_up, b_gate_up, w_down, b_down,
             tm, tf):
    cap, half = xs.shape
    d = 2 * half
    n_exp, _, two_f = w_gate_up.shape
    ff = two_f // 2
    nf = ff // tf
    n_items = item_e.shape[0]
    big = min(512, tm)
    assert big % ROUTE_PAD == 0 and (big & (big - 1)) == 0

    def fidx(w, f, rows):
        return jnp.where(rows[w] > 0, f, nf - 1)

    return pl.pallas_call(
        functools.partial(_expert_kernel, nf=nf, n_items=n_items, big=big),
        out_shape=jax.ShapeDtypeStruct((cap, half), I32),
        grid_spec=pltpu.PrefetchScalarGridSpec(
            num_scalar_prefetch=3,
            grid=(n_active, nf),
            in_specs=[
                pl.BlockSpec(memory_space=pl.ANY),
                pl.BlockSpec((1, d, tf), lambda w, f, ie, ist, ir: (ie[w], 0, fidx(w, f, ir))),
                pl.BlockSpec((1, d, tf), lambda w, f, ie, ist, ir: (ie[w], 0, nf + fidx(w, f, ir))),
                pl.BlockSpec((1, tf, d), lambda w, f, ie, ist, ir: (ie[w], fidx(w, f, ir), 0)),
                pl.BlockSpec((1, 1, tf), lambda w, f, ie, ist, ir: (ie[w], 0, fidx(w, f, ir))),
                pl.BlockSpec((1, 1, tf), lambda w, f, ie, ist, ir: (ie[w], 0, nf + fidx(w, f, ir))),
                pl.BlockSpec((1, 1, d), lambda w, f, ie, ist, ir: (ie[w], 0, 0)),
            ],
            out_specs=pl.BlockSpec(memory_space=pl.ANY),
            scratch_shapes=[pltpu.VMEM((2, tm, half), I32), pltpu.VMEM((tm, d), F32),
                            pltpu.SemaphoreType.DMA((2,)), pltpu.SemaphoreType.DMA]),
        compiler_params=_cparams(("arbitrary", "arbitrary")),
        name="moe_experts",
    )(item_e, item_start, item_rows, xs, w_gate_up, w_gate_up, w_down,
      b_gate_up.reshape(n_exp, 1, two_f), b_gate_up.reshape(n_exp, 1, two_f),
      b_down.reshape(n_exp, 1, d))


def _combine_kernel(yg_ref, gate_ref, h_ref, g_ref, b_ref, o_ref):
    gates = gate_ref[...]
    half = yg_ref.shape[2]
    z_lo = DEEPNORM_ALPHA * h_ref[:, 0:half]
    z_hi = DEEPNORM_ALPHA * h_ref[:, half:2 * half]
    for r in range(TOP_K):
        lo, hi = _unpack_halves(yg_ref[r])
        z_lo = z_lo + gates[:, r:r + 1] * lo
        z_hi = z_hi + gates[:, r:r + 1] * hi
    inv_d = 1.0 / (2 * half)
    mu = (jnp.sum(z_lo, axis=-1, keepdims=True) + jnp.sum(z_hi, axis=-1, keepdims=True)) * inv_d
    c_lo = z_lo - mu
    c_hi = z_hi - mu
    var = (jnp.sum(c_lo * c_lo, axis=-1, keepdims=True)
           + jnp.sum(c_hi * c_hi, axis=-1, keepdims=True)) * inv_d
    rstd = lax.rsqrt(var + LN_EPS)
    o_ref[:, 0:half] = c_lo * rstd * g_ref[:, 0:half] + b_ref[:, 0:half]
    o_ref[:, half:2 * half] = c_hi * rstd * g_ref[:, half:2 * half] + b_ref[:, half:2 * half]


def _combine(yg, gates, h1, ln_g, ln_b, tc):
    s, d = h1.shape
    return pl.pallas_call(
        _combine_kernel,
        out_shape=jax.ShapeDtypeStruct((s, d), F32),
        grid=(s // tc,),
        in_specs=[pl.BlockSpec((TOP_K, tc, d // 2), lambda t: (0, t, 0)),
                  pl.BlockSpec((tc, TOP_K), lambda t: (t, 0)),
                  pl.BlockSpec((tc, d), lambda t: (t, 0)),
                  pl.BlockSpec((1, d), lambda t: (0, 0)),
                  pl.BlockSpec((1, d), lambda t: (0, 0))],
        out_specs=pl.BlockSpec((tc, d), lambda t: (t, 0)),
        compiler_params=_cparams(("arbitrary",)),
        name="moe_combine_ln2",
    )(yg, gates, h1, ln_g.reshape(1, d), ln_b.reshape(1, d))


def _tiles(s, d, ff, n_heads):
    return dict(
        proj_tm=min(512, s), select_tq=min(512, s), conv_ts=min(512, s), outproj_tm=min(512, s),
        attn_heads=min(4, n_heads),
        route_tr=min(256, s), combine_tc=min(256, s),
        expert_tm=min(1280, max(ROUTE_PAD, (s * TOP_K // 16) // ROUTE_PAD * ROUTE_PAD)),
        expert_tf=min(512, ff))


def kernel(x, w_in, conv_w, conv_b, conv_ln_g, conv_ln_b, w_out, ln1_g, ln1_b,
           w_router, b_router, w_gate_up, b_gate_up, w_down, b_down, ln2_g, ln2_b):
    b, s, d = x.shape
    assert b == 1, "the kernels treat the sequence of the single batch element as the row axis"
    g = w_in.shape[1] // 5
    n_heads = g // HEAD_DIM
    nb = s // MOBA_BLOCK
    n_exp = w_router.shape[1]
    ff = w_down.shape[1]
    assert s % MOBA_BLOCK == 0 and nb % 8 == 0 and HEAD_DIM + nb + 3 <= AUG_DIM
    tl = _tiles(s, d, ff, n_heads)
    x2 = x.reshape(s, d)

    kgg = _in_projection(x2, w_in, g, tl["proj_tm"])
    qv_t, conv = _qv_projection_and_conv(x2, w_in, kgg, conv_w, conv_b, conv_ln_g, conv_ln_b, g,
                                         tl["conv_ts"])
    kmean = _block_means(kgg, g, nb)
    slopes = 2.0 ** (-(8.0 / n_heads) * jnp.arange(1, n_heads + 1, dtype=F32))
    q_aug, k_aug = _select(slopes, qv_t, kmean, kgg, n_heads, nb, tl["select_tq"])
    attn = _attention(q_aug, k_aug, qv_t, n_heads, nb, tl["attn_heads"])
    h1, h1_packed, logits_t = _out_projection(attn, conv, w_out, x2, ln1_g, ln1_b,
                                              w_router, b_router, tl["outproj_tm"])

    e_t, gate_t, rank_t, counts = _route(logits_t, tl["route_tr"])
    counts = counts[:, 0]
    padded = (counts + ROUTE_PAD - 1) // ROUTE_PAD * ROUTE_PAD
    pad_end = jnp.cumsum(padded)
    pad_start = (pad_end - padded).astype(I32)
    cap = s * TOP_K + n_exp * ROUTE_PAD
    e_flat = e_t.reshape(-1)
    rank_flat = rank_t.reshape(-1)
    xs = _sc_dispatch(h1_packed, e_flat, rank_flat, pad_start, cap, s)

    tm = tl["expert_tm"]
    n_items = cap // tm + n_exp
    per_e = (padded + tm - 1) // tm
    item_end = jnp.cumsum(per_e)
    item_ids = jnp.arange(n_items, dtype=I32)
    item_e = jnp.minimum(jnp.sum(item_ids[:, None] >= item_end[None, :], axis=1),
                         n_exp - 1).astype(I32)
    piece = item_ids - (item_end - per_e)[item_e]
    active = item_ids < item_end[-1]
    item_start = jnp.where(active, pad_start[item_e] + piece * tm, 0).astype(I32)
    item_rows = jnp.where(active, jnp.minimum(tm, padded[item_e] - piece * tm), 0).astype(I32)
    last_e = item_e[jnp.maximum(item_end[-1] - 1, 0)]
    item_e = jnp.where(active, item_e, last_e).astype(I32)

    y = _experts(item_e, item_start, item_rows, item_end[-1].astype(I32), xs,
                 w_gate_up, b_gate_up, w_down, b_down, tm, tl["expert_tf"])
    yg = _sc_combine_gather(y, e_flat, rank_flat, pad_start).reshape(TOP_K, s, d // 2)
    out = _combine(yg, gate_t.T, h1, ln2_g, ln2_b, tl["combine_tc"])
    return out.reshape(b, s, d)
```

```python
import functools

import jax
import jax.numpy as jnp
from jax import lax
from jax.experimental import pallas as pl
from jax.experimental.pallas import tpu as pltpu
from jax.experimental.pallas import tpu_sc as plsc

F32 = jnp.float32
BF16 = jnp.bfloat16
I32 = jnp.int32

HEAD_DIM = 128
MOBA_BLOCK = 256
MOBA_TOPK = 3
CONV_WIDTH = 31
CONV_HALO = 32
SUBLANES = 8
LANES = 128
TOP_K = 4
SWIGLU_ALPHA = 1.702
SWIGLU_LIMIT = 7.0
LN_EPS = 1e-5
DEPTH = 1
DEEPNORM_ALPHA = (2.0 * DEPTH) ** 0.25
ROUTE_PAD = 128
AUG_DIM = 256
MASK_NEG = -1e30
ONES_ROWS = 16
ATTN_UNIT = 4
VMEM_LIMIT = 56 * 1024 * 1024


def _cparams(sem):
    return pltpu.CompilerParams(dimension_semantics=sem, vmem_limit_bytes=VMEM_LIMIT)


def _proj_nn_kernel(x_ref, w_ref, o_ref, wb_ref):
    @pl.when(pl.program_id(1) == 0)
    def _():
        wb_ref[...] = w_ref[...].astype(BF16)

    o_ref[...] = jnp.dot(x_ref[...].astype(BF16), wb_ref[...],
                         preferred_element_type=F32).astype(o_ref.dtype)


def _in_projection(x2, w_in, g, tm):
    s, d = x2.shape
    return pl.pallas_call(
        _proj_nn_kernel,
        out_shape=jax.ShapeDtypeStruct((s, 3 * g), BF16),
        grid=(3, s // tm),
        in_specs=[pl.BlockSpec((tm, d), lambda n, m: (m, 0)),
                  pl.BlockSpec((d, g), lambda n, m: (0, jnp.where(n == 0, 1, n + 2)))],
        out_specs=pl.BlockSpec((tm, g), lambda n, m: (m, n)),
        scratch_shapes=[pltpu.VMEM((d, g), BF16)],
        compiler_params=_cparams(("arbitrary", "arbitrary")),
        name="proj_nn",
    )(x2, w_in)


def _kmean_kernel(k_ref, o_ref, *, blocks):
    for b in range(blocks):
        kb = k_ref[b * MOBA_BLOCK:(b + 1) * MOBA_BLOCK, :].astype(F32)
        o_ref[b:b + 1, :] = jnp.sum(kb, axis=0, keepdims=True) * (1.0 / MOBA_BLOCK)


def _block_means(kgg, g, nb):
    blocks = 8
    return pl.pallas_call(
        functools.partial(_kmean_kernel, blocks=blocks),
        out_shape=jax.ShapeDtypeStruct((nb, g), F32),
        grid=(nb // blocks,),
        in_specs=[pl.BlockSpec((blocks * MOBA_BLOCK, g), lambda i: (i, 0))],
        out_specs=pl.BlockSpec((blocks, g), lambda i: (i, 0)),
        compiler_params=_cparams(("arbitrary",)),
        name="moba_kmean",
    )(kgg)


def _select_kernel(slopes_ref, qt_ref, km_ref, k_ref, qa_ref, ka_ref, *, nb, tq, n_heads):
    t = pl.program_id(0)
    col = t * tq + lax.broadcasted_iota(I32, (nb, tq), 1)
    qblk = col // MOBA_BLOCK
    j = lax.broadcasted_iota(I32, (nb, tq), 0)
    neg_inf = jnp.float32(-jnp.inf)
    past = j < qblk
    own = j == qblk
    n_extra = AUG_DIM - HEAD_DIM - nb
    r = lax.broadcasted_iota(I32, (n_extra, tq), 0)
    qb = ((t * tq + lax.broadcasted_iota(I32, (n_extra, tq), 1)) // MOBA_BLOCK).astype(F32)
    extra_unit = jnp.where(r == 0, 1.0,
                           jnp.where(r == 1, float(MOBA_BLOCK),
                                     jnp.where(r == 2, -float(MOBA_BLOCK) * qb, 0.0)))
    n_aug = AUG_DIM - HEAD_DIM
    pos = t * tq + lax.broadcasted_iota(I32, (tq, n_aug), 0)
    kblk = pos // MOBA_BLOCK
    pib = pos % MOBA_BLOCK
    lane = lax.broadcasted_iota(I32, (tq, n_aug), 1)
    aug = jnp.where(lane < nb, (lane == kblk).astype(F32),
                    jnp.where(lane == nb, pib.astype(F32),
                              jnp.where(lane == nb + 1, kblk.astype(F32),
                                        jnp.where(lane == nb + 2, 1.0, 0.0)))).astype(BF16)
    scale = HEAD_DIM ** -0.5
    for h in range(n_heads):
        cols = slice(h * HEAD_DIM, (h + 1) * HEAD_DIM)
        q_t = qt_ref[cols, :].astype(F32)
        gate = jnp.dot(km_ref[:, cols], q_t, preferred_element_type=F32,
                       precision=lax.Precision.HIGHEST)
        gcur = jnp.where(past, gate, neg_inf)
        sel = own
        for _ in range(MOBA_TOPK):
            m = jnp.max(gcur, axis=0, keepdims=True)
            is_max = jnp.logical_and(gcur == m, m > neg_inf)
            idx = jnp.min(jnp.where(is_max, j, nb), axis=0, keepdims=True)
            pick = j == idx
            sel = jnp.logical_or(sel, pick)
            gcur = jnp.where(pick, neg_inf, gcur)
        qa_ref[h, 0:HEAD_DIM, :] = (q_t * scale).astype(BF16)
        qa_ref[h, HEAD_DIM:HEAD_DIM + nb, :] = jnp.where(sel, 0.0, MASK_NEG).astype(BF16)
        qa_ref[h, HEAD_DIM + nb:AUG_DIM, :] = (slopes_ref[h] * extra_unit).astype(BF16)
        ka_ref[h, :, 0:HEAD_DIM] = k_ref[:, cols]
        ka_ref[h, :, HEAD_DIM:AUG_DIM] = aug


def _select(slopes, qv_t, kmean, kgg, n_heads, nb, tq):
    s = kgg.shape[0]
    g = n_heads * HEAD_DIM
    return pl.pallas_call(
        functools.partial(_select_kernel, nb=nb, tq=tq, n_heads=n_heads),
        out_shape=(jax.ShapeDtypeStruct((n_heads, AUG_DIM, s), BF16),
                   jax.ShapeDtypeStruct((n_heads, s, AUG_DIM), BF16)),
        grid_spec=pltpu.PrefetchScalarGridSpec(
            num_scalar_prefetch=1,
            grid=(s // tq,),
            in_specs=[pl.BlockSpec((g, tq), lambda t, sl: (0, t)),
                      pl.BlockSpec((nb, g), lambda t, sl: (0, 0)),
                      pl.BlockSpec((tq, g), lambda t, sl: (t, 0))],
            out_specs=[pl.BlockSpec((n_heads, AUG_DIM, tq), lambda t, sl: (0, 0, t)),
                       pl.BlockSpec((n_heads, tq, AUG_DIM), lambda t, sl: (0, t, 0))]),
        compiler_params=_cparams(("arbitrary",)),
        name="moba_select",
    )(slopes, qv_t, kmean, kgg)


def _attn_kernel(qa_ref, ka_ref, vt_ref, o_ref, sa_ref, sb_ref, *, heads):
    i = pl.program_id(1)
    tq = MOBA_BLOCK
    unit_keys = ATTN_UNIT * MOBA_BLOCK
    neg_inf = jnp.float32(-jnp.inf)

    def scores_of(j, nkeys):
        off = pl.multiple_of(j * MOBA_BLOCK, MOBA_BLOCK)
        return [jnp.dot(ka_ref[hh, pl.ds(off, nkeys), :], qa_ref[hh],
                        preferred_element_type=F32) for hh in range(heads)]

    def scores_into(ref, unit):
        for hh, s_t in enumerate(scores_of(ATTN_UNIT * unit, unit_keys)):
            ref[hh] = s_t

    def update(j, scores, carries, diagonal, nkeys):
        off = pl.multiple_of(j * MOBA_BLOCK, MOBA_BLOCK)
        stats = []
        for hh in range(heads):
            m = carries[hh][0]
            s_t = scores[hh]
            if diagonal:
                key = lax.broadcasted_iota(I32, s_t.shape, 0) - (nkeys - MOBA_BLOCK)
                qry = lax.broadcasted_iota(I32, s_t.shape, 1)
                s_t = jnp.where(key <= qry, s_t, neg_inf)
            m_new = jnp.maximum(m, jnp.max(s_t, axis=0, keepdims=True))
            stats.append((m_new, jnp.exp(m - m_new), jnp.exp(s_t - m_new).astype(BF16)))
        out = []
        ones = jnp.ones((ONES_ROWS, nkeys), BF16)
        for hh in range(heads):
            m_new, alpha, p = stats[hh]
            vb = vt_ref[hh * HEAD_DIM:(hh + 1) * HEAD_DIM, pl.ds(off, nkeys)]
            vb1 = jnp.concatenate([vb, ones], axis=0)
            acc_new = alpha * carries[hh][1] + jnp.dot(vb1, p, preferred_element_type=F32)
            out.append((m_new, acc_new))
        return tuple(out)

    init = tuple((jnp.full((1, tq), neg_inf, F32), jnp.zeros((HEAD_DIM + ONES_ROWS, tq), F32))
                 for _ in range(heads))
    def tail_update(r):
        nkeys = (r + 1) * MOBA_BLOCK
        return lambda cs: update(i - r, scores_of(i - r, nkeys), cs, True, nkeys)

    carries = lax.switch(i % ATTN_UNIT, [tail_update(r) for r in range(ATTN_UNIT)], init)

    n_units = i // ATTN_UNIT
    last_unit = jnp.maximum(n_units - 1, 0)

    @pl.when(n_units > 0)
    def _():
        scores_into(sa_ref, 0)

    def from_ref(ref):
        return [ref[hh] for hh in range(heads)]

    def two_units(k, cs):
        scores_into(sb_ref, 2 * k + 1)
        cs = update(2 * ATTN_UNIT * k, from_ref(sa_ref), cs, False, unit_keys)
        scores_into(sa_ref, jnp.minimum(2 * k + 2, last_unit))
        return update(2 * ATTN_UNIT * k + ATTN_UNIT, from_ref(sb_ref), cs, False, unit_keys)

    carries = lax.fori_loop(0, n_units // 2, two_units, carries)
    carries = lax.cond(
        n_units % 2 == 1,
        lambda cs: update(ATTN_UNIT * last_unit, from_ref(sa_ref), cs, False, unit_keys),
        lambda cs: cs, carries)
    for hh in range(heads):
        acc = carries[hh][1]
        out_t = acc[0:HEAD_DIM, :] / acc[HEAD_DIM:HEAD_DIM + 1, :]
        o_ref[:, hh * HEAD_DIM:(hh + 1) * HEAD_DIM] = out_t.T.astype(o_ref.dtype)


def _attention(q_aug, k_aug, qv_t, n_heads, nb, heads):
    s = k_aug.shape[1]
    groups = n_heads // heads
    once = pl.Buffered(1)
    return pl.pallas_call(
        functools.partial(_attn_kernel, heads=heads),
        out_shape=jax.ShapeDtypeStruct((s, n_heads * HEAD_DIM), BF16),
        grid=(groups, nb),
        in_specs=[pl.BlockSpec((heads, AUG_DIM, MOBA_BLOCK), lambda hg, i: (hg, 0, i)),
                  pl.BlockSpec((heads, s, AUG_DIM), lambda hg, i: (hg, 0, 0), pipeline_mode=once),
                  pl.BlockSpec((heads * HEAD_DIM, s), lambda hg, i: (groups + hg, 0),
                               pipeline_mode=once)],
        out_specs=pl.BlockSpec((MOBA_BLOCK, heads * HEAD_DIM), lambda hg, i: (i, hg)),
        scratch_shapes=[pltpu.VMEM((heads, ATTN_UNIT * MOBA_BLOCK, MOBA_BLOCK), F32),
                        pltpu.VMEM((heads, ATTN_UNIT * MOBA_BLOCK, MOBA_BLOCK), F32)],
        compiler_params=_cparams(("arbitrary", "arbitrary")),
        name="moba_attention",
    )(q_aug, k_aug, qv_t)


def _qv_conv_kernel(wt_ref, x_ref, ga_ref, gb_ref, w_ref, b_ref, g_ref, beta_ref, qv_ref, o_ref,
                    u_ref, ush_ref, *, ts, sub):
    t = pl.program_id(0)

    @pl.when(t == 0)
    def _():
        u_ref[0:CONV_HALO, :] = jnp.zeros((CONV_HALO, u_ref.shape[1]), F32)

    @pl.when(t > 0)
    def _():
        u_ref[0:CONV_HALO, :] = u_ref[ts:ts + CONV_HALO, :]

    x_b = x_ref[...].astype(BF16)
    n_slabs = min(8, ts // sub)
    groups_per_slab = (ts // sub) // n_slabs
    slab = wt_ref.shape[0] // n_slabs

    def project(k, anchor):
        rows = slice(k * slab, (k + 1) * slab)
        x_k = x_b if anchor is None else x_b + anchor
        qv_ref[rows, :] = lax.dot_general(wt_ref[rows, :], x_k, (((1,), (1,)), ((), ())),
                                          preferred_element_type=F32).astype(qv_ref.dtype)

    gb = gb_ref[...].astype(F32)
    u_ref[CONV_HALO:CONV_HALO + ts, :] = ga_ref[...].astype(F32) * jax.nn.sigmoid(gb)
    for b in range(SUBLANES):
        ush_ref[b, 0:ts + CONV_HALO - b, :] = u_ref[b:ts + CONV_HALO, :]
    first = CONV_HALO - (CONV_WIDTH - 1)
    anchor = None
    for r0 in range(0, ts, sub):
        if (r0 // sub) % groups_per_slab == 0:
            project(r0 // sub // groups_per_slab, anchor)
        acc = jnp.broadcast_to(b_ref[...], (sub, u_ref.shape[1]))
        for tap in range(CONV_WIDTH):
            off = first + tap
            aligned = r0 + off - off % SUBLANES
            acc = acc + w_ref[tap:tap + 1, :] * ush_ref[off % SUBLANES, aligned:aligned + sub, :]
        mu = jnp.mean(acc, axis=-1, keepdims=True)
        cen = acc - mu
        var = jnp.mean(cen * cen, axis=-1, keepdims=True)
        y = cen * lax.rsqrt(var + LN_EPS) * g_ref[...] + beta_ref[...]
        o_ref[r0:r0 + sub, :] = (y * jax.nn.sigmoid(y)).astype(o_ref.dtype)
        bits = pltpu.bitcast(y[0:1, 0:1], I32)
        zero = lax.shift_right_logical(lax.shift_right_logical(bits, jnp.int32(31)), jnp.int32(1))
        anchor = zero.astype(BF16)


def _transpose_cast_kernel(w_ref, o_ref):
    o_ref[...] = w_ref[...].T.astype(o_ref.dtype)


def _qv_weights_transposed(w_in, g):
    d = w_in.shape[0]
    cols = min(256, g)
    per_group = g // cols
    return pl.pallas_call(
        _transpose_cast_kernel,
        out_shape=jax.ShapeDtypeStruct((2 * g, d), BF16),
        grid=(2 * per_group,),
        in_specs=[pl.BlockSpec((d, cols), lambda c: (0, c + jnp.where(c >= per_group, per_group, 0)))],
        out_specs=pl.BlockSpec((cols, d), lambda c: (c, 0)),
        compiler_params=_cparams(("arbitrary",)),
        name="qv_weight_transpose",
    )(w_in)


def _qv_projection_and_conv(x2, w_in, kgg, conv_w, conv_b, conv_ln_g, conv_ln_b, g, ts):
    s, d = x2.shape
    row = lambda v: v.reshape(1, g).astype(F32)
    w_qv_t = _qv_weights_transposed(w_in, g)
    once = pl.Buffered(1)
    return pl.pallas_call(
        functools.partial(_qv_conv_kernel, ts=ts, sub=32),
        out_shape=(jax.ShapeDtypeStruct((2 * g, s), BF16), jax.ShapeDtypeStruct((s, g), BF16)),
        grid=(s // ts,),
        in_specs=[pl.BlockSpec((2 * g, d), lambda t: (0, 0), pipeline_mode=once),
                  pl.BlockSpec((ts, d), lambda t: (t, 0)),
                  pl.BlockSpec((ts, g), lambda t: (t, 1)),
                  pl.BlockSpec((ts, g), lambda t: (t, 2)),
                  pl.BlockSpec((CONV_WIDTH, g), lambda t: (0, 0)),
                  pl.BlockSpec((1, g), lambda t: (0, 0)),
                  pl.BlockSpec((1, g), lambda t: (0, 0)),
                  pl.BlockSpec((1, g), lambda t: (0, 0))],
        out_specs=[pl.BlockSpec((2 * g, ts), lambda t: (0, t)),
                   pl.BlockSpec((ts, g), lambda t: (t, 0))],
        scratch_shapes=[pltpu.VMEM((ts + CONV_HALO, g), F32),
                        pltpu.VMEM((SUBLANES, ts + CONV_HALO, g), F32)],
        compiler_params=_cparams(("arbitrary",)),
        name="qv_proj_conformer_conv",
    )(w_qv_t, x2, kgg, kgg, conv_w, row(conv_b), row(conv_ln_g), row(conv_ln_b))


def _layer_norm_rows(z, gain, bias):
    mu = jnp.mean(z, axis=-1, keepdims=True)
    cen = z - mu
    var = jnp.mean(cen * cen, axis=-1, keepdims=True)
    return cen * lax.rsqrt(var + LN_EPS) * gain + bias


HI_HALF = -65536


def _pack_halves(lo, hi):
    lo_bits = pltpu.bitcast(lo.astype(BF16).astype(F32), I32)
    hi_bits = pltpu.bitcast(hi.astype(BF16).astype(F32), I32)
    return lax.shift_right_logical(lo_bits, jnp.int32(16)) | (hi_bits & jnp.int32(HI_HALF))


def _unpack_halves(words):
    lo = pltpu.bitcast(lax.shift_left(words, jnp.int32(16)), F32)
    hi = pltpu.bitcast(words & jnp.int32(HI_HALF), F32)
    return lo, hi


def _outproj_kernel(attn_ref, conv_ref, wo_ref, x_ref, g_ref, b_ref, wr_ref, br_ref,
                    h_ref, hp_ref, lg_ref, *, g):
    tm = x_ref.shape[0]
    sub = min(tm, 256)
    half = x_ref.shape[1] // 2
    n_exp = lg_ref.shape[0]
    wr = wr_ref[...]
    w_hi = wr.astype(BF16)
    w_lo = (wr - w_hi.astype(F32)).astype(BF16)
    mixes = []
    for r0 in range(0, tm, sub):
        mix = jnp.dot(attn_ref[r0:r0 + sub, :], wo_ref[0:g, :], preferred_element_type=F32)
        mixes.append(mix + jnp.dot(conv_ref[r0:r0 + sub, :], wo_ref[g:2 * g, :],
                                   preferred_element_type=F32))
    for k, r0 in enumerate(range(0, tm, sub)):
        h1 = _layer_norm_rows(DEEPNORM_ALPHA * x_ref[r0:r0 + sub, :] + mixes[k],
                              g_ref[...], b_ref[...])
        h_ref[r0:r0 + sub, :] = h1
        hp_ref[r0:r0 + sub, :] = _pack_halves(h1[:, 0:half], h1[:, half:])
        h_hi = h1.astype(BF16)
        h_lo = (h1 - h_hi.astype(F32)).astype(BF16)
        lg = jnp.dot(h_hi, w_hi, preferred_element_type=F32)
        lg = lg + jnp.dot(h_lo, w_hi, preferred_element_type=F32)
        lg = lg + jnp.dot(h_hi, w_lo, preferred_element_type=F32)
        lg_ref[:, r0:r0 + sub] = lg.T[0:n_exp, :] + br_ref[...]


def _out_projection(attn, conv, w_out, x2, ln_g, ln_b, w_router, b_router, tm):
    s, d = x2.shape
    g = attn.shape[1]
    e = w_router.shape[1]
    return pl.pallas_call(
        functools.partial(_outproj_kernel, g=g),
        out_shape=(jax.ShapeDtypeStruct((s, d), F32), jax.ShapeDtypeStruct((s, d // 2), I32),
                   jax.ShapeDtypeStruct((e, s), F32)),
        grid=(s // tm,),
        in_specs=[pl.BlockSpec((tm, g), lambda m: (m, 0)),
                  pl.BlockSpec((tm, g), lambda m: (m, 0)),
                  pl.BlockSpec((2 * g, d), lambda m: (0, 0), pipeline_mode=pl.Buffered(1)),
                  pl.BlockSpec((tm, d), lambda m: (m, 0)),
                  pl.BlockSpec((1, d), lambda m: (0, 0)),
                  pl.BlockSpec((1, d), lambda m: (0, 0)),
                  pl.BlockSpec((d, LANES), lambda m: (0, 0)),
                  pl.BlockSpec((e, 1), lambda m: (0, 0))],
        out_specs=[pl.BlockSpec((tm, d), lambda m: (m, 0)),
                   pl.BlockSpec((tm, d // 2), lambda m: (m, 0)),
                   pl.BlockSpec((e, tm), lambda m: (0, m))],
        compiler_params=_cparams(("arbitrary",)),
        name="outproj_ln1_router",
    )(attn, conv, w_out.astype(BF16), x2, ln_g.reshape(1, d), ln_b.reshape(1, d),
      jnp.pad(w_router, ((0, 0), (0, LANES - e))), b_router.reshape(e, 1))


def _route_kernel(lg_ref, e_ref, gate_ref, rank_ref, cnt_ref, carry_ref, *, n_exp, tr):
    t = pl.program_id(0)

    @pl.when(t == 0)
    def _():
        carry_ref[...] = jnp.zeros_like(carry_ref)

    neg_inf = jnp.float32(-jnp.inf)
    cur = lg_ref[...]
    j = lax.broadcasted_iota(I32, (n_exp, tr), 0)
    vals, picks = [], []
    for r in range(TOP_K):
        m = jnp.max(cur, axis=0, keepdims=True)
        idx = jnp.min(jnp.where(cur == m, j, n_exp), axis=0, keepdims=True)
        pick = j == idx
        e_ref[r:r + 1, :] = idx
        vals.append(m)
        picks.append(pick)
        cur = jnp.where(pick, neg_inf, cur)
    exps = [jnp.exp(v - vals[0]) for v in vals]
    den = exps[0]
    for r in range(1, TOP_K):
        den = den + exps[r]
    for r in range(TOP_K):
        gate_ref[r:r + 1, :] = exps[r] / den

    chosen = picks[0].astype(F32)
    for r in range(1, TOP_K):
        chosen = chosen + picks[r].astype(F32)
    a = lax.broadcasted_iota(I32, (tr, tr), 0)
    b = lax.broadcasted_iota(I32, (tr, tr), 1)
    upper = (a < b).astype(BF16)
    excl = jnp.dot(chosen.astype(BF16), upper, preferred_element_type=F32)
    base = carry_ref[:, 0:1]
    rank = excl + base
    for r in range(TOP_K):
        rank_ref[r:r + 1, :] = jnp.sum(jnp.where(picks[r], rank, 0.0), axis=0,
                                       keepdims=True).astype(I32)
    total = base + jnp.sum(chosen, axis=1, keepdims=True)
    carry_ref[...] = jnp.broadcast_to(total, carry_ref.shape)
    cnt_ref[...] = jnp.broadcast_to(total, cnt_ref.shape).astype(I32)


def _route(logits_t, tr):
    n_exp, s = logits_t.shape
    return pl.pallas_call(
        functools.partial(_route_kernel, n_exp=n_exp, tr=tr),
        out_shape=(jax.ShapeDtypeStruct((TOP_K, s), I32), jax.ShapeDtypeStruct((TOP_K, s), F32),
                   jax.ShapeDtypeStruct((TOP_K, s), I32), jax.ShapeDtypeStruct((n_exp, 128), I32)),
        grid=(s // tr,),
        in_specs=[pl.BlockSpec((n_exp, tr), lambda t: (0, t))],
        out_specs=[pl.BlockSpec((TOP_K, tr), lambda t: (0, t)),
                   pl.BlockSpec((TOP_K, tr), lambda t: (0, t)),
                   pl.BlockSpec((TOP_K, tr), lambda t: (0, t)),
                   pl.BlockSpec((n_exp, 128), lambda t: (0, 0))],
        scratch_shapes=[pltpu.VMEM((n_exp, 128), F32)],
        compiler_params=_cparams(("arbitrary",)),
        name="moe_route",
    )(logits_t)


SC_CORES = 2
SC_SUBCORES = 16
SC_LANES = 16
SC_GATHER_ROWS = 32
SC_SCAN = 8192


_SC_PARAMS = pltpu.CompilerParams(needs_layout_passes=False)


def _sc_mesh():
    return plsc.VectorSubcoreMesh(core_axis_name="c", subcore_axis_name="s",
                                  num_cores=SC_CORES, num_subcores=SC_SUBCORES)


def _sc_worker():
    return lax.axis_index("s") * SC_CORES + lax.axis_index("c")


def _sc_gather_loop(table_hbm, idx_all, out_hbm, bufs, sems, base, n_chunks):
    chunk = SC_GATHER_ROWS

    def gather(ci, b):
        return pltpu.make_async_copy(
            table_hbm.at[idx_all.at[pl.ds(ci * chunk, chunk)]], bufs[b], sems[b])

    gather(0, 0).start()

    @pl.loop(0, n_chunks, step=2)
    def _(ci):
        gather(ci + 1, 1).start()
        gather(ci, 0).wait()
        pltpu.sync_copy(bufs[0], out_hbm.at[pl.ds(base + ci * chunk, chunk)])

        @pl.when(ci + 2 < n_chunks)
        def _():
            gather(ci + 2, 0).start()

        gather(ci + 1, 1).wait()
        pltpu.sync_copy(bufs[1], out_hbm.at[pl.ds(base + (ci + 1) * chunk, chunk)])


def _sc_dest(ps_v, e_buf, r_buf, v):
    e_vec = e_buf[pl.ds(v * SC_LANES, SC_LANES)]
    return plsc.load_gather(ps_v, [e_vec]) + r_buf[pl.ds(v * SC_LANES, SC_LANES)]


def _sc_dispatch(table, e_flat, rank_flat, pad_start, cap, s):
    n_assign, width = e_flat.shape[0], table.shape[1]
    n_workers = SC_CORES * SC_SUBCORES
    per_worker = cap // n_workers
    chunk = SC_GATHER_ROWS
    assert cap % (n_workers * 2 * chunk) == 0 and n_assign % SC_SCAN == 0
    assert per_worker % SC_LANES == 0 and s % SC_SCAN == 0

    @functools.partial(
        pl.kernel, mesh=_sc_mesh(), out_type=jax.ShapeDtypeStruct((cap, width), table.dtype),
        scratch_types=[pltpu.VMEM((per_worker,), I32), pltpu.VMEM((pad_start.shape[0],), I32),
                       pltpu.VMEM((SC_SCAN,), I32), pltpu.VMEM((SC_SCAN,), I32),
                       pltpu.VMEM((chunk, width), table.dtype),
                       pltpu.VMEM((chunk, width), table.dtype),
                       pltpu.SemaphoreType.DMA, pltpu.SemaphoreType.DMA],
        compiler_params=_SC_PARAMS, name="sc_dispatch")
    def dispatch(table_hbm, e_hbm, r_hbm, ps_hbm, out_hbm, idx_all, ps_v, e_buf, r_buf,
                 buf0, buf1, sem0, sem1):
        base = _sc_worker() * per_worker
        lane = lax.iota(I32, SC_LANES)
        pltpu.sync_copy(ps_hbm, ps_v)

        @pl.loop(0, per_worker // SC_LANES)
        def _(k):
            idx_all[pl.ds(k * SC_LANES, SC_LANES)] = lax.rem(base + k * SC_LANES + lane, s)

        @pl.loop(0, n_assign // SC_SCAN)
        def _(c):
            pltpu.sync_copy(e_hbm.at[pl.ds(c * SC_SCAN, SC_SCAN)], e_buf)
            pltpu.sync_copy(r_hbm.at[pl.ds(c * SC_SCAN, SC_SCAN)], r_buf)
            tok0 = lax.rem(c * SC_SCAN, s)

            @pl.loop(0, SC_SCAN // SC_LANES)
            def _(v):
                loc = _sc_dest(ps_v, e_buf, r_buf, v) - base
                mine = jnp.logical_and(loc >= 0, loc < per_worker)
                tok = tok0 + v * SC_LANES + lane
                plsc.store_scatter(idx_all, [jnp.where(mine, loc, 0)], tok, mask=mine)

        _sc_gather_loop(table_hbm, idx_all, out_hbm, (buf0, buf1), (sem0, sem1), base,
                        per_worker // chunk)

    return dispatch(table, e_flat, rank_flat, pad_start)


def _sc_combine_gather(y, e_flat, rank_flat, pad_start):
    n_assign, width = e_flat.shape[0], y.shape[1]
    n_workers = SC_CORES * SC_SUBCORES
    per_worker = n_assign // n_workers
    chunk = SC_GATHER_ROWS
    assert n_assign % (n_workers * 2 * chunk) == 0

    @functools.partial(
        pl.kernel, mesh=_sc_mesh(), out_type=jax.ShapeDtypeStruct((n_assign, width), y.dtype),
        scratch_types=[pltpu.VMEM((per_worker,), I32), pltpu.VMEM((pad_start.shape[0],), I32),
                       pltpu.VMEM((per_worker,), I32), pltpu.VMEM((per_worker,), I32),
                       pltpu.VMEM((chunk, width), y.dtype), pltpu.VMEM((chunk, width), y.dtype),
                       pltpu.SemaphoreType.DMA, pltpu.SemaphoreType.DMA],
        compiler_params=_SC_PARAMS, name="sc_combine_gather")
    def combine(y_hbm, e_hbm, r_hbm, ps_hbm, out_hbm, idx_all, ps_v, e_buf, r_buf,
                buf0, buf1, sem0, sem1):
        base = _sc_worker() * per_worker
        pltpu.sync_copy(ps_hbm, ps_v)
        pltpu.sync_copy(e_hbm.at[pl.ds(base, per_worker)], e_buf)
        pltpu.sync_copy(r_hbm.at[pl.ds(base, per_worker)], r_buf)

        @pl.loop(0, per_worker // SC_LANES)
        def _(v):
            idx_all[pl.ds(v * SC_LANES, SC_LANES)] = _sc_dest(ps_v, e_buf, r_buf, v)

        _sc_gather_loop(y_hbm, idx_all, out_hbm, (buf0, buf1), (sem0, sem1), base,
                        per_worker // chunk)

    return combine(y, e_flat, rank_flat, pad_start)


def _expert_kernel(item_e, item_start, item_rows, xs_ref, wg_ref, wu_ref, wd_ref,
                   bg_ref, bu_ref, bd_ref, y_ref, xin_ref, acc_ref,
                   sem_x, sem_y, *, nf, n_items, big):
    w = pl.program_id(0)
    f = pl.program_id(1)
    rows = item_rows[w]
    slot = w % 2
    half = xin_ref.shape[2]
    has_next = jnp.logical_and(w + 1 < n_items, item_rows[jnp.minimum(w + 1, n_items - 1)] > 0)

    def x_copy(item, sl, c):
        src = pl.multiple_of(item_start[item] + c * ROUTE_PAD, ROUTE_PAD)
        dst = pl.multiple_of(c * ROUTE_PAD, ROUTE_PAD)
        return pltpu.make_async_copy(xs_ref.at[pl.ds(src, ROUTE_PAD)],
                                     xin_ref.at[sl, pl.ds(dst, ROUTE_PAD)], sem_x.at[sl])

    def y_copy(item, sl, c):
        src = pl.multiple_of(c * ROUTE_PAD, ROUTE_PAD)
        dst = pl.multiple_of(item_start[item] + c * ROUTE_PAD, ROUTE_PAD)
        return pltpu.make_async_copy(xin_ref.at[sl, pl.ds(src, ROUTE_PAD)],
                                     y_ref.at[pl.ds(dst, ROUTE_PAD)], sem_y)

    def for_chunks(item, fn):
        def body(c, carry):
            fn(c)
            return carry
        lax.fori_loop(0, item_rows[item] // ROUTE_PAD, body, 0)

    def compute(r0, size, first):
        lo, hi = _unpack_halves(xin_ref[slot, pl.ds(r0, size), :])
        lo = lo.astype(BF16)
        hi = hi.astype(BF16)

        def x_dot(w_ref):
            return (jnp.dot(lo, w_ref[0, 0:half, :].astype(BF16), preferred_element_type=F32)
                    + jnp.dot(hi, w_ref[0, half:2 * half, :].astype(BF16),
                              preferred_element_type=F32))

        gate = x_dot(wg_ref) + bg_ref[0]
        up = x_dot(wu_ref) + bu_ref[0]
        gate = jnp.minimum(gate, SWIGLU_LIMIT)
        up = jnp.clip(up, -SWIGLU_LIMIT, SWIGLU_LIMIT)
        act = (up + 1.0) * (gate * jax.nn.sigmoid(SWIGLU_ALPHA * gate))
        part = jnp.dot(act.astype(BF16), wd_ref[0].astype(BF16), preferred_element_type=F32)
        if first:
            acc_ref[pl.ds(r0, size), :] = part + bd_ref[0]
        else:
            acc_ref[pl.ds(r0, size), :] += part

    def compute_all(first):
        n_big = rows // big

        def body(c, carry):
            compute(pl.multiple_of(c * big, big), big, first)
            return carry
        lax.fori_loop(0, n_big, body, 0)
        done = n_big * big
        size = big // 2
        while size >= ROUTE_PAD:
            take = ((rows - done) & size) != 0
            here = done

            @pl.when(take)
            def _(here=here, size=size):
                compute(pl.multiple_of(here, ROUTE_PAD), size, first)
            done = done + jnp.where(take, size, 0)
            size //= 2

    @pl.when(rows > 0)
    def _():
        @pl.when(f == 0)
        def _():
            @pl.when(w == 0)
            def _():
                for_chunks(w, lambda c: x_copy(w, slot, c).start())

            for_chunks(w, lambda c: x_copy(w, slot, c).wait())

            @pl.when(w > 0)
            def _():
                for_chunks(w - 1, lambda c: y_copy(w - 1, 1 - slot, c).wait())

            @pl.when(has_next)
            def _():
                for_chunks(w + 1, lambda c: x_copy(w + 1, 1 - slot, c).start())

            compute_all(True)

        @pl.when(f > 0)
        def _():
            compute_all(False)

        @pl.when(f == nf - 1)
        def _():
            def pack(c):
                r0 = pl.multiple_of(c * ROUTE_PAD, ROUTE_PAD)
                xin_ref[slot, pl.ds(r0, ROUTE_PAD), :] = _pack_halves(
                    acc_ref[pl.ds(r0, ROUTE_PAD), 0:half],
                    acc_ref[pl.ds(r0, ROUTE_PAD), half:2 * half])
                y_copy(w, slot, c).start()
            for_chunks(w, pack)

            @pl.when(jnp.logical_not(has_next))
            def _():
                for_chunks(w, lambda c: y_copy(w, slot, c).wait())


def _experts(item_e, item_start, item_rows, n_active, xs, w_gate_up, b_gate_up, w_down, b_down,
             tm, tf):
    cap, half = xs.shape
    d = 2 * half
    n_exp, _, two_f = w_gate_up.shape
    ff = two_f // 2
    nf = ff // tf
    n_items = item_e.shape[0]
    big = min(512, tm)
    assert big % ROUTE_PAD == 0 and (big & (big - 1)) == 0

    def fidx(w, f, rows):
        return jnp.where(rows[w] > 0, f, nf - 1)

    return pl.pallas_call(
        functools.partial(_expert_kernel, nf=nf, n_items=n_items, big=big),
        out_shape=jax.ShapeDtypeStruct((cap, half), I32),
        grid_spec=pltpu.PrefetchScalarGridSpec(
            num_scalar_prefetch=3,
            grid=(n_active, nf),
            in_specs=[
                pl.BlockSpec(memory_space=pl.ANY),
                pl.BlockSpec((1, d, tf), lambda w, f, ie, ist, ir: (ie[w], 0, fidx(w, f, ir))),
                pl.BlockSpec((1, d, tf), lambda w, f, ie, ist, ir: (ie[w], 0, nf + fidx(w, f, ir))),
                pl.BlockSpec((1, tf, d), lambda w, f, ie, ist, ir: (ie[w], fidx(w, f, ir), 0)),
                pl.BlockSpec((1, 1, tf), lambda w, f, ie, ist, ir: (ie[w], 0, fidx(w, f, ir))),
                pl.BlockSpec((1, 1, tf), lambda w, f, ie, ist, ir: (ie[w], 0, nf + fidx(w, f, ir))),
                pl.BlockSpec((1, 1, d), lambda w, f, ie, ist, ir: (ie[w], 0, 0)),
            ],
            out_specs=pl.BlockSpec(memory_space=pl.ANY),
            scratch_shapes=[pltpu.VMEM((2, tm, half), I32), pltpu.VMEM((tm, d), F32),
                            pltpu.SemaphoreType.DMA((2,)), pltpu.SemaphoreType.DMA]),
        compiler_params=_cparams(("arbitrary", "arbitrary")),
        name="moe_experts",
    )(item_e, item_start, item_rows, xs, w_gate_up, w_gate_up, w_down,
      b_gate_up.reshape(n_exp, 1, two_f), b_gate_up.reshape(n_exp, 1, two_f),
      b_down.reshape(n_exp, 1, d))


def _combine_kernel(yg_ref, gate_ref, h_ref, g_ref, b_ref, o_ref):
    gates = gate_ref[...]
    half = yg_ref.shape[2]
    z_lo = DEEPNORM_ALPHA * h_ref[:, 0:half]
    z_hi = DEEPNORM_ALPHA * h_ref[:, half:2 * half]
    for r in range(TOP_K):
        lo, hi = _unpack_halves(yg_ref[r])
        z_lo = z_lo + gates[:, r:r + 1] * lo
        z_hi = z_hi + gates[:, r:r + 1] * hi
    inv_d = 1.0 / (2 * half)
    mu = (jnp.sum(z_lo, axis=-1, keepdims=True) + jnp.sum(z_hi, axis=-1, keepdims=True)) * inv_d
    c_lo = z_lo - mu
    c_hi = z_hi - mu
    var = (jnp.sum(c_lo * c_lo, axis=-1, keepdims=True)
           + jnp.sum(c_hi * c_hi, axis=-1, keepdims=True)) * inv_d
    rstd = lax.rsqrt(var + LN_EPS)
    o_ref[:, 0:half] = c_lo * rstd * g_ref[:, 0:half] + b_ref[:, 0:half]
    o_ref[:, half:2 * half] = c_hi * rstd * g_ref[:, half:2 * half] + b_ref[:, half:2 * half]


def _combine(yg, gates, h1, ln_g, ln_b, tc):
    s, d = h1.shape
    return pl.pallas_call(
        _combine_kernel,
        out_shape=jax.ShapeDtypeStruct((s, d), F32),
        grid=(s // tc,),
        in_specs=[pl.BlockSpec((TOP_K, tc, d // 2), lambda t: (0, t, 0)),
                  pl.BlockSpec((tc, TOP_K), lambda t: (t, 0)),
                  pl.BlockSpec((tc, d), lambda t: (t, 0)),
                  pl.BlockSpec((1, d), lambda t: (0, 0)),
                  pl.BlockSpec((1, d), lambda t: (0, 0))],
        out_specs=pl.BlockSpec((tc, d), lambda t: (t, 0)),
        compiler_params=_cparams(("arbitrary",)),
        name="moe_combine_ln2",
    )(yg, gates, h1, ln_g.reshape(1, d), ln_b.reshape(1, d))


def _tiles(s, d, ff, n_heads):
    return dict(
        proj_tm=min(512, s), select_tq=min(512, s), conv_ts=min(512, s), outproj_tm=min(512, s),
        attn_heads=min(4, n_heads),
        route_tr=min(256, s), combine_tc=min(256, s),
        expert_tm=min(1280, max(ROUTE_PAD, (s * TOP_K // 16) // ROUTE_PAD * ROUTE_PAD)),
        expert_tf=min(512, ff))


def kernel(x, w_in, conv_w, conv_b, conv_ln_g, conv_ln_b, w_out, ln1_g, ln1_b,
           w_router, b_router, w_gate_up, b_gate_up, w_down, b_down, ln2_g, ln2_b):
    b, s, d = x.shape
    assert b == 1, "the kernels treat the sequence of the single batch element as the row axis"
    g = w_in.shape[1] // 5
    n_heads = g // HEAD_DIM
    nb = s // MOBA_BLOCK
    n_exp = w_router.shape[1]
    ff = w_down.shape[1]
    assert s % MOBA_BLOCK == 0 and nb % 8 == 0 and HEAD_DIM + nb + 3 <= AUG_DIM
    tl = _tiles(s, d, ff, n_heads)
    x2 = x.reshape(s, d)

    kgg = _in_projection(x2, w_in, g, tl["proj_tm"])
    qv_t, conv = _qv_projection_and_conv(x2, w_in, kgg, conv_w, conv_b, conv_ln_g, conv_ln_b, g,
                                         tl["conv_ts"])
    kmean = _block_means(kgg, g, nb)
    slopes = 2.0 ** (-(8.0 / n_heads) * jnp.arange(1, n_heads + 1, dtype=F32))
    q_aug, k_aug = _select(slopes, qv_t, kmean, kgg, n_heads, nb, tl["select_tq"])
    attn = _attention(q_aug, k_aug, qv_t, n_heads, nb, tl["attn_heads"])
    h1, h1_packed, logits_t = _out_projection(attn, conv, w_out, x2, ln1_g, ln1_b,
                                              w_router, b_router, tl["outproj_tm"])

    e_t, gate_t, rank_t, counts = _route(logits_t, tl["route_tr"])
    counts = counts[:, 0]
    padded = (counts + ROUTE_PAD - 1) // ROUTE_PAD * ROUTE_PAD
    pad_end = jnp.cumsum(padded)
    pad_start = (pad_end - padded).astype(I32)
    cap = s * TOP_K + n_exp * ROUTE_PAD
    e_flat = e_t.reshape(-1)
    rank_flat = rank_t.reshape(-1)
    xs = _sc_dispatch(h1_packed, e_flat, rank_flat, pad_start, cap, s)

    tm = tl["expert_tm"]
    n_items = cap // tm + n_exp
    per_e = (padded + tm - 1) // tm
    item_end = jnp.cumsum(per_e)
    item_ids = jnp.arange(n_items, dtype=I32)
    item_e = jnp.minimum(jnp.sum(item_ids[:, None] >= item_end[None, :], axis=1),
                         n_exp - 1).astype(I32)
    piece = item_ids - (item_end - per_e)[item_e]
    active = item_ids < item_end[-1]
    item_start = jnp.where(active, pad_start[item_e] + piece * tm, 0).astype(I32)
    item_rows = jnp.where(active, jnp.minimum(tm, padded[item_e] - piece * tm), 0).astype(I32)
    last_e = item_e[jnp.maximum(item_end[-1] - 1, 0)]
    item_e = jnp.where(active, item_e, last_e).astype(I32)

    y = _experts(item_e, item_start, item_rows, item_end[-1].astype(I32), xs,
                 w_gate_up, b_gate_up, w_down, b_down, tm, tl["expert_tf"])
    yg = _sc_combine_gather(y, e_flat, rank_flat, pad_start).reshape(TOP_K, s, d // 2)
    out = _combine(yg, gate_t.T, h1, ln2_g, ln2_b, tl["combine_tc"])
    return out.reshape(b, s, d)
```

```python
import functools

import jax
import jax.numpy as jnp
from jax import lax
from jax.experimental import pallas as pl
from jax.experimental.pallas import tpu as pltpu
from jax.experimental.pallas import tpu_sc as plsc

F32 = jnp.float32
BF16 = jnp.bfloat16
I32 = jnp.int32

HEAD_DIM = 128
MOBA_BLOCK = 256
MOBA_TOPK = 3
CONV_WIDTH = 31
CONV_HALO = 32
SUBLANES = 8
LANES = 128
TOP_K = 4
SWIGLU_ALPHA = 1.702
SWIGLU_LIMIT = 7.0
LN_EPS = 1e-5
DEPTH = 1
DEEPNORM_ALPHA = (2.0 * DEPTH) ** 0.25
ROUTE_PAD = 128
AUG_DIM = 256
MASK_NEG = -1e30
ONES_ROWS = 16
ATTN_UNIT = 4
COMBINE_PARTS = 2
VMEM_LIMIT = 56 * 1024 * 1024


def _cparams(sem):
    return pltpu.CompilerParams(dimension_semantics=sem, vmem_limit_bytes=VMEM_LIMIT)


def _proj_nn_kernel(x_ref, w_ref, o_ref, wb_ref):
    @pl.when(pl.program_id(1) == 0)
    def _():
        wb_ref[...] = w_ref[...].astype(BF16)

    o_ref[...] = jnp.dot(x_ref[...].astype(BF16), wb_ref[...],
                         preferred_element_type=F32).astype(o_ref.dtype)


def _in_projection(x2, w_in, g, tm):
    s, d = x2.shape
    return pl.pallas_call(
        _proj_nn_kernel,
        out_shape=jax.ShapeDtypeStruct((s, 3 * g), BF16),
        grid=(3, s // tm),
        in_specs=[pl.BlockSpec((tm, d), lambda n, m: (m, 0)),
                  pl.BlockSpec((d, g), lambda n, m: (0, jnp.where(n == 0, 1, n + 2)))],
        out_specs=pl.BlockSpec((tm, g), lambda n, m: (m, n)),
        scratch_shapes=[pltpu.VMEM((d, g), BF16)],
        compiler_params=_cparams(("arbitrary", "arbitrary")),
        name="proj_nn",
    )(x2, w_in)


def _kmean_kernel(k_ref, o_ref, *, blocks):
    for b in range(blocks):
        kb = k_ref[b * MOBA_BLOCK:(b + 1) * MOBA_BLOCK, :].astype(F32)
        o_ref[b:b + 1, :] = jnp.sum(kb, axis=0, keepdims=True) * (1.0 / MOBA_BLOCK)


def _block_means(kgg, g, nb):
    blocks = 8
    return pl.pallas_call(
        functools.partial(_kmean_kernel, blocks=blocks),
        out_shape=jax.ShapeDtypeStruct((nb, g), F32),
        grid=(nb // blocks,),
        in_specs=[pl.BlockSpec((blocks * MOBA_BLOCK, g), lambda i: (i, 0))],
        out_specs=pl.BlockSpec((blocks, g), lambda i: (i, 0)),
        compiler_params=_cparams(("arbitrary",)),
        name="moba_kmean",
    )(kgg)


def _select_kernel(slopes_ref, qt_ref, km_ref, k_ref, qa_ref, ka_ref, *, nb, tq, n_heads):
    t = pl.program_id(0)
    col = t * tq + lax.broadcasted_iota(I32, (nb, tq), 1)
    qblk = col // MOBA_BLOCK
    j = lax.broadcasted_iota(I32, (nb, tq), 0)
    neg_inf = jnp.float32(-jnp.inf)
    past = j < qblk
    own = j == qblk
    n_extra = AUG_DIM - HEAD_DIM - nb
    r = lax.broadcasted_iota(I32, (n_extra, tq), 0)
    qb = ((t * tq + lax.broadcasted_iota(I32, (n_extra, tq), 1)) // MOBA_BLOCK).astype(F32)
    extra_unit = jnp.where(r == 0, 1.0,
                           jnp.where(r == 1, float(MOBA_BLOCK),
                                     jnp.where(r == 2, -float(MOBA_BLOCK) * qb, 0.0)))
    n_aug = AUG_DIM - HEAD_DIM
    pos = t * tq + lax.broadcasted_iota(I32, (tq, n_aug), 0)
    kblk = pos // MOBA_BLOCK
    pib = pos % MOBA_BLOCK
    lane = lax.broadcasted_iota(I32, (tq, n_aug), 1)
    aug = jnp.where(lane < nb, (lane == kblk).astype(F32),
                    jnp.where(lane == nb, pib.astype(F32),
                              jnp.where(lane == nb + 1, kblk.astype(F32),
                                        jnp.where(lane == nb + 2, 1.0, 0.0)))).astype(BF16)
    scale = HEAD_DIM ** -0.5
    for h in range(n_heads):
        cols = slice(h * HEAD_DIM, (h + 1) * HEAD_DIM)
        q_t = qt_ref[cols, :].astype(F32)
        gate = jnp.dot(km_ref[:, cols], q_t, preferred_element_type=F32,
                       precision=lax.Precision.HIGHEST)
        gcur = jnp.where(past, gate, neg_inf)
        sel = own
        for _ in range(MOBA_TOPK):
            m = jnp.max(gcur, axis=0, keepdims=True)
            is_max = jnp.logical_and(gcur == m, m > neg_inf)
            idx = jnp.min(jnp.where(is_max, j, nb), axis=0, keepdims=True)
            pick = j == idx
            sel = jnp.logical_or(sel, pick)
            gcur = jnp.where(pick, neg_inf, gcur)
        qa_ref[h, 0:HEAD_DIM, :] = (q_t * scale).astype(BF16)
        qa_ref[h, HEAD_DIM:HEAD_DIM + nb, :] = jnp.where(sel, 0.0, MASK_NEG).astype(BF16)
        qa_ref[h, HEAD_DIM + nb:AUG_DIM, :] = (slopes_ref[h] * extra_unit).astype(BF16)
        ka_ref[h, :, 0:HEAD_DIM] = k_ref[:, cols]
        ka_ref[h, :, HEAD_DIM:AUG_DIM] = aug


def _select(slopes, qv_t, kmean, kgg, n_heads, nb, tq):
    s = kgg.shape[0]
    g = n_heads * HEAD_DIM
    return pl.pallas_call(
        functools.partial(_select_kernel, nb=nb, tq=tq, n_heads=n_heads),
        out_shape=(jax.ShapeDtypeStruct((n_heads, AUG_DIM, s), BF16),
                   jax.ShapeDtypeStruct((n_heads, s, AUG_DIM), BF16)),
        grid_spec=pltpu.PrefetchScalarGridSpec(
            num_scalar_prefetch=1,
            grid=(s // tq,),
            in_specs=[pl.BlockSpec((g, tq), lambda t, sl: (0, t)),
                      pl.BlockSpec((nb, g), lambda t, sl: (0, 0)),
                      pl.BlockSpec((tq, g), lambda t, sl: (t, 0))],
            out_specs=[pl.BlockSpec((n_heads, AUG_DIM, tq), lambda t, sl: (0, 0, t)),
                       pl.BlockSpec((n_heads, tq, AUG_DIM), lambda t, sl: (0, t, 0))]),
        compiler_params=_cparams(("arbitrary",)),
        name="moba_select",
    )(slopes, qv_t, kmean, kgg)


def _attn_kernel(qa_ref, ka_ref, vt_ref, o_ref, sa_ref, sb_ref, *, heads):
    i = pl.program_id(1)
    tq = MOBA_BLOCK
    unit_keys = ATTN_UNIT * MOBA_BLOCK
    neg_inf = jnp.float32(-jnp.inf)

    def scores_of(j, nkeys):
        off = pl.multiple_of(j * MOBA_BLOCK, MOBA_BLOCK)
        return [jnp.dot(ka_ref[hh, pl.ds(off, nkeys), :], qa_ref[hh],
                        preferred_element_type=F32) for hh in range(heads)]

    def scores_into(ref, unit):
        for hh, s_t in enumerate(scores_of(ATTN_UNIT * unit, unit_keys)):
            ref[hh] = s_t

    def update(j, scores, carries, diagonal, nkeys):
        off = pl.multiple_of(j * MOBA_BLOCK, MOBA_BLOCK)
        stats = []
        for hh in range(heads):
            m = carries[hh][0]
            s_t = scores[hh]
            if diagonal:
                key = lax.broadcasted_iota(I32, s_t.shape, 0) - (nkeys - MOBA_BLOCK)
                qry = lax.broadcasted_iota(I32, s_t.shape, 1)
                s_t = jnp.where(key <= qry, s_t, neg_inf)
            m_new = jnp.maximum(m, jnp.max(s_t, axis=0, keepdims=True))
            stats.append((m_new, jnp.exp(m - m_new), jnp.exp(s_t - m_new).astype(BF16)))
        out = []
        ones = jnp.ones((ONES_ROWS, nkeys), BF16)
        for hh in range(heads):
            m_new, alpha, p = stats[hh]
            vb = vt_ref[hh * HEAD_DIM:(hh + 1) * HEAD_DIM, pl.ds(off, nkeys)]
            vb1 = jnp.concatenate([vb, ones], axis=0)
            acc_new = alpha * carries[hh][1] + jnp.dot(vb1, p, preferred_element_type=F32)
            out.append((m_new, acc_new))
        return tuple(out)

    init = tuple((jnp.full((1, tq), neg_inf, F32), jnp.zeros((HEAD_DIM + ONES_ROWS, tq), F32))
                 for _ in range(heads))
    def tail_update(r):
        nkeys = (r + 1) * MOBA_BLOCK
        return lambda cs: update(i - r, scores_of(i - r, nkeys), cs, True, nkeys)

    carries = lax.switch(i % ATTN_UNIT, [tail_update(r) for r in range(ATTN_UNIT)], init)

    n_units = i // ATTN_UNIT
    last_unit = jnp.maximum(n_units - 1, 0)

    @pl.when(n_units > 0)
    def _():
        scores_into(sa_ref, 0)

    def from_ref(ref):
        return [ref[hh] for hh in range(heads)]

    def two_units(k, cs):
        scores_into(sb_ref, 2 * k + 1)
        cs = update(2 * ATTN_UNIT * k, from_ref(sa_ref), cs, False, unit_keys)
        scores_into(sa_ref, jnp.minimum(2 * k + 2, last_unit))
        return update(2 * ATTN_UNIT * k + ATTN_UNIT, from_ref(sb_ref), cs, False, unit_keys)

    carries = lax.fori_loop(0, n_units // 2, two_units, carries)
    carries = lax.cond(
        n_units % 2 == 1,
        lambda cs: update(ATTN_UNIT * last_unit, from_ref(sa_ref), cs, False, unit_keys),
        lambda cs: cs, carries)
    for hh in range(heads):
        acc = carries[hh][1]
        out_t = acc[0:HEAD_DIM, :] / acc[HEAD_DIM:HEAD_DIM + 1, :]
        o_ref[:, hh * HEAD_DIM:(hh + 1) * HEAD_DIM] = out_t.T.astype(o_ref.dtype)


def _attention(q_aug, k_aug, qv_t, n_heads, nb, heads):
    s = k_aug.shape[1]
    groups = n_heads // heads
    once = pl.Buffered(1)
    return pl.pallas_call(
        functools.partial(_attn_kernel, heads=heads),
        out_shape=jax.ShapeDtypeStruct((s, n_heads * HEAD_DIM), BF16),
        grid=(groups, nb),
        in_specs=[pl.BlockSpec((heads, AUG_DIM, MOBA_BLOCK), lambda hg, i: (hg, 0, i)),
                  pl.BlockSpec((heads, s, AUG_DIM), lambda hg, i: (hg, 0, 0), pipeline_mode=once),
                  pl.BlockSpec((heads * HEAD_DIM, s), lambda hg, i: (groups + hg, 0),
                               pipeline_mode=once)],
        out_specs=pl.BlockSpec((MOBA_BLOCK, heads * HEAD_DIM), lambda hg, i: (i, hg)),
        scratch_shapes=[pltpu.VMEM((heads, ATTN_UNIT * MOBA_BLOCK, MOBA_BLOCK), F32),
                        pltpu.VMEM((heads, ATTN_UNIT * MOBA_BLOCK, MOBA_BLOCK), F32)],
        compiler_params=_cparams(("arbitrary", "arbitrary")),
        name="moba_attention",
    )(q_aug, k_aug, qv_t)


def _qv_conv_kernel(wt_ref, x_ref, ga_ref, gb_ref, w_ref, b_ref, g_ref, beta_ref, qv_ref, o_ref,
                    u_ref, ush_ref, *, ts, sub):
    t = pl.program_id(0)

    @pl.when(t == 0)
    def _():
        u_ref[0:CONV_HALO, :] = jnp.zeros((CONV_HALO, u_ref.shape[1]), F32)

    @pl.when(t > 0)
    def _():
        u_ref[0:CONV_HALO, :] = u_ref[ts:ts + CONV_HALO, :]

    x_b = x_ref[...].astype(BF16)
    n_slabs = min(8, ts // sub)
    groups_per_slab = (ts // sub) // n_slabs
    slab = wt_ref.shape[0] // n_slabs

    def project(k, anchor):
        rows = slice(k * slab, (k + 1) * slab)
        x_k = x_b if anchor is None else x_b + anchor
        qv_ref[rows, :] = lax.dot_general(wt_ref[rows, :], x_k, (((1,), (1,)), ((), ())),
                                          preferred_element_type=F32).astype(qv_ref.dtype)

    gb = gb_ref[...].astype(F32)
    u_ref[CONV_HALO:CONV_HALO + ts, :] = ga_ref[...].astype(F32) * jax.nn.sigmoid(gb)
    for b in range(SUBLANES):
        ush_ref[b, 0:ts + CONV_HALO - b, :] = u_ref[b:ts + CONV_HALO, :]
    first = CONV_HALO - (CONV_WIDTH - 1)
    anchor = None
    for r0 in range(0, ts, sub):
        if (r0 // sub) % groups_per_slab == 0:
            project(r0 // sub // groups_per_slab, anchor)
        acc = jnp.broadcast_to(b_ref[...], (sub, u_ref.shape[1]))
        for tap in range(CONV_WIDTH):
            off = first + tap
            aligned = r0 + off - off % SUBLANES
            acc = acc + w_ref[tap:tap + 1, :] * ush_ref[off % SUBLANES, aligned:aligned + sub, :]
        mu = jnp.mean(acc, axis=-1, keepdims=True)
        cen = acc - mu
        var = jnp.mean(cen * cen, axis=-1, keepdims=True)
        y = cen * lax.rsqrt(var + LN_EPS) * g_ref[...] + beta_ref[...]
        o_ref[r0:r0 + sub, :] = (y * jax.nn.sigmoid(y)).astype(o_ref.dtype)
        bits = pltpu.bitcast(y[0:1, 0:1], I32)
        zero = lax.shift_right_logical(lax.shift_right_logical(bits, jnp.int32(31)), jnp.int32(1))
        anchor = zero.astype(BF16)


def _transpose_cast_kernel(w_ref, o_ref):
    o_ref[...] = w_ref[...].T.astype(o_ref.dtype)


def _qv_weights_transposed(w_in, g):
    d = w_in.shape[0]
    cols = min(256, g)
    per_group = g // cols
    return pl.pallas_call(
        _transpose_cast_kernel,
        out_shape=jax.ShapeDtypeStruct((2 * g, d), BF16),
        grid=(2 * per_group,),
        in_specs=[pl.BlockSpec((d, cols), lambda c: (0, c + jnp.where(c >= per_group, per_group, 0)))],
        out_specs=pl.BlockSpec((cols, d), lambda c: (c, 0)),
        compiler_params=_cparams(("arbitrary",)),
        name="qv_weight_transpose",
    )(w_in)


def _qv_projection_and_conv(x2, w_in, kgg, conv_w, conv_b, conv_ln_g, conv_ln_b, g, ts):
    s, d = x2.shape
    row = lambda v: v.reshape(1, g).astype(F32)
    w_qv_t = _qv_weights_transposed(w_in, g)
    once = pl.Buffered(1)
    return pl.pallas_call(
        functools.partial(_qv_conv_kernel, ts=ts, sub=32),
        out_shape=(jax.ShapeDtypeStruct((2 * g, s), BF16), jax.ShapeDtypeStruct((s, g), BF16)),
        grid=(s // ts,),
        in_specs=[pl.BlockSpec((2 * g, d), lambda t: (0, 0), pipeline_mode=once),
                  pl.BlockSpec((ts, d), lambda t: (t, 0)),
                  pl.BlockSpec((ts, g), lambda t: (t, 1)),
                  pl.BlockSpec((ts, g), lambda t: (t, 2)),
                  pl.BlockSpec((CONV_WIDTH, g), lambda t: (0, 0)),
                  pl.BlockSpec((1, g), lambda t: (0, 0)),
                  pl.BlockSpec((1, g), lambda t: (0, 0)),
                  pl.BlockSpec((1, g), lambda t: (0, 0))],
        out_specs=[pl.BlockSpec((2 * g, ts), lambda t: (0, t)),
                   pl.BlockSpec((ts, g), lambda t: (t, 0))],
        scratch_shapes=[pltpu.VMEM((ts + CONV_HALO, g), F32),
                        pltpu.VMEM((SUBLANES, ts + CONV_HALO, g), F32)],
        compiler_params=_cparams(("arbitrary",)),
        name="qv_proj_conformer_conv",
    )(w_qv_t, x2, kgg, kgg, conv_w, row(conv_b), row(conv_ln_g), row(conv_ln_b))


def _layer_norm_rows(z, gain, bias):
    mu = jnp.mean(z, axis=-1, keepdims=True)
    cen = z - mu
    var = jnp.mean(cen * cen, axis=-1, keepdims=True)
    return cen * lax.rsqrt(var + LN_EPS) * gain + bias


HI_HALF = -65536


def _pack_halves(lo, hi):
    lo_bits = pltpu.bitcast(lo.astype(BF16).astype(F32), I32)
    hi_bits = pltpu.bitcast(hi.astype(BF16).astype(F32), I32)
    return lax.shift_right_logical(lo_bits, jnp.int32(16)) | (hi_bits & jnp.int32(HI_HALF))


def _unpack_halves(words):
    lo = pltpu.bitcast(lax.shift_left(words, jnp.int32(16)), F32)
    hi = pltpu.bitcast(words & jnp.int32(HI_HALF), F32)
    return lo, hi


def _outproj_kernel(attn_ref, conv_ref, wo_ref, x_ref, g_ref, b_ref, wr_ref, br_ref,
                    h_ref, hp_ref, lg_ref, *, g):
    tm = x_ref.shape[0]
    sub = min(tm, 256)
    half = x_ref.shape[1] // 2
    n_exp = lg_ref.shape[0]
    wr = wr_ref[...]
    w_hi = wr.astype(BF16)
    w_lo = (wr - w_hi.astype(F32)).astype(BF16)
    mixes = []
    for r0 in range(0, tm, sub):
        mix = jnp.dot(attn_ref[r0:r0 + sub, :], wo_ref[0:g, :], preferred_element_type=F32)
        mixes.append(mix + jnp.dot(conv_ref[r0:r0 + sub, :], wo_ref[g:2 * g, :],
                                   preferred_element_type=F32))
    for k, r0 in enumerate(range(0, tm, sub)):
        h1 = _layer_norm_rows(DEEPNORM_ALPHA * x_ref[r0:r0 + sub, :] + mixes[k],
                              g_ref[...], b_ref[...])
        h_ref[r0:r0 + sub, :] = h1
        hp_ref[r0:r0 + sub, :] = _pack_halves(h1[:, 0:half], h1[:, half:])
        h_hi = h1.astype(BF16)
        h_lo = (h1 - h_hi.astype(F32)).astype(BF16)
        lg = jnp.dot(h_hi, w_hi, preferred_element_type=F32)
        lg = lg + jnp.dot(h_lo, w_hi, preferred_element_type=F32)
        lg = lg + jnp.dot(h_hi, w_lo, preferred_element_type=F32)
        lg_ref[:, r0:r0 + sub] = lg.T[0:n_exp, :] + br_ref[...]


def _out_projection(attn, conv, w_out, x2, ln_g, ln_b, w_router, b_router, tm):
    s, d = x2.shape
    g = attn.shape[1]
    e = w_router.shape[1]
    return pl.pallas_call(
        functools.partial(_outproj_kernel, g=g),
        out_shape=(jax.ShapeDtypeStruct((s, d), F32), jax.ShapeDtypeStruct((s, d // 2), I32),
                   jax.ShapeDtypeStruct((e, s), F32)),
        grid=(s // tm,),
        in_specs=[pl.BlockSpec((tm, g), lambda m: (m, 0)),
                  pl.BlockSpec((tm, g), lambda m: (m, 0)),
                  pl.BlockSpec((2 * g, d), lambda m: (0, 0), pipeline_mode=pl.Buffered(1)),
                  pl.BlockSpec((tm, d), lambda m: (m, 0)),
                  pl.BlockSpec((1, d), lambda m: (0, 0)),
                  pl.BlockSpec((1, d), lambda m: (0, 0)),
                  pl.BlockSpec((d, LANES), lambda m: (0, 0)),
                  pl.BlockSpec((e, 1), lambda m: (0, 0))],
        out_specs=[pl.BlockSpec((tm, d), lambda m: (m, 0)),
                   pl.BlockSpec((tm, d // 2), lambda m: (m, 0)),
                   pl.BlockSpec((e, tm), lambda m: (0, m))],
        compiler_params=_cparams(("arbitrary",)),
        name="outproj_ln1_router",
    )(attn, conv, w_out.astype(BF16), x2, ln_g.reshape(1, d), ln_b.reshape(1, d),
      jnp.pad(w_router, ((0, 0), (0, LANES - e))), b_router.reshape(e, 1))


def _route_kernel(lg_ref, e_ref, gate_ref, rank_ref, cnt_ref, carry_ref, *, n_exp, tr):
    t = pl.program_id(0)

    @pl.when(t == 0)
    def _():
        carry_ref[...] = jnp.zeros_like(carry_ref)

    neg_inf = jnp.float32(-jnp.inf)
    cur = lg_ref[...]
    j = lax.broadcasted_iota(I32, (n_exp, tr), 0)
    vals, picks = [], []
    for r in range(TOP_K):
        m = jnp.max(cur, axis=0, keepdims=True)
        idx = jnp.min(jnp.where(cur == m, j, n_exp), axis=0, keepdims=True)
        pick = j == idx
        e_ref[r:r + 1, :] = idx
        vals.append(m)
        picks.append(pick)
        cur = jnp.where(pick, neg_inf, cur)
    exps = [jnp.exp(v - vals[0]) for v in vals]
    den = exps[0]
    for r in range(1, TOP_K):
        den = den + exps[r]
    for r in range(TOP_K):
        gate_ref[r:r + 1, :] = exps[r] / den

    chosen = picks[0].astype(F32)
    for r in range(1, TOP_K):
        chosen = chosen + picks[r].astype(F32)
    a = lax.broadcasted_iota(I32, (tr, tr), 0)
    b = lax.broadcasted_iota(I32, (tr, tr), 1)
    upper = (a < b).astype(BF16)
    excl = jnp.dot(chosen.astype(BF16), upper, preferred_element_type=F32)
    base = carry_ref[:, 0:1]
    rank = excl + base
    for r in range(TOP_K):
        rank_ref[r:r + 1, :] = jnp.sum(jnp.where(picks[r], rank, 0.0), axis=0,
                                       keepdims=True).astype(I32)
    total = base + jnp.sum(chosen, axis=1, keepdims=True)
    carry_ref[...] = jnp.broadcast_to(total, carry_ref.shape)
    cnt_ref[...] = jnp.broadcast_to(total, cnt_ref.shape).astype(I32)


def _route(logits_t, tr):
    n_exp, s = logits_t.shape
    return pl.pallas_call(
        functools.partial(_route_kernel, n_exp=n_exp, tr=tr),
        out_shape=(jax.ShapeDtypeStruct((TOP_K, s), I32), jax.ShapeDtypeStruct((TOP_K, s), F32),
                   jax.ShapeDtypeStruct((TOP_K, s), I32), jax.ShapeDtypeStruct((n_exp, 128), I32)),
        grid=(s // tr,),
        in_specs=[pl.BlockSpec((n_exp, tr), lambda t: (0, t))],
        out_specs=[pl.BlockSpec((TOP_K, tr), lambda t: (0, t)),
                   pl.BlockSpec((TOP_K, tr), lambda t: (0, t)),
                   pl.BlockSpec((TOP_K, tr), lambda t: (0, t)),
                   pl.BlockSpec((n_exp, 128), lambda t: (0, 0))],
        scratch_shapes=[pltpu.VMEM((n_exp, 128), F32)],
        compiler_params=_cparams(("arbitrary",)),
        name="moe_route",
    )(logits_t)


SC_CORES = 2
SC_SUBCORES = 16
SC_LANES = 16
SC_GATHER_ROWS = 32
SC_SCAN = 8192


_SC_PARAMS = pltpu.CompilerParams(needs_layout_passes=False)


def _sc_mesh():
    return plsc.VectorSubcoreMesh(core_axis_name="c", subcore_axis_name="s",
                                  num_cores=SC_CORES, num_subcores=SC_SUBCORES)


def _sc_worker():
    return lax.axis_index("s") * SC_CORES + lax.axis_index("c")


def _sc_gather_loop(table_hbm, idx_all, out_hbm, bufs, sems, base, n_chunks):
    chunk = SC_GATHER_ROWS

    def gather(ci, b):
        return pltpu.make_async_copy(
            table_hbm.at[idx_all.at[pl.ds(ci * chunk, chunk)]], bufs[b], sems[b])

    gather(0, 0).start()

    @pl.loop(0, n_chunks, step=2)
    def _(ci):
        gather(ci + 1, 1).start()
        gather(ci, 0).wait()
        pltpu.sync_copy(bufs[0], out_hbm.at[pl.ds(base + ci * chunk, chunk)])

        @pl.when(ci + 2 < n_chunks)
        def _():
            gather(ci + 2, 0).start()

        gather(ci + 1, 1).wait()
        pltpu.sync_copy(bufs[1], out_hbm.at[pl.ds(base + (ci + 1) * chunk, chunk)])


def _sc_dest(ps_v, e_buf, r_buf, v):
    e_vec = e_buf[pl.ds(v * SC_LANES, SC_LANES)]
    return plsc.load_gather(ps_v, [e_vec]) + r_buf[pl.ds(v * SC_LANES, SC_LANES)]


def _sc_dispatch(table, e_flat, rank_flat, pad_start, cap, s):
    n_assign, width = e_flat.shape[0], table.shape[1]
    n_workers = SC_CORES * SC_SUBCORES
    per_worker = cap // n_workers
    chunk = SC_GATHER_ROWS
    assert cap % (n_workers * 2 * chunk) == 0 and n_assign % SC_SCAN == 0
    assert per_worker % SC_LANES == 0 and s % SC_SCAN == 0

    @functools.partial(
        pl.kernel, mesh=_sc_mesh(), out_type=jax.ShapeDtypeStruct((cap, width), table.dtype),
        scratch_types=[pltpu.VMEM((per_worker,), I32), pltpu.VMEM((pad_start.shape[0],), I32),
                       pltpu.VMEM((SC_SCAN,), I32), pltpu.VMEM((SC_SCAN,), I32),
                       pltpu.VMEM((chunk, width), table.dtype),
                       pltpu.VMEM((chunk, width), table.dtype),
                       pltpu.SemaphoreType.DMA, pltpu.SemaphoreType.DMA],
        compiler_params=_SC_PARAMS, name="sc_dispatch")
    def dispatch(table_hbm, e_hbm, r_hbm, ps_hbm, out_hbm, idx_all, ps_v, e_buf, r_buf,
                 buf0, buf1, sem0, sem1):
        base = _sc_worker() * per_worker
        lane = lax.iota(I32, SC_LANES)
        pltpu.sync_copy(ps_hbm, ps_v)

        @pl.loop(0, per_worker // SC_LANES)
        def _(k):
            idx_all[pl.ds(k * SC_LANES, SC_LANES)] = lax.rem(base + k * SC_LANES + lane, s)

        @pl.loop(0, n_assign // SC_SCAN)
        def _(c):
            pltpu.sync_copy(e_hbm.at[pl.ds(c * SC_SCAN, SC_SCAN)], e_buf)
            pltpu.sync_copy(r_hbm.at[pl.ds(c * SC_SCAN, SC_SCAN)], r_buf)
            tok0 = lax.rem(c * SC_SCAN, s)

            @pl.loop(0, SC_SCAN // SC_LANES)
            def _(v):
                loc = _sc_dest(ps_v, e_buf, r_buf, v) - base
                mine = jnp.logical_and(loc >= 0, loc < per_worker)
                tok = tok0 + v * SC_LANES + lane
                plsc.store_scatter(idx_all, [jnp.where(mine, loc, 0)], tok, mask=mine)

        _sc_gather_loop(table_hbm, idx_all, out_hbm, (buf0, buf1), (sem0, sem1), base,
                        per_worker // chunk)

    return dispatch(table, e_flat, rank_flat, pad_start)


def _sc_combine_gather(y, e_flat, rank_flat, pad_start):
    n_assign, width = e_flat.shape[0], y.shape[1]
    n_workers = SC_CORES * SC_SUBCORES
    per_worker = n_assign // n_workers
    chunk = SC_GATHER_ROWS
    assert n_assign % (n_workers * 2 * chunk) == 0

    @functools.partial(
        pl.kernel, mesh=_sc_mesh(), out_type=jax.ShapeDtypeStruct((n_assign, width), y.dtype),
        scratch_types=[pltpu.VMEM((per_worker,), I32), pltpu.VMEM((pad_start.shape[0],), I32),
                       pltpu.VMEM((per_worker,), I32), pltpu.VMEM((per_worker,), I32),
                       pltpu.VMEM((chunk, width), y.dtype), pltpu.VMEM((chunk, width), y.dtype),
                       pltpu.SemaphoreType.DMA, pltpu.SemaphoreType.DMA],
        compiler_params=_SC_PARAMS, name="sc_combine_gather")
    def combine(y_hbm, e_hbm, r_hbm, ps_hbm, out_hbm, idx_all, ps_v, e_buf, r_buf,
                buf0, buf1, sem0, sem1):
        base = _sc_worker() * per_worker
        pltpu.sync_copy(ps_hbm, ps_v)
        pltpu.sync_copy(e_hbm.at[pl.ds(base, per_worker)], e_buf)
        pltpu.sync_copy(r_hbm.at[pl.ds(base, per_worker)], r_buf)

        @pl.loop(0, per_worker // SC_LANES)
        def _(v):
            idx_all[pl.ds(v * SC_LANES, SC_LANES)] = _sc_dest(ps_v, e_buf, r_buf, v)

        _sc_gather_loop(y_hbm, idx_all, out_hbm, (buf0, buf1), (sem0, sem1), base,
                        per_worker // chunk)

    return combine(y, e_flat, rank_flat, pad_start)


def _expert_kernel(item_e, item_start, item_rows, xs_ref, wg_ref, wu_ref, wd_ref,
                   bg_ref, bu_ref, bd_ref, y_ref, xin_ref, acc_ref,
                   sem_x, sem_y, *, nf, n_items, big):
    w = pl.program_id(0)
    f = pl.program_id(1)
    rows = item_rows[w]
    slot = w % 2
    half = xin_ref.shape[2]
    has_next = jnp.logical_and(w + 1 < n_items, item_rows[jnp.minimum(w + 1, n_items - 1)] > 0)

    def x_copy(item, sl, c):
        src = pl.multiple_of(item_start[item] + c * ROUTE_PAD, ROUTE_PAD)
        dst = pl.multiple_of(c * ROUTE_PAD, ROUTE_PAD)
        return pltpu.make_async_copy(xs_ref.at[pl.ds(src, ROUTE_PAD)],
                                     xin_ref.at[sl, pl.ds(dst, ROUTE_PAD)], sem_x.at[sl])

    def y_copy(item, sl, c):
        src = pl.multiple_of(c * ROUTE_PAD, ROUTE_PAD)
        dst = pl.multiple_of(item_start[item] + c * ROUTE_PAD, ROUTE_PAD)
        return pltpu.make_async_copy(xin_ref.at[sl, pl.ds(src, ROUTE_PAD)],
                                     y_ref.at[pl.ds(dst, ROUTE_PAD)], sem_y)

    def for_chunks(item, fn):
        def body(c, carry):
            fn(c)
            return carry
        lax.fori_loop(0, item_rows[item] // ROUTE_PAD, body, 0)

    def compute(r0, size, first):
        lo, hi = _unpack_halves(xin_ref[slot, pl.ds(r0, size), :])
        lo = lo.astype(BF16)
        hi = hi.astype(BF16)

        def x_dot(w_ref):
            return (jnp.dot(lo, w_ref[0, 0:half, :].astype(BF16), preferred_element_type=F32)
                    + jnp.dot(hi, w_ref[0, half:2 * half, :].astype(BF16),
                              preferred_element_type=F32))

        gate = x_dot(wg_ref) + bg_ref[0]
        up = x_dot(wu_ref) + bu_ref[0]
        gate = jnp.minimum(gate, SWIGLU_LIMIT)
        up = jnp.clip(up, -SWIGLU_LIMIT, SWIGLU_LIMIT)
        act = (up + 1.0) * (gate * jax.nn.sigmoid(SWIGLU_ALPHA * gate))
        part = jnp.dot(act.astype(BF16), wd_ref[0].astype(BF16), preferred_element_type=F32)
        if first:
            acc_ref[pl.ds(r0, size), :] = part + bd_ref[0]
        else:
            acc_ref[pl.ds(r0, size), :] += part

    def compute_all(first):
        n_big = rows // big

        def body(c, carry):
            compute(pl.multiple_of(c * big, big), big, first)
            return carry
        lax.fori_loop(0, n_big, body, 0)
        done = n_big * big
        size = big // 2
        while size >= ROUTE_PAD:
            take = ((rows - done) & size) != 0
            here = done

            @pl.when(take)
            def _(here=here, size=size):
                compute(pl.multiple_of(here, ROUTE_PAD), size, first)
            done = done + jnp.where(take, size, 0)
            size //= 2

    @pl.when(rows > 0)
    def _():
        @pl.when(f == 0)
        def _():
            @pl.when(w == 0)
            def _():
                for_chunks(w, lambda c: x_copy(w, slot, c).start())

            for_chunks(w, lambda c: x_copy(w, slot, c).wait())

            @pl.when(w > 0)
            def _():
                for_chunks(w - 1, lambda c: y_copy(w - 1, 1 - slot, c).wait())

            @pl.when(has_next)
            def _():
                for_chunks(w + 1, lambda c: x_copy(w + 1, 1 - slot, c).start())

            compute_all(True)

        @pl.when(f > 0)
        def _():
            compute_all(False)

        @pl.when(f == nf - 1)
        def _():
            def pack(c):
                r0 = pl.multiple_of(c * ROUTE_PAD, ROUTE_PAD)
                xin_ref[slot, pl.ds(r0, ROUTE_PAD), :] = _pack_halves(
                    acc_ref[pl.ds(r0, ROUTE_PAD), 0:half],
                    acc_ref[pl.ds(r0, ROUTE_PAD), half:2 * half])
                y_copy(w, slot, c).start()
            for_chunks(w, pack)

            @pl.when(jnp.logical_not(has_next))
            def _():
                for_chunks(w, lambda c: y_copy(w, slot, c).wait())


def _experts(item_e, item_start, item_rows, n_active, xs, w_gate_up, b_gate_up, w_down, b_down,
             tm, tf):
    cap, half = xs.shape
    d = 2 * half
    n_exp, _, two_f = w_gate_up.shape
    ff = two_f // 2
    nf = ff // tf
    n_items = item_e.shape[0]
    big = min(512, tm)
    assert big % ROUTE_PAD == 0 and (big & (big - 1)) == 0

    def fidx(w, f, rows):
        return jnp.where(rows[w] > 0, f, nf - 1)

    return pl.pallas_call(
        functools.partial(_expert_kernel, nf=nf, n_items=n_items, big=big),
        out_shape=jax.ShapeDtypeStruct((cap, half), I32),
        grid_spec=pltpu.PrefetchScalarGridSpec(
            num_scalar_prefetch=3,
            grid=(n_active, nf),
            in_specs=[
                pl.BlockSpec(memory_space=pl.ANY),
                pl.BlockSpec((1, d, tf), lambda w, f, ie, ist, ir: (ie[w], 0, fidx(w, f, ir))),
                pl.BlockSpec((1, d, tf), lambda w, f, ie, ist, ir: (ie[w], 0, nf + fidx(w, f, ir))),
                pl.BlockSpec((1, tf, d), lambda w, f, ie, ist, ir: (ie[w], fidx(w, f, ir), 0)),
                pl.BlockSpec((1, 1, tf), lambda w, f, ie, ist, ir: (ie[w], 0, fidx(w, f, ir))),
                pl.BlockSpec((1, 1, tf), lambda w, f, ie, ist, ir: (ie[w], 0, nf + fidx(w, f, ir))),
                pl.BlockSpec((1, 1, d), lambda w, f, ie, ist, ir: (ie[w], 0, 0)),
            ],
            out_specs=pl.BlockSpec(memory_space=pl.ANY),
            scratch_shapes=[pltpu.VMEM((2, tm, half), I32), pltpu.VMEM((tm, d), F32),
                            pltpu.SemaphoreType.DMA((2,)), pltpu.SemaphoreType.DMA]),
        compiler_params=_cparams(("arbitrary", "arbitrary")),
        name="moe_experts",
    )(item_e, item_start, item_rows, xs, w_gate_up, w_gate_up, w_down,
      b_gate_up.reshape(n_exp, 1, two_f), b_gate_up.reshape(n_exp, 1, two_f),
      b_down.reshape(n_exp, 1, d))


def _combine_kernel(yg_ref, gate_ref, h_ref, g_ref, b_ref, o_ref):
    gates = gate_ref[...]
    half = yg_ref.shape[2]
    z_lo = DEEPNORM_ALPHA * h_ref[:, 0:half]
    z_hi = DEEPNORM_ALPHA * h_ref[:, half:2 * half]
    for r in range(TOP_K):
        lo, hi = _unpack_halves(yg_ref[r])
        z_lo = z_lo + gates[:, r:r + 1] * lo
        z_hi = z_hi + gates[:, r:r + 1] * hi
    inv_d = 1.0 / (2 * half)
    mu = (jnp.sum(z_lo, axis=-1, keepdims=True) + jnp.sum(z_hi, axis=-1, keepdims=True)) * inv_d
    c_lo = z_lo - mu
    c_hi = z_hi - mu
    var = (jnp.sum(c_lo * c_lo, axis=-1, keepdims=True)
           + jnp.sum(c_hi * c_hi, axis=-1, keepdims=True)) * inv_d
    rstd = lax.rsqrt(var + LN_EPS)
    o_ref[:, 0:half] = c_lo * rstd * g_ref[:, 0:half] + b_ref[:, 0:half]
    o_ref[:, half:2 * half] = c_hi * rstd * g_ref[:, half:2 * half] + b_ref[:, half:2 * half]


def _combine_part_kernel(yg_ref, gate_ref, h_ref, g_ref, b_ref, prev_ref, o_ref):
    del prev_ref
    _combine_kernel(yg_ref, gate_ref, h_ref, g_ref, b_ref, o_ref)


def _combine(yg, gates, h1, ln_g, ln_b, tc, first_tile, out_prev):
    s, d = h1.shape
    n_tiles = yg.shape[1] // tc
    in_specs = [pl.BlockSpec((TOP_K, tc, d // 2), lambda t: (0, t, 0)),
                pl.BlockSpec((tc, TOP_K), lambda t: (t + first_tile, 0)),
                pl.BlockSpec((tc, d), lambda t: (t + first_tile, 0)),
                pl.BlockSpec((1, d), lambda t: (0, 0)),
                pl.BlockSpec((1, d), lambda t: (0, 0))]
    args = [yg, gates, h1, ln_g.reshape(1, d), ln_b.reshape(1, d)]
    if out_prev is None:
        body, aliases = _combine_kernel, {}
    else:
        body, aliases = _combine_part_kernel, {len(args): 0}
        in_specs.append(pl.BlockSpec(memory_space=pl.ANY))
        args.append(out_prev)
    return pl.pallas_call(
        body,
        out_shape=jax.ShapeDtypeStruct((s, d), F32),
        grid=(n_tiles,),
        in_specs=in_specs,
        out_specs=pl.BlockSpec((tc, d), lambda t: (t + first_tile, 0)),
        input_output_aliases=aliases,
        compiler_params=_cparams(("arbitrary",)),
        name="moe_combine_ln2",
    )(*args)


def _tiles(s, d, ff, n_heads):
    return dict(
        proj_tm=min(512, s), select_tq=min(512, s), conv_ts=min(512, s), outproj_tm=min(512, s),
        attn_heads=min(4, n_heads),
        route_tr=min(256, s), combine_tc=min(256, s),
        expert_tm=min(1280, max(ROUTE_PAD, (s * TOP_K // 16) // ROUTE_PAD * ROUTE_PAD)),
        expert_tf=min(512, ff))


def kernel(x, w_in, conv_w, conv_b, conv_ln_g, conv_ln_b, w_out, ln1_g, ln1_b,
           w_router, b_router, w_gate_up, b_gate_up, w_down, b_down, ln2_g, ln2_b):
    b, s, d = x.shape
    assert b == 1, "the kernels treat the sequence of the single batch element as the row axis"
    g = w_in.shape[1] // 5
    n_heads = g // HEAD_DIM
    nb = s // MOBA_BLOCK
    n_exp = w_router.shape[1]
    ff = w_down.shape[1]
    assert s % MOBA_BLOCK == 0 and nb % 8 == 0 and HEAD_DIM + nb + 3 <= AUG_DIM
    tl = _tiles(s, d, ff, n_heads)
    x2 = x.reshape(s, d)

    kgg = _in_projection(x2, w_in, g, tl["proj_tm"])
    qv_t, conv = _qv_projection_and_conv(x2, w_in, kgg, conv_w, conv_b, conv_ln_g, conv_ln_b, g,
                                         tl["conv_ts"])
    kmean = _block_means(kgg, g, nb)
    slopes = 2.0 ** (-(8.0 / n_heads) * jnp.arange(1, n_heads + 1, dtype=F32))
    q_aug, k_aug = _select(slopes, qv_t, kmean, kgg, n_heads, nb, tl["select_tq"])
    attn = _attention(q_aug, k_aug, qv_t, n_heads, nb, tl["attn_heads"])
    h1, h1_packed, logits_t = _out_projection(attn, conv, w_out, x2, ln1_g, ln1_b,
                                              w_router, b_router, tl["outproj_tm"])

    e_t, gate_t, rank_t, counts = _route(logits_t, tl["route_tr"])
    counts = counts[:, 0]
    padded = (counts + ROUTE_PAD - 1) // ROUTE_PAD * ROUTE_PAD
    pad_end = jnp.cumsum(padded)
    pad_start = (pad_end - padded).astype(I32)
    cap = s * TOP_K + n_exp * ROUTE_PAD
    e_flat = e_t.reshape(-1)
    rank_flat = rank_t.reshape(-1)
    xs = _sc_dispatch(h1_packed, e_flat, rank_flat, pad_start, cap, s)

    tm = tl["expert_tm"]
    n_items = cap // tm + n_exp
    per_e = (padded + tm - 1) // tm
    item_end = jnp.cumsum(per_e)
    item_ids = jnp.arange(n_items, dtype=I32)
    item_e = jnp.minimum(jnp.sum(item_ids[:, None] >= item_end[None, :], axis=1),
                         n_exp - 1).astype(I32)
    piece = item_ids - (item_end - per_e)[item_e]
    active = item_ids < item_end[-1]
    item_start = jnp.where(active, pad_start[item_e] + piece * tm, 0).astype(I32)
    item_rows = jnp.where(active, jnp.minimum(tm, padded[item_e] - piece * tm), 0).astype(I32)
    last_e = item_e[jnp.maximum(item_end[-1] - 1, 0)]
    item_e = jnp.where(active, item_e, last_e).astype(I32)

    y = _experts(item_e, item_start, item_rows, item_end[-1].astype(I32), xs,
                 w_gate_up, b_gate_up, w_down, b_down, tm, tl["expert_tf"])
    tc = tl["combine_tc"]
    s_part = s // COMBINE_PARTS
    gates = gate_t.T
    out = None
    for part in range(COMBINE_PARTS):
        cols = slice(part * s_part, (part + 1) * s_part)
        yg = _sc_combine_gather(y, e_t[:, cols].reshape(-1), rank_t[:, cols].reshape(-1), pad_start)
        out = _combine(yg.reshape(TOP_K, s_part, d // 2), gates, h1, ln2_g, ln2_b, tc,
                       part * (s_part // tc), out)
    return out.reshape(b, s, d)
```

```python
import functools

import jax
import jax.numpy as jnp
from jax import lax
from jax.experimental import pallas as pl
from jax.experimental.pallas import tpu as pltpu
from jax.experimental.pallas import tpu_sc as plsc

F32 = jnp.float32
BF16 = jnp.bfloat16
I32 = jnp.int32

HEAD_DIM = 128
MOBA_BLOCK = 256
MOBA_TOPK = 3
CONV_WIDTH = 31
CONV_HALO = 32
SUBLANES = 8
LANES = 128
TOP_K = 4
SWIGLU_ALPHA = 1.702
SWIGLU_LIMIT = 7.0
LN_EPS = 1e-5
DEPTH = 1
DEEPNORM_ALPHA = (2.0 * DEPTH) ** 0.25
ROUTE_PAD = 128
AUG_DIM = 256
MASK_NEG = -1e30
ONES_ROWS = 16
ATTN_UNIT = 4
COMBINE_PARTS = 2
VMEM_LIMIT = 56 * 1024 * 1024


def _cparams(sem):
    return pltpu.CompilerParams(dimension_semantics=sem, vmem_limit_bytes=VMEM_LIMIT)


def _proj_nn_kernel(x_ref, w_ref, o_ref, wb_ref):
    @pl.when(pl.program_id(1) == 0)
    def _():
        wb_ref[...] = w_ref[...].astype(BF16)

    o_ref[...] = jnp.dot(x_ref[...].astype(BF16), wb_ref[...],
                         preferred_element_type=F32).astype(o_ref.dtype)


def _in_projection(x2, w_in, g, tm):
    s, d = x2.shape
    return pl.pallas_call(
        _proj_nn_kernel,
        out_shape=jax.ShapeDtypeStruct((s, 3 * g), BF16),
        grid=(3, s // tm),
        in_specs=[pl.BlockSpec((tm, d), lambda n, m: (m, 0)),
                  pl.BlockSpec((d, g), lambda n, m: (0, jnp.where(n == 0, 1, n + 2)))],
        out_specs=pl.BlockSpec((tm, g), lambda n, m: (m, n)),
        scratch_shapes=[pltpu.VMEM((d, g), BF16)],
        compiler_params=_cparams(("arbitrary", "arbitrary")),
        name="proj_nn",
    )(x2, w_in)


def _kmean_kernel(k_ref, o_ref, *, blocks):
    for b in range(blocks):
        kb = k_ref[b * MOBA_BLOCK:(b + 1) * MOBA_BLOCK, :].astype(F32)
        o_ref[b:b + 1, :] = jnp.sum(kb, axis=0, keepdims=True) * (1.0 / MOBA_BLOCK)


def _block_means(kgg, g, nb):
    blocks = 8
    return pl.pallas_call(
        functools.partial(_kmean_kernel, blocks=blocks),
        out_shape=jax.ShapeDtypeStruct((nb, g), F32),
        grid=(nb // blocks,),
        in_specs=[pl.BlockSpec((blocks * MOBA_BLOCK, g), lambda i: (i, 0))],
        out_specs=pl.BlockSpec((blocks, g), lambda i: (i, 0)),
        compiler_params=_cparams(("arbitrary",)),
        name="moba_kmean",
    )(kgg)


def _select_kernel(slopes_ref, qt_ref, km_ref, k_ref, qa_ref, ka_ref, *, nb, tq, n_heads):
    t = pl.program_id(0)
    col = t * tq + lax.broadcasted_iota(I32, (nb, tq), 1)
    qblk = col // MOBA_BLOCK
    j = lax.broadcasted_iota(I32, (nb, tq), 0)
    neg_inf = jnp.float32(-jnp.inf)
    past = j < qblk
    own = j == qblk
    n_extra = AUG_DIM - HEAD_DIM - nb
    r = lax.broadcasted_iota(I32, (n_extra, tq), 0)
    qb = ((t * tq + lax.broadcasted_iota(I32, (n_extra, tq), 1)) // MOBA_BLOCK).astype(F32)
    extra_unit = jnp.where(r == 0, 1.0,
                           jnp.where(r == 1, float(MOBA_BLOCK),
                                     jnp.where(r == 2, -float(MOBA_BLOCK) * qb, 0.0)))
    n_aug = AUG_DIM - HEAD_DIM
    pos = t * tq + lax.broadcasted_iota(I32, (tq, n_aug), 0)
    kblk = pos // MOBA_BLOCK
    pib = pos % MOBA_BLOCK
    lane = lax.broadcasted_iota(I32, (tq, n_aug), 1)
    aug = jnp.where(lane < nb, (lane == kblk).astype(F32),
                    jnp.where(lane == nb, pib.astype(F32),
                              jnp.where(lane == nb + 1, kblk.astype(F32),
                                        jnp.where(lane == nb + 2, 1.0, 0.0)))).astype(BF16)
    scale = HEAD_DIM ** -0.5
    for h in range(n_heads):
        cols = slice(h * HEAD_DIM, (h + 1) * HEAD_DIM)
        q_t = qt_ref[cols, :].astype(F32)
        gate = jnp.dot(km_ref[:, cols], q_t, preferred_element_type=F32,
                       precision=lax.Precision.HIGHEST)
        gcur = jnp.where(past, gate, neg_inf)
        sel = own
        for _ in range(MOBA_TOPK):
            m = jnp.max(gcur, axis=0, keepdims=True)
            is_max = jnp.logical_and(gcur == m, m > neg_inf)
            idx = jnp.min(jnp.where(is_max, j, nb), axis=0, keepdims=True)
            pick = j == idx
            sel = jnp.logical_or(sel, pick)
            gcur = jnp.where(pick, neg_inf, gcur)
        qa_ref[h, 0:HEAD_DIM, :] = (q_t * scale).astype(BF16)
        qa_ref[h, HEAD_DIM:HEAD_DIM + nb, :] = jnp.where(sel, 0.0, MASK_NEG).astype(BF16)
        qa_ref[h, HEAD_DIM + nb:AUG_DIM, :] = (slopes_ref[h] * extra_unit).astype(BF16)
        ka_ref[h, :, 0:HEAD_DIM] = k_ref[:, cols]
        ka_ref[h, :, HEAD_DIM:AUG_DIM] = aug


def _select(slopes, qv_t, kmean, kgg, n_heads, nb, tq):
    s = kgg.shape[0]
    g = n_heads * HEAD_DIM
    return pl.pallas_call(
        functools.partial(_select_kernel, nb=nb, tq=tq, n_heads=n_heads),
        out_shape=(jax.ShapeDtypeStruct((n_heads, AUG_DIM, s), BF16),
                   jax.ShapeDtypeStruct((n_heads, s, AUG_DIM), BF16)),
        grid_spec=pltpu.PrefetchScalarGridSpec(
            num_scalar_prefetch=1,
            grid=(s // tq,),
            in_specs=[pl.BlockSpec((g, tq), lambda t, sl: (0, t)),
                      pl.BlockSpec((nb, g), lambda t, sl: (0, 0)),
                      pl.BlockSpec((tq, g), lambda t, sl: (t, 0))],
            out_specs=[pl.BlockSpec((n_heads, AUG_DIM, tq), lambda t, sl: (0, 0, t)),
                       pl.BlockSpec((n_heads, tq, AUG_DIM), lambda t, sl: (0, t, 0))]),
        compiler_params=_cparams(("arbitrary",)),
        name="moba_select",
    )(slopes, qv_t, kmean, kgg)


def _attn_kernel(qa_ref, ka_ref, vt_ref, o_ref, sa_ref, sb_ref, *, heads):
    i = pl.program_id(1)
    tq = MOBA_BLOCK
    unit_keys = ATTN_UNIT * MOBA_BLOCK
    neg_inf = jnp.float32(-jnp.inf)

    def scores_of(j, nkeys):
        off = pl.multiple_of(j * MOBA_BLOCK, MOBA_BLOCK)
        return [jnp.dot(ka_ref[hh, pl.ds(off, nkeys), :], qa_ref[hh],
                        preferred_element_type=F32) for hh in range(heads)]

    def scores_into(ref, unit):
        for hh, s_t in enumerate(scores_of(ATTN_UNIT * unit, unit_keys)):
            ref[hh] = s_t

    def update(j, scores, carries, diagonal, nkeys):
        off = pl.multiple_of(j * MOBA_BLOCK, MOBA_BLOCK)
        stats = []
        for hh in range(heads):
            m = carries[hh][0]
            s_t = scores[hh]
            if diagonal:
                key = lax.broadcasted_iota(I32, s_t.shape, 0) - (nkeys - MOBA_BLOCK)
                qry = lax.broadcasted_iota(I32, s_t.shape, 1)
                s_t = jnp.where(key <= qry, s_t, neg_inf)
            m_new = jnp.maximum(m, jnp.max(s_t, axis=0, keepdims=True))
            stats.append((m_new, jnp.exp(m - m_new), jnp.exp(s_t - m_new).astype(BF16)))
        out = []
        ones = jnp.ones((ONES_ROWS, nkeys), BF16)
        for hh in range(heads):
            m_new, alpha, p = stats[hh]
            vb = vt_ref[hh * HEAD_DIM:(hh + 1) * HEAD_DIM, pl.ds(off, nkeys)]
            vb1 = jnp.concatenate([vb, ones], axis=0)
            acc_new = alpha * carries[hh][1] + jnp.dot(vb1, p, preferred_element_type=F32)
            out.append((m_new, acc_new))
        return tuple(out)

    init = tuple((jnp.full((1, tq), neg_inf, F32), jnp.zeros((HEAD_DIM + ONES_ROWS, tq), F32))
                 for _ in range(heads))
    def tail_update(r):
        nkeys = (r + 1) * MOBA_BLOCK
        return lambda cs: update(i - r, scores_of(i - r, nkeys), cs, True, nkeys)

    carries = lax.switch(i % ATTN_UNIT, [tail_update(r) for r in range(ATTN_UNIT)], init)

    n_units = i // ATTN_UNIT
    last_unit = jnp.maximum(n_units - 1, 0)

    @pl.when(n_units > 0)
    def _():
        scores_into(sa_ref, 0)

    def from_ref(ref):
        return [ref[hh] for hh in range(heads)]

    def two_units(k, cs):
        scores_into(sb_ref, 2 * k + 1)
        cs = update(2 * ATTN_UNIT * k, from_ref(sa_ref), cs, False, unit_keys)
        scores_into(sa_ref, jnp.minimum(2 * k + 2, last_unit))
        return update(2 * ATTN_UNIT * k + ATTN_UNIT, from_ref(sb_ref), cs, False, unit_keys)

    carries = lax.fori_loop(0, n_units // 2, two_units, carries)
    carries = lax.cond(
        n_units % 2 == 1,
        lambda cs: update(ATTN_UNIT * last_unit, from_ref(sa_ref), cs, False, unit_keys),
        lambda cs: cs, carries)
    for hh in range(heads):
        acc = carries[hh][1]
        out_t = acc[0:HEAD_DIM, :] / acc[HEAD_DIM:HEAD_DIM + 1, :]
        o_ref[:, hh * HEAD_DIM:(hh + 1) * HEAD_DIM] = out_t.T.astype(o_ref.dtype)


def _attention(q_aug, k_aug, qv_t, n_heads, nb, heads):
    s = k_aug.shape[1]
    groups = n_heads // heads
    once = pl.Buffered(1)
    return pl.pallas_call(
        functools.partial(_attn_kernel, heads=heads),
        out_shape=jax.ShapeDtypeStruct((s, n_heads * HEAD_DIM), BF16),
        grid=(groups, nb),
        in_specs=[pl.BlockSpec((heads, AUG_DIM, MOBA_BLOCK), lambda hg, i: (hg, 0, i)),
                  pl.BlockSpec((heads, s, AUG_DIM), lambda hg, i: (hg, 0, 0), pipeline_mode=once),
                  pl.BlockSpec((heads * HEAD_DIM, s), lambda hg, i: (groups + hg, 0),
                               pipeline_mode=once)],
        out_specs=pl.BlockSpec((MOBA_BLOCK, heads * HEAD_DIM), lambda hg, i: (i, hg)),
        scratch_shapes=[pltpu.VMEM((heads, ATTN_UNIT * MOBA_BLOCK, MOBA_BLOCK), F32),
                        pltpu.VMEM((heads, ATTN_UNIT * MOBA_BLOCK, MOBA_BLOCK), F32)],
        compiler_params=_cparams(("arbitrary", "arbitrary")),
        name="moba_attention",
    )(q_aug, k_aug, qv_t)


def _qv_conv_kernel(wt_ref, x_ref, ga_ref, gb_ref, w_ref, b_ref, g_ref, beta_ref, qv_ref, o_ref,
                    u_ref, ush_ref, *, ts, sub):
    t = pl.program_id(0)

    @pl.when(t == 0)
    def _():
        u_ref[0:CONV_HALO, :] = jnp.zeros((CONV_HALO, u_ref.shape[1]), F32)

    @pl.when(t > 0)
    def _():
        u_ref[0:CONV_HALO, :] = u_ref[ts:ts + CONV_HALO, :]

    x_b = x_ref[...].astype(BF16)
    n_slabs = min(8, ts // sub)
    groups_per_slab = (ts // sub) // n_slabs
    slab = wt_ref.shape[0] // n_slabs

    def project(k, anchor):
        rows = slice(k * slab, (k + 1) * slab)
        x_k = x_b if anchor is None else x_b + anchor
        qv_ref[rows, :] = lax.dot_general(wt_ref[rows, :], x_k, (((1,), (1,)), ((), ())),
                                          preferred_element_type=F32).astype(qv_ref.dtype)

    gb = gb_ref[...].astype(F32)
    u_ref[CONV_HALO:CONV_HALO + ts, :] = ga_ref[...].astype(F32) * jax.nn.sigmoid(gb)
    for b in range(SUBLANES):
        ush_ref[b, 0:ts + CONV_HALO - b, :] = u_ref[b:ts + CONV_HALO, :]
    first = CONV_HALO - (CONV_WIDTH - 1)
    anchor = None
    for r0 in range(0, ts, sub):
        if (r0 // sub) % groups_per_slab == 0:
            project(r0 // sub // groups_per_slab, anchor)
        acc = jnp.broadcast_to(b_ref[...], (sub, u_ref.shape[1]))
        for tap in range(CONV_WIDTH):
            off = first + tap
            aligned = r0 + off - off % SUBLANES
            acc = acc + w_ref[tap:tap + 1, :] * ush_ref[off % SUBLANES, aligned:aligned + sub, :]
        mu = jnp.mean(acc, axis=-1, keepdims=True)
        cen = acc - mu
        var = jnp.mean(cen * cen, axis=-1, keepdims=True)
        y = cen * lax.rsqrt(var + LN_EPS) * g_ref[...] + beta_ref[...]
        o_ref[r0:r0 + sub, :] = (y * jax.nn.sigmoid(y)).astype(o_ref.dtype)
        bits = pltpu.bitcast(y[0:1, 0:1], I32)
        zero = lax.shift_right_logical(lax.shift_right_logical(bits, jnp.int32(31)), jnp.int32(1))
        anchor = zero.astype(BF16)


def _transpose_cast_kernel(w_ref, o_ref):
    o_ref[...] = w_ref[...].T.astype(o_ref.dtype)


def _qv_weights_transposed(w_in, g):
    d = w_in.shape[0]
    cols = min(256, g)
    per_group = g // cols
    return pl.pallas_call(
        _transpose_cast_kernel,
        out_shape=jax.ShapeDtypeStruct((2 * g, d), BF16),
        grid=(2 * per_group,),
        in_specs=[pl.BlockSpec((d, cols), lambda c: (0, c + jnp.where(c >= per_group, per_group, 0)))],
        out_specs=pl.BlockSpec((cols, d), lambda c: (c, 0)),
        compiler_params=_cparams(("arbitrary",)),
        name="qv_weight_transpose",
    )(w_in)


def _qv_projection_and_conv(x2, w_in, kgg, conv_w, conv_b, conv_ln_g, conv_ln_b, g, ts):
    s, d = x2.shape
    row = lambda v: v.reshape(1, g).astype(F32)
    w_qv_t = _qv_weights_transposed(w_in, g)
    once = pl.Buffered(1)
    return pl.pallas_call(
        functools.partial(_qv_conv_kernel, ts=ts, sub=32),
        out_shape=(jax.ShapeDtypeStruct((2 * g, s), BF16), jax.ShapeDtypeStruct((s, g), BF16)),
        grid=(s // ts,),
        in_specs=[pl.BlockSpec((2 * g, d), lambda t: (0, 0), pipeline_mode=once),
                  pl.BlockSpec((ts, d), lambda t: (t, 0)),
                  pl.BlockSpec((ts, g), lambda t: (t, 1)),
                  pl.BlockSpec((ts, g), lambda t: (t, 2)),
                  pl.BlockSpec((CONV_WIDTH, g), lambda t: (0, 0)),
                  pl.BlockSpec((1, g), lambda t: (0, 0)),
                  pl.BlockSpec((1, g), lambda t: (0, 0)),
                  pl.BlockSpec((1, g), lambda t: (0, 0))],
        out_specs=[pl.BlockSpec((2 * g, ts), lambda t: (0, t)),
                   pl.BlockSpec((ts, g), lambda t: (t, 0))],
        scratch_shapes=[pltpu.VMEM((ts + CONV_HALO, g), F32),
                        pltpu.VMEM((SUBLANES, ts + CONV_HALO, g), F32)],
        compiler_params=_cparams(("arbitrary",)),
        name="qv_proj_conformer_conv",
    )(w_qv_t, x2, kgg, kgg, conv_w, row(conv_b), row(conv_ln_g), row(conv_ln_b))


def _layer_norm_rows(z, gain, bias):
    mu = jnp.mean(z, axis=-1, keepdims=True)
    cen = z - mu
    var = jnp.mean(cen * cen, axis=-1, keepdims=True)
    return cen * lax.rsqrt(var + LN_EPS) * gain + bias


HI_HALF = -65536


def _pack_halves(lo, hi):
    lo_bits = pltpu.bitcast(lo.astype(BF16).astype(F32), I32)
    hi_bits = pltpu.bitcast(hi.astype(BF16).astype(F32), I32)
    return lax.shift_right_logical(lo_bits, jnp.int32(16)) | (hi_bits & jnp.int32(HI_HALF))


def _unpack_halves(words):
    lo = pltpu.bitcast(lax.shift_left(words, jnp.int32(16)), F32)
    hi = pltpu.bitcast(words & jnp.int32(HI_HALF), F32)
    return lo, hi


def _outproj_kernel(attn_ref, conv_ref, wo_ref, x_ref, g_ref, b_ref, wr_ref, br_ref,
                    h_ref, hp_ref, lg_ref, *, g):
    tm = x_ref.shape[0]
    sub = min(tm, 256)
    half = x_ref.shape[1] // 2
    n_exp = lg_ref.shape[0]
    wr = wr_ref[...]
    w_hi = wr.astype(BF16)
    w_lo = (wr - w_hi.astype(F32)).astype(BF16)
    mixes = []
    for r0 in range(0, tm, sub):
        mix = jnp.dot(attn_ref[r0:r0 + sub, :], wo_ref[0:g, :], preferred_element_type=F32)
        mixes.append(mix + jnp.dot(conv_ref[r0:r0 + sub, :], wo_ref[g:2 * g, :],
                                   preferred_element_type=F32))
    for k, r0 in enumerate(range(0, tm, sub)):
        h1 = _layer_norm_rows(DEEPNORM_ALPHA * x_ref[r0:r0 + sub, :] + mixes[k],
                              g_ref[...], b_ref[...])
        h_ref[r0:r0 + sub, :] = h1
        hp_ref[r0:r0 + sub, :] = _pack_halves(h1[:, 0:half], h1[:, half:])
        h_hi = h1.astype(BF16)
        h_lo = (h1 - h_hi.astype(F32)).astype(BF16)
        lg = jnp.dot(h_hi, w_hi, preferred_element_type=F32)
        lg = lg + jnp.dot(h_lo, w_hi, preferred_element_type=F32)
        lg = lg + jnp.dot(h_hi, w_lo, preferred_element_type=F32)
        lg_ref[:, r0:r0 + sub] = lg.T[0:n_exp, :] + br_ref[...]


def _out_projection(attn, conv, w_out, x2, ln_g, ln_b, w_router, b_router, tm):
    s, d = x2.shape
    g = attn.shape[1]
    e = w_router.shape[1]
    return pl.pallas_call(
        functools.partial(_outproj_kernel, g=g),
        out_shape=(jax.ShapeDtypeStruct((s, d), F32), jax.ShapeDtypeStruct((s, d // 2), I32),
                   jax.ShapeDtypeStruct((e, s), F32)),
        grid=(s // tm,),
        in_specs=[pl.BlockSpec((tm, g), lambda m: (m, 0)),
                  pl.BlockSpec((tm, g), lambda m: (m, 0)),
                  pl.BlockSpec((2 * g, d), lambda m: (0, 0), pipeline_mode=pl.Buffered(1)),
                  pl.BlockSpec((tm, d), lambda m: (m, 0)),
                  pl.BlockSpec((1, d), lambda m: (0, 0)),
                  pl.BlockSpec((1, d), lambda m: (0, 0)),
                  pl.BlockSpec((d, LANES), lambda m: (0, 0)),
                  pl.BlockSpec((e, 1), lambda m: (0, 0))],
        out_specs=[pl.BlockSpec((tm, d), lambda m: (m, 0)),
                   pl.BlockSpec((tm, d // 2), lambda m: (m, 0)),
                   pl.BlockSpec((e, tm), lambda m: (0, m))],
        compiler_params=_cparams(("arbitrary",)),
        name="outproj_ln1_router",
    )(attn, conv, w_out.astype(BF16), x2, ln_g.reshape(1, d), ln_b.reshape(1, d),
      jnp.pad(w_router, ((0, 0), (0, LANES - e))), b_router.reshape(e, 1))


def _route_kernel(lg_ref, e_ref, gate_ref, rank_ref, cnt_ref, carry_ref, *, n_exp, tr):
    t = pl.program_id(0)

    @pl.when(t == 0)
    def _():
        carry_ref[...] = jnp.zeros_like(carry_ref)

    neg_inf = jnp.float32(-jnp.inf)
    cur = lg_ref[...]
    j = lax.broadcasted_iota(I32, (n_exp, tr), 0)
    vals, picks = [], []
    for r in range(TOP_K):
        m = jnp.max(cur, axis=0, keepdims=True)
        idx = jnp.min(jnp.where(cur == m, j, n_exp), axis=0, keepdims=True)
        pick = j == idx
        e_ref[r:r + 1, :] = idx
        vals.append(m)
        picks.append(pick)
        cur = jnp.where(pick, neg_inf, cur)
    exps = [jnp.exp(v - vals[0]) for v in vals]
    den = exps[0]
    for r in range(1, TOP_K):
        den = den + exps[r]
    for r in range(TOP_K):
        gate_ref[r:r + 1, :] = exps[r] / den

    chosen = picks[0].astype(F32)
    for r in range(1, TOP_K):
        chosen = chosen + picks[r].astype(F32)
    a = lax.broadcasted_iota(I32, (tr, tr), 0)
    b = lax.broadcasted_iota(I32, (tr, tr), 1)
    upper = (a < b).astype(BF16)
    excl = jnp.dot(chosen.astype(BF16), upper, preferred_element_type=F32)
    base = carry_ref[:, 0:1]
    rank = excl + base
    for r in range(TOP_K):
        rank_ref[r:r + 1, :] = jnp.sum(jnp.where(picks[r], rank, 0.0), axis=0,
                                       keepdims=True).astype(I32)
    total = base + jnp.sum(chosen, axis=1, keepdims=True)
    carry_ref[...] = jnp.broadcast_to(total, carry_ref.shape)
    cnt_ref[...] = jnp.broadcast_to(total, cnt_ref.shape).astype(I32)


def _route(logits_t, tr):
    n_exp, s = logits_t.shape
    return pl.pallas_call(
        functools.partial(_route_kernel, n_exp=n_exp, tr=tr),
        out_shape=(jax.ShapeDtypeStruct((TOP_K, s), I32), jax.ShapeDtypeStruct((TOP_K, s), F32),
                   jax.ShapeDtypeStruct((TOP_K, s), I32), jax.ShapeDtypeStruct((n_exp, 128), I32)),
        grid=(s // tr,),
        in_specs=[pl.BlockSpec((n_exp, tr), lambda t: (0, t))],
        out_specs=[pl.BlockSpec((TOP_K, tr), lambda t: (0, t)),
                   pl.BlockSpec((TOP_K, tr), lambda t: (0, t)),
                   pl.BlockSpec((TOP_K, tr), lambda t: (0, t)),
                   pl.BlockSpec((n_exp, 128), lambda t: (0, 0))],
        scratch_shapes=[pltpu.VMEM((n_exp, 128), F32)],
        compiler_params=_cparams(("arbitrary",)),
        name="moe_route",
    )(logits_t)


SC_CORES = 2
SC_SUBCORES = 16
SC_LANES = 16
SC_GATHER_ROWS = 32
SC_SCAN = 8192


_SC_PARAMS = pltpu.CompilerParams(needs_layout_passes=False)


def _sc_mesh():
    return plsc.VectorSubcoreMesh(core_axis_name="c", subcore_axis_name="s",
                                  num_cores=SC_CORES, num_subcores=SC_SUBCORES)


def _sc_worker():
    return lax.axis_index("s") * SC_CORES + lax.axis_index("c")


def _sc_gather_loop(table_hbm, idx_all, out_hbm, bufs, sems, base, n_chunks):
    chunk = SC_GATHER_ROWS

    def gather(ci, b):
        return pltpu.make_async_copy(
            table_hbm.at[idx_all.at[pl.ds(ci * chunk, chunk)]], bufs[b], sems[b])

    gather(0, 0).start()

    @pl.loop(0, n_chunks, step=2)
    def _(ci):
        gather(ci + 1, 1).start()
        gather(ci, 0).wait()
        pltpu.sync_copy(bufs[0], out_hbm.at[pl.ds(base + ci * chunk, chunk)])

        @pl.when(ci + 2 < n_chunks)
        def _():
            gather(ci + 2, 0).start()

        gather(ci + 1, 1).wait()
        pltpu.sync_copy(bufs[1], out_hbm.at[pl.ds(base + (ci + 1) * chunk, chunk)])


def _sc_dest(ps_v, e_buf, r_buf, v):
    e_vec = e_buf[pl.ds(v * SC_LANES, SC_LANES)]
    return plsc.load_gather(ps_v, [e_vec]) + r_buf[pl.ds(v * SC_LANES, SC_LANES)]


def _sc_dispatch(table, e_flat, rank_flat, pad_start, cap, s):
    n_assign, width = e_flat.shape[0], table.shape[1]
    n_workers = SC_CORES * SC_SUBCORES
    per_worker = cap // n_workers
    chunk = SC_GATHER_ROWS
    assert cap % (n_workers * 2 * chunk) == 0 and n_assign % SC_SCAN == 0
    assert per_worker % SC_LANES == 0 and s % SC_SCAN == 0

    @functools.partial(
        pl.kernel, mesh=_sc_mesh(), out_type=jax.ShapeDtypeStruct((cap, width), table.dtype),
        scratch_types=[pltpu.VMEM((per_worker,), I32), pltpu.VMEM((pad_start.shape[0],), I32),
                       pltpu.VMEM((SC_SCAN,), I32), pltpu.VMEM((SC_SCAN,), I32),
                       pltpu.VMEM((chunk, width), table.dtype),
                       pltpu.VMEM((chunk, width), table.dtype),
                       pltpu.SemaphoreType.DMA, pltpu.SemaphoreType.DMA],
        compiler_params=_SC_PARAMS, name="sc_dispatch")
    def dispatch(table_hbm, e_hbm, r_hbm, ps_hbm, out_hbm, idx_all, ps_v, e_buf, r_buf,
                 buf0, buf1, sem0, sem1):
        base = _sc_worker() * per_worker
        lane = lax.iota(I32, SC_LANES)
        pltpu.sync_copy(ps_hbm, ps_v)

        @pl.loop(0, per_worker // SC_LANES)
        def _(k):
            idx_all[pl.ds(k * SC_LANES, SC_LANES)] = lax.rem(base + k * SC_LANES + lane, s)

        @pl.loop(0, n_assign // SC_SCAN)
        def _(c):
            pltpu.sync_copy(e_hbm.at[pl.ds(c * SC_SCAN, SC_SCAN)], e_buf)
            pltpu.sync_copy(r_hbm.at[pl.ds(c * SC_SCAN, SC_SCAN)], r_buf)
            tok0 = lax.rem(c * SC_SCAN, s)

            @pl.loop(0, SC_SCAN // SC_LANES)
            def _(v):
                loc = _sc_dest(ps_v, e_buf, r_buf, v) - base
                mine = jnp.logical_and(loc >= 0, loc < per_worker)
                tok = tok0 + v * SC_LANES + lane
                plsc.store_scatter(idx_all, [jnp.where(mine, loc, 0)], tok, mask=mine)

        _sc_gather_loop(table_hbm, idx_all, out_hbm, (buf0, buf1), (sem0, sem1), base,
                        per_worker // chunk)

    return dispatch(table, e_flat, rank_flat, pad_start)


def _sc_combine_gather(y, e_flat, rank_flat, pad_start):
    n_assign, width = e_flat.shape[0], y.shape[1]
    n_workers = SC_CORES * SC_SUBCORES
    per_worker = n_assign // n_workers
    chunk = SC_GATHER_ROWS
    assert n_assign % (n_workers * 2 * chunk) == 0

    @functools.partial(
        pl.kernel, mesh=_sc_mesh(), out_type=jax.ShapeDtypeStruct((n_assign, width), y.dtype),
        scratch_types=[pltpu.VMEM((per_worker,), I32), pltpu.VMEM((pad_start.shape[0],), I32),
                       pltpu.VMEM((per_worker,), I32), pltpu.VMEM((per_worker,), I32),
                       pltpu.VMEM((chunk, width), y.dtype), pltpu.VMEM((chunk, width), y.dtype),
                       pltpu.SemaphoreType.DMA, pltpu.SemaphoreType.DMA],
        compiler_params=_SC_PARAMS, name="sc_combine_gather")
    def combine(y_hbm, e_hbm, r_hbm, ps_hbm, out_hbm, idx_all, ps_v, e_buf, r_buf,
                buf0, buf1, sem0, sem1):
        base = _sc_worker() * per_worker
        pltpu.sync_copy(ps_hbm, ps_v)
        pltpu.sync_copy(e_hbm.at[pl.ds(base, per_worker)], e_buf)
        pltpu.sync_copy(r_hbm.at[pl.ds(base, per_worker)], r_buf)

        @pl.loop(0, per_worker // SC_LANES)
        def _(v):
            idx_all[pl.ds(v * SC_LANES, SC_LANES)] = _sc_dest(ps_v, e_buf, r_buf, v)

        _sc_gather_loop(y_hbm, idx_all, out_hbm, (buf0, buf1), (sem0, sem1), base,
                        per_worker // chunk)

    return combine(y, e_flat, rank_flat, pad_start)


def _expert_kernel(item_e, item_start, item_rows, xs_ref, wg_ref, wu_ref, wd_ref,
                   bg_ref, bu_ref, bd_ref, y_ref, xin_ref, acc_ref,
                   sem_x, sem_y, *, nf, n_items, big):
    w = pl.program_id(0)
    f = pl.program_id(1)
    rows = item_rows[w]
    slot = w % 2
    half = xin_ref.shape[2]
    has_next = jnp.logical_and(w + 1 < n_items, item_rows[jnp.minimum(w + 1, n_items - 1)] > 0)

    def x_copy(item, sl, c):
        src = pl.multiple_of(item_start[item] + c * ROUTE_PAD, ROUTE_PAD)
        dst = pl.multiple_of(c * ROUTE_PAD, ROUTE_PAD)
        return pltpu.make_async_copy(xs_ref.at[pl.ds(src, ROUTE_PAD)],
                                     xin_ref.at[sl, pl.ds(dst, ROUTE_PAD)], sem_x.at[sl])

    def y_copy(item, sl, c):
        src = pl.multiple_of(c * ROUTE_PAD, ROUTE_PAD)
        dst = pl.multiple_of(item_start[item] + c * ROUTE_PAD, ROUTE_PAD)
        return pltpu.make_async_copy(xin_ref.at[sl, pl.ds(src, ROUTE_PAD)],
                                     y_ref.at[pl.ds(dst, ROUTE_PAD)], sem_y)

    def for_chunks(item, fn):
        def body(c, carry):
            fn(c)
            return carry
        lax.fori_loop(0, item_rows[item] // ROUTE_PAD, body, 0)

    def compute(r0, size, first):
        lo, hi = _unpack_halves(xin_ref[slot, pl.ds(r0, size), :])
        lo = lo.astype(BF16)
        hi = hi.astype(BF16)

        def x_dot(w_ref):
            return (jnp.dot(lo, w_ref[0, 0:half, :].astype(BF16), preferred_element_type=F32)
                    + jnp.dot(hi, w_ref[0, half:2 * half, :].astype(BF16),
                              preferred_element_type=F32))

        gate = x_dot(wg_ref) + bg_ref[0]
        up = x_dot(wu_ref) + bu_ref[0]
        gate = jnp.minimum(gate, SWIGLU_LIMIT)
        up = jnp.clip(up, -SWIGLU_LIMIT, SWIGLU_LIMIT)
        act = (up + 1.0) * (gate * jax.nn.sigmoid(SWIGLU_ALPHA * gate))
        part = jnp.dot(act.astype(BF16), wd_ref[0].astype(BF16), preferred_element_type=F32)
        if first:
            acc_ref[pl.ds(r0, size), :] = part + bd_ref[0]
        else:
            acc_ref[pl.ds(r0, size), :] += part

    def compute_all(first):
        n_big = rows // big
        rem = rows - n_big * big
        merge = jnp.logical_and(n_big >= 1, jnp.logical_and(rem > 0, rem <= big // 2))
        n_loop = n_big - jnp.where(merge, 1, 0)

        def body(c, carry):
            compute(pl.multiple_of(c * big, big), big, first)
            return carry
        lax.fori_loop(0, n_loop, body, 0)
        for extra in range(ROUTE_PAD, big // 2 + 1, ROUTE_PAD):
            @pl.when(jnp.logical_and(merge, rem == extra))
            def _(extra=extra):
                compute(pl.multiple_of(n_loop * big, big), big + extra, first)

        @pl.when(jnp.logical_not(merge))
        def _():
            done = n_big * big
            size = big // 2
            while size >= ROUTE_PAD:
                take = ((rows - done) & size) != 0
                here = done

                @pl.when(take)
                def _(here=here, size=size):
                    compute(pl.multiple_of(here, ROUTE_PAD), size, first)
                done = done + jnp.where(take, size, 0)
                size //= 2

    @pl.when(rows > 0)
    def _():
        @pl.when(f == 0)
        def _():
            @pl.when(w == 0)
            def _():
                for_chunks(w, lambda c: x_copy(w, slot, c).start())

            for_chunks(w, lambda c: x_copy(w, slot, c).wait())

            @pl.when(w > 0)
            def _():
                for_chunks(w - 1, lambda c: y_copy(w - 1, 1 - slot, c).wait())

            @pl.when(has_next)
            def _():
                for_chunks(w + 1, lambda c: x_copy(w + 1, 1 - slot, c).start())

            compute_all(True)

        @pl.when(f > 0)
        def _():
            compute_all(False)

        @pl.when(f == nf - 1)
        def _():
            def pack(c):
                r0 = pl.multiple_of(c * ROUTE_PAD, ROUTE_PAD)
                xin_ref[slot, pl.ds(r0, ROUTE_PAD), :] = _pack_halves(
                    acc_ref[pl.ds(r0, ROUTE_PAD), 0:half],
                    acc_ref[pl.ds(r0, ROUTE_PAD), half:2 * half])
                y_copy(w, slot, c).start()
            for_chunks(w, pack)

            @pl.when(jnp.logical_not(has_next))
            def _():
                for_chunks(w, lambda c: y_copy(w, slot, c).wait())


def _experts(item_e, item_start, item_rows, n_active, xs, w_gate_up, b_gate_up, w_down, b_down,
             tm, tf):
    cap, half = xs.shape
    d = 2 * half
    n_exp, _, two_f = w_gate_up.shape
    ff = two_f // 2
    nf = ff // tf
    n_items = item_e.shape[0]
    big = min(512, tm)
    assert big % ROUTE_PAD == 0 and (big & (big - 1)) == 0

    def fidx(w, f, rows):
        return jnp.where(rows[w] > 0, f, nf - 1)

    return pl.pallas_call(
        functools.partial(_expert_kernel, nf=nf, n_items=n_items, big=big),
        out_shape=jax.ShapeDtypeStruct((cap, half), I32),
        grid_spec=pltpu.PrefetchScalarGridSpec(
            num_scalar_prefetch=3,
            grid=(n_active, nf),
            in_specs=[
                pl.BlockSpec(memory_space=pl.ANY),
                pl.BlockSpec((1, d, tf), lambda w, f, ie, ist, ir: (ie[w], 0, fidx(w, f, ir))),
                pl.BlockSpec((1, d, tf), lambda w, f, ie, ist, ir: (ie[w], 0, nf + fidx(w, f, ir))),
                pl.BlockSpec((1, tf, d), lambda w, f, ie, ist, ir: (ie[w], fidx(w, f, ir), 0)),
                pl.BlockSpec((1, 1, tf), lambda w, f, ie, ist, ir: (ie[w], 0, fidx(w, f, ir))),
                pl.BlockSpec((1, 1, tf), lambda w, f, ie, ist, ir: (ie[w], 0, nf + fidx(w, f, ir))),
                pl.BlockSpec((1, 1, d), lambda w, f, ie, ist, ir: (ie[w], 0, 0)),
            ],
            out_specs=pl.BlockSpec(memory_space=pl.ANY),
            scratch_shapes=[pltpu.VMEM((2, tm, half), I32), pltpu.VMEM((tm, d), F32),
                            pltpu.SemaphoreType.DMA((2,)), pltpu.SemaphoreType.DMA]),
        compiler_params=_cparams(("arbitrary", "arbitrary")),
        name="moe_experts",
    )(item_e, item_start, item_rows, xs, w_gate_up, w_gate_up, w_down,
      b_gate_up.reshape(n_exp, 1, two_f), b_gate_up.reshape(n_exp, 1, two_f),
      b_down.reshape(n_exp, 1, d))


def _combine_kernel(yg_ref, gate_ref, h_ref, g_ref, b_ref, o_ref):
    gates = gate_ref[...]
    half = yg_ref.shape[2]
    z_lo = DEEPNORM_ALPHA * h_ref[:, 0:half]
    z_hi = DEEPNORM_ALPHA * h_ref[:, half:2 * half]
    for r in range(TOP_K):
        lo, hi = _unpack_halves(yg_ref[r])
        z_lo = z_lo + gates[:, r:r + 1] * lo
        z_hi = z_hi + gates[:, r:r + 1] * hi
    inv_d = 1.0 / (2 * half)
    mu = (jnp.sum(z_lo, axis=-1, keepdims=True) + jnp.sum(z_hi, axis=-1, keepdims=True)) * inv_d
    c_lo = z_lo - mu
    c_hi = z_hi - mu
    var = (jnp.sum(c_lo * c_lo, axis=-1, keepdims=True)
           + jnp.sum(c_hi * c_hi, axis=-1, keepdims=True)) * inv_d
    rstd = lax.rsqrt(var + LN_EPS)
    o_ref[:, 0:half] = c_lo * rstd * g_ref[:, 0:half] + b_ref[:, 0:half]
    o_ref[:, half:2 * half] = c_hi * rstd * g_ref[:, half:2 * half] + b_ref[:, half:2 * half]


def _combine_part_kernel(yg_ref, gate_ref, h_ref, g_ref, b_ref, prev_ref, o_ref):
    del prev_ref
    _combine_kernel(yg_ref, gate_ref, h_ref, g_ref, b_ref, o_ref)


def _combine(yg, gates, h1, ln_g, ln_b, tc, first_tile, out_prev):
    s, d = h1.shape
    n_tiles = yg.shape[1] // tc
    in_specs = [pl.BlockSpec((TOP_K, tc, d // 2), lambda t: (0, t, 0)),
                pl.BlockSpec((tc, TOP_K), lambda t: (t + first_tile, 0)),
                pl.BlockSpec((tc, d), lambda t: (t + first_tile, 0)),
                pl.BlockSpec((1, d), lambda t: (0, 0)),
                pl.BlockSpec((1, d), lambda t: (0, 0))]
    args = [yg, gates, h1, ln_g.reshape(1, d), ln_b.reshape(1, d)]
    if out_prev is None:
        body, aliases = _combine_kernel, {}
    else:
        body, aliases = _combine_part_kernel, {len(args): 0}
        in_specs.append(pl.BlockSpec(memory_space=pl.ANY))
        args.append(out_prev)
    return pl.pallas_call(
        body,
        out_shape=jax.ShapeDtypeStruct((s, d), F32),
        grid=(n_tiles,),
        in_specs=in_specs,
        out_specs=pl.BlockSpec((tc, d), lambda t: (t + first_tile, 0)),
        input_output_aliases=aliases,
        compiler_params=_cparams(("arbitrary",)),
        name="moe_combine_ln2",
    )(*args)


def _tiles(s, d, ff, n_heads):
    return dict(
        proj_tm=min(512, s), select_tq=min(512, s), conv_ts=min(512, s), outproj_tm=min(512, s),
        attn_heads=min(4, n_heads),
        route_tr=min(256, s), combine_tc=min(256, s),
        expert_tm=min(1280, max(ROUTE_PAD, (s * TOP_K // 16) // ROUTE_PAD * ROUTE_PAD)),
        expert_tf=min(512, ff))


def kernel(x, w_in, conv_w, conv_b, conv_ln_g, conv_ln_b, w_out, ln1_g, ln1_b,
           w_router, b_router, w_gate_up, b_gate_up, w_down, b_down, ln2_g, ln2_b):
    b, s, d = x.shape
    assert b == 1, "the kernels treat the sequence of the single batch element as the row axis"
    g = w_in.shape[1] // 5
    n_heads = g // HEAD_DIM
    nb = s // MOBA_BLOCK
    n_exp = w_router.shape[1]
    ff = w_down.shape[1]
    assert s % MOBA_BLOCK == 0 and nb % 8 == 0 and HEAD_DIM + nb + 3 <= AUG_DIM
    tl = _tiles(s, d, ff, n_heads)
    x2 = x.reshape(s, d)

    kgg = _in_projection(x2, w_in, g, tl["proj_tm"])
    qv_t, conv = _qv_projection_and_conv(x2, w_in, kgg, conv_w, conv_b, conv_ln_g, conv_ln_b, g,
                                         tl["conv_ts"])
    kmean = _block_means(kgg, g, nb)
    slopes = 2.0 ** (-(8.0 / n_heads) * jnp.arange(1, n_heads + 1, dtype=F32))
    q_aug, k_aug = _select(slopes, qv_t, kmean, kgg, n_heads, nb, tl["select_tq"])
    attn = _attention(q_aug, k_aug, qv_t, n_heads, nb, tl["attn_heads"])
    h1, h1_packed, logits_t = _out_projection(attn, conv, w_out, x2, ln1_g, ln1_b,
                                              w_router, b_router, tl["outproj_tm"])

    e_t, gate_t, rank_t, counts = _route(logits_t, tl["route_tr"])
    counts = counts[:, 0]
    padded = (counts + ROUTE_PAD - 1) // ROUTE_PAD * ROUTE_PAD
    pad_end = jnp.cumsum(padded)
    pad_start = (pad_end - padded).astype(I32)
    cap = s * TOP_K + n_exp * ROUTE_PAD
    e_flat = e_t.reshape(-1)
    rank_flat = rank_t.reshape(-1)
    xs = _sc_dispatch(h1_packed, e_flat, rank_flat, pad_start, cap, s)

    tm = tl["expert_tm"]
    n_items = cap // tm + n_exp
    per_e = (padded + tm - 1) // tm
    item_end = jnp.cumsum(per_e)
    item_ids = jnp.arange(n_items, dtype=I32)
    item_e = jnp.minimum(jnp.sum(item_ids[:, None] >= item_end[None, :], axis=1),
                         n_exp - 1).astype(I32)
    piece = item_ids - (item_end - per_e)[item_e]
    active = item_ids < item_end[-1]
    item_start = jnp.where(active, pad_start[item_e] + piece * tm, 0).astype(I32)
    item_rows = jnp.where(active, jnp.minimum(tm, padded[item_e] - piece * tm), 0).astype(I32)
    last_e = item_e[jnp.maximum(item_end[-1] - 1, 0)]
    item_e = jnp.where(active, item_e, last_e).astype(I32)

    y = _experts(item_e, item_start, item_rows, item_end[-1].astype(I32), xs,
                 w_gate_up, b_gate_up, w_down, b_down, tm, tl["expert_tf"])
    tc = tl["combine_tc"]
    s_part = s // COMBINE_PARTS
    gates = gate_t.T
    out = None
    for part in range(COMBINE_PARTS):
        cols = slice(part * s_part, (part + 1) * s_part)
        yg = _sc_combine_gather(y, e_t[:, cols].reshape(-1), rank_t[:, cols].reshape(-1), pad_start)
        out = _combine(yg.reshape(TOP_K, s_part, d // 2), gates, h1, ln2_g, ln2_b, tc,
                       part * (s_part // tc), out)
    return out.reshape(b, s, d)
```

```python
import functools

import jax
import jax.numpy as jnp
from jax import lax
from jax.experimental import pallas as pl
from jax.experimental.pallas import tpu as pltpu
from jax.experimental.pallas import tpu_sc as plsc

F32 = jnp.float32
BF16 = jnp.bfloat16
I32 = jnp.int32

HEAD_DIM = 128
MOBA_BLOCK = 256
MOBA_TOPK = 3
CONV_WIDTH = 31
CONV_HALO = 32
SUBLANES = 8
LANES = 128
TOP_K = 4
SWIGLU_ALPHA = 1.702
SWIGLU_LIMIT = 7.0
LN_EPS = 1e-5
DEPTH = 1
DEEPNORM_ALPHA = (2.0 * DEPTH) ** 0.25
ROUTE_PAD = 128
AUG_DIM = 256
MASK_NEG = -1e30
ONES_ROWS = 16
ATTN_UNIT = 4
COMBINE_PARTS = 2
VMEM_LIMIT = 56 * 1024 * 1024


def _cparams(sem):
    return pltpu.CompilerParams(dimension_semantics=sem, vmem_limit_bytes=VMEM_LIMIT)


def _proj_nn_kernel(x_ref, w_ref, o_ref, wb_ref):
    @pl.when(pl.program_id(1) == 0)
    def _():
        wb_ref[...] = w_ref[...].astype(BF16)

    o_ref[...] = jnp.dot(x_ref[...].astype(BF16), wb_ref[...],
                         preferred_element_type=F32).astype(o_ref.dtype)


def _in_projection(x2, w_in, g, tm):
    s, d = x2.shape
    return pl.pallas_call(
        _proj_nn_kernel,
        out_shape=jax.ShapeDtypeStruct((s, 3 * g), BF16),
        grid=(3, s // tm),
        in_specs=[pl.BlockSpec((tm, d), lambda n, m: (m, 0)),
                  pl.BlockSpec((d, g), lambda n, m: (0, jnp.where(n == 0, 1, n + 2)))],
        out_specs=pl.BlockSpec((tm, g), lambda n, m: (m, n)),
        scratch_shapes=[pltpu.VMEM((d, g), BF16)],
        compiler_params=_cparams(("arbitrary", "arbitrary")),
        name="proj_nn",
    )(x2, w_in)


def _kmean_kernel(k_ref, o_ref, *, blocks):
    for b in range(blocks):
        kb = k_ref[b * MOBA_BLOCK:(b + 1) * MOBA_BLOCK, :].astype(F32)
        o_ref[b:b + 1, :] = jnp.sum(kb, axis=0, keepdims=True) * (1.0 / MOBA_BLOCK)


def _block_means(kgg, g, nb):
    blocks = 8
    return pl.pallas_call(
        functools.partial(_kmean_kernel, blocks=blocks),
        out_shape=jax.ShapeDtypeStruct((nb, g), F32),
        grid=(nb // blocks,),
        in_specs=[pl.BlockSpec((blocks * MOBA_BLOCK, g), lambda i: (i, 0))],
        out_specs=pl.BlockSpec((blocks, g), lambda i: (i, 0)),
        compiler_params=_cparams(("arbitrary",)),
        name="moba_kmean",
    )(kgg)


def _select_kernel(slopes_ref, qt_ref, km_ref, k_ref, qa_ref, ka_ref, *, nb, tq, n_heads):
    t = pl.program_id(0)
    col = t * tq + lax.broadcasted_iota(I32, (nb, tq), 1)
    qblk = col // MOBA_BLOCK
    j = lax.broadcasted_iota(I32, (nb, tq), 0)
    neg_inf = jnp.float32(-jnp.inf)
    past = j < qblk
    own = j == qblk
    n_extra = AUG_DIM - HEAD_DIM - nb
    r = lax.broadcasted_iota(I32, (n_extra, tq), 0)
    qb = ((t * tq + lax.broadcasted_iota(I32, (n_extra, tq), 1)) // MOBA_BLOCK).astype(F32)
    extra_unit = jnp.where(r == 0, 1.0,
                           jnp.where(r == 1, float(MOBA_BLOCK),
                                     jnp.where(r == 2, -float(MOBA_BLOCK) * qb, 0.0)))
    n_aug = AUG_DIM - HEAD_DIM
    pos = t * tq + lax.broadcasted_iota(I32, (tq, n_aug), 0)
    kblk = pos // MOBA_BLOCK
    pib = pos % MOBA_BLOCK
    lane = lax.broadcasted_iota(I32, (tq, n_aug), 1)
    aug = jnp.where(lane < nb, (lane == kblk).astype(F32),
                    jnp.where(lane == nb, pib.astype(F32),
                              jnp.where(lane == nb + 1, kblk.astype(F32),
                                        jnp.where(lane == nb + 2, 1.0, 0.0)))).astype(BF16)
    scale = HEAD_DIM ** -0.5
    for h in range(n_heads):
        cols = slice(h * HEAD_DIM, (h + 1) * HEAD_DIM)
        q_t = qt_ref[cols, :].astype(F32)
        gate = jnp.dot(km_ref[:, cols], q_t, preferred_element_type=F32,
                       precision=lax.Precision.HIGHEST)
        gcur = jnp.where(past, gate, neg_inf)
        sel = own
        for _ in range(MOBA_TOPK):
            m = jnp.max(gcur, axis=0, keepdims=True)
            is_max = jnp.logical_and(gcur == m, m > neg_inf)
            idx = jnp.min(jnp.where(is_max, j, nb), axis=0, keepdims=True)
            pick = j == idx
            sel = jnp.logical_or(sel, pick)
            gcur = jnp.where(pick, neg_inf, gcur)
        qa_ref[h, 0:HEAD_DIM, :] = (q_t * scale).astype(BF16)
        qa_ref[h, HEAD_DIM:HEAD_DIM + nb, :] = jnp.where(sel, 0.0, MASK_NEG).astype(BF16)
        qa_ref[h, HEAD_DIM + nb:AUG_DIM, :] = (slopes_ref[h] * extra_unit).astype(BF16)
        ka_ref[h, :, 0:HEAD_DIM] = k_ref[:, cols]
        ka_ref[h, :, HEAD_DIM:AUG_DIM] = aug


def _select(slopes, qv_t, kmean, kgg, n_heads, nb, tq):
    s = kgg.shape[0]
    g = n_heads * HEAD_DIM
    return pl.pallas_call(
        functools.partial(_select_kernel, nb=nb, tq=tq, n_heads=n_heads),
        out_shape=(jax.ShapeDtypeStruct((n_heads, AUG_DIM, s), BF16),
                   jax.ShapeDtypeStruct((n_heads, s, AUG_DIM), BF16)),
        grid_spec=pltpu.PrefetchScalarGridSpec(
            num_scalar_prefetch=1,
            grid=(s // tq,),
            in_specs=[pl.BlockSpec((g, tq), lambda t, sl: (0, t)),
                      pl.BlockSpec((nb, g), lambda t, sl: (0, 0)),
                      pl.BlockSpec((tq, g), lambda t, sl: (t, 0))],
            out_specs=[pl.BlockSpec((n_heads, AUG_DIM, tq), lambda t, sl: (0, 0, t)),
                       pl.BlockSpec((n_heads, tq, AUG_DIM), lambda t, sl: (0, t, 0))]),
        compiler_params=_cparams(("arbitrary",)),
        name="moba_select",
    )(slopes, qv_t, kmean, kgg)


def _attn_kernel(qa_ref, ka_ref, vt_ref, o_ref, sa_ref, sb_ref, *, heads):
    i = pl.program_id(1)
    tq = MOBA_BLOCK
    unit_keys = ATTN_UNIT * MOBA_BLOCK
    neg_inf = jnp.float32(-jnp.inf)

    def scores_of(j, nkeys):
        off = pl.multiple_of(j * MOBA_BLOCK, MOBA_BLOCK)
        return [jnp.dot(ka_ref[hh, pl.ds(off, nkeys), :], qa_ref[hh],
                        preferred_element_type=F32) for hh in range(heads)]

    def scores_into(ref, unit):
        for hh, s_t in enumerate(scores_of(ATTN_UNIT * unit, unit_keys)):
            ref[hh] = s_t

    def update(j, scores, carries, diagonal, nkeys):
        off = pl.multiple_of(j * MOBA_BLOCK, MOBA_BLOCK)
        stats = []
        for hh in range(heads):
            m = carries[hh][0]
            s_t = scores[hh]
            if diagonal:
                key = lax.broadcasted_iota(I32, s_t.shape, 0) - (nkeys - MOBA_BLOCK)
                qry = lax.broadcasted_iota(I32, s_t.shape, 1)
                s_t = jnp.where(key <= qry, s_t, neg_inf)
            m_new = jnp.maximum(m, jnp.max(s_t, axis=0, keepdims=True))
            stats.append((m_new, jnp.exp(m - m_new), jnp.exp(s_t - m_new).astype(BF16)))
        out = []
        ones = jnp.ones((ONES_ROWS, nkeys), BF16)
        for hh in range(heads):
            m_new, alpha, p = stats[hh]
            vb = vt_ref[hh * HEAD_DIM:(hh + 1) * HEAD_DIM, pl.ds(off, nkeys)]
            vb1 = jnp.concatenate([vb, ones], axis=0)
            acc_new = alpha * carries[hh][1] + jnp.dot(vb1, p, preferred_element_type=F32)
            out.append((m_new, acc_new))
        return tuple(out)

    init = tuple((jnp.full((1, tq), neg_inf, F32), jnp.zeros((HEAD_DIM + ONES_ROWS, tq), F32))
                 for _ in range(heads))
    def tail_update(r):
        nkeys = (r + 1) * MOBA_BLOCK

        def branch(cs):
            tail_scores = scores_of(i - r, nkeys)
            scores_into(sa_ref, 0)
            return update(i - r, tail_scores, cs, True, nkeys)
        return branch

    carries = lax.switch(i % ATTN_UNIT, [tail_update(r) for r in range(ATTN_UNIT)], init)

    n_units = i // ATTN_UNIT
    last_unit = jnp.maximum(n_units - 1, 0)

    def from_ref(ref):
        return [ref[hh] for hh in range(heads)]

    def two_units(k, cs):
        scores_into(sb_ref, 2 * k + 1)
        cs = update(2 * ATTN_UNIT * k, from_ref(sa_ref), cs, False, unit_keys)
        scores_into(sa_ref, jnp.minimum(2 * k + 2, last_unit))
        return update(2 * ATTN_UNIT * k + ATTN_UNIT, from_ref(sb_ref), cs, False, unit_keys)

    carries = lax.fori_loop(0, n_units // 2, two_units, carries)
    carries = lax.cond(
        n_units % 2 == 1,
        lambda cs: update(ATTN_UNIT * last_unit, from_ref(sa_ref), cs, False, unit_keys),
        lambda cs: cs, carries)
    for hh in range(heads):
        acc = carries[hh][1]
        out_t = acc[0:HEAD_DIM, :] / acc[HEAD_DIM:HEAD_DIM + 1, :]
        o_ref[:, hh * HEAD_DIM:(hh + 1) * HEAD_DIM] = out_t.T.astype(o_ref.dtype)


def _attention(q_aug, k_aug, qv_t, n_heads, nb, heads):
    s = k_aug.shape[1]
    groups = n_heads // heads
    once = pl.Buffered(1)
    return pl.pallas_call(
        functools.partial(_attn_kernel, heads=heads),
        out_shape=jax.ShapeDtypeStruct((s, n_heads * HEAD_DIM), BF16),
        grid=(groups, nb),
        in_specs=[pl.BlockSpec((heads, AUG_DIM, MOBA_BLOCK), lambda hg, i: (hg, 0, i)),
                  pl.BlockSpec((heads, s, AUG_DIM), lambda hg, i: (hg, 0, 0), pipeline_mode=once),
                  pl.BlockSpec((heads * HEAD_DIM, s), lambda hg, i: (groups + hg, 0),
                               pipeline_mode=once)],
        out_specs=pl.BlockSpec((MOBA_BLOCK, heads * HEAD_DIM), lambda hg, i: (i, hg)),
        scratch_shapes=[pltpu.VMEM((heads, ATTN_UNIT * MOBA_BLOCK, MOBA_BLOCK), F32),
                        pltpu.VMEM((heads, ATTN_UNIT * MOBA_BLOCK, MOBA_BLOCK), F32)],
        compiler_params=_cparams(("arbitrary", "arbitrary")),
        name="moba_attention",
    )(q_aug, k_aug, qv_t)


def _qv_conv_kernel(wt_ref, x_ref, ga_ref, gb_ref, w_ref, b_ref, g_ref, beta_ref, qv_ref, o_ref,
                    u_ref, ush_ref, *, ts, sub):
    t = pl.program_id(0)

    @pl.when(t == 0)
    def _():
        u_ref[0:CONV_HALO, :] = jnp.zeros((CONV_HALO, u_ref.shape[1]), F32)

    @pl.when(t > 0)
    def _():
        u_ref[0:CONV_HALO, :] = u_ref[ts:ts + CONV_HALO, :]

    x_b = x_ref[...].astype(BF16)
    n_slabs = min(8, ts // sub)
    groups_per_slab = (ts // sub) // n_slabs
    slab = wt_ref.shape[0] // n_slabs

    def project(k, anchor):
        rows = slice(k * slab, (k + 1) * slab)
        x_k = x_b if anchor is None else x_b + anchor
        qv_ref[rows, :] = lax.dot_general(wt_ref[rows, :], x_k, (((1,), (1,)), ((), ())),
                                          preferred_element_type=F32).astype(qv_ref.dtype)

    gb = gb_ref[...].astype(F32)
    u_ref[CONV_HALO:CONV_HALO + ts, :] = ga_ref[...].astype(F32) * jax.nn.sigmoid(gb)
    for b in range(SUBLANES):
        ush_ref[b, 0:ts + CONV_HALO - b, :] = u_ref[b:ts + CONV_HALO, :]
    first = CONV_HALO - (CONV_WIDTH - 1)
    anchor = None
    for r0 in range(0, ts, sub):
        if (r0 // sub) % groups_per_slab == 0:
            project(r0 // sub // groups_per_slab, anchor)
        acc = jnp.broadcast_to(b_ref[...], (sub, u_ref.shape[1]))
        for tap in range(CONV_WIDTH):
            off = first + tap
            aligned = r0 + off - off % SUBLANES
            acc = acc + w_ref[tap:tap + 1, :] * ush_ref[off % SUBLANES, aligned:aligned + sub, :]
        mu = jnp.mean(acc, axis=-1, keepdims=True)
        cen = acc - mu
        var = jnp.mean(cen * cen, axis=-1, keepdims=True)
        y = cen * lax.rsqrt(var + LN_EPS) * g_ref[...] + beta_ref[...]
        o_ref[r0:r0 + sub, :] = (y * jax.nn.sigmoid(y)).astype(o_ref.dtype)
        bits = pltpu.bitcast(y[0:1, 0:1], I32)
        zero = lax.shift_right_logical(lax.shift_right_logical(bits, jnp.int32(31)), jnp.int32(1))
        anchor = zero.astype(BF16)


def _transpose_cast_kernel(w_ref, o_ref):
    o_ref[...] = w_ref[...].T.astype(o_ref.dtype)


def _qv_weights_transposed(w_in, g):
    d = w_in.shape[0]
    cols = min(256, g)
    per_group = g // cols
    return pl.pallas_call(
        _transpose_cast_kernel,
        out_shape=jax.ShapeDtypeStruct((2 * g, d), BF16),
        grid=(2 * per_group,),
        in_specs=[pl.BlockSpec((d, cols), lambda c: (0, c + jnp.where(c >= per_group, per_group, 0)))],
        out_specs=pl.BlockSpec((cols, d), lambda c: (c, 0)),
        compiler_params=_cparams(("arbitrary",)),
        name="qv_weight_transpose",
    )(w_in)


def _qv_projection_and_conv(x2, w_in, kgg, conv_w, conv_b, conv_ln_g, conv_ln_b, g, ts):
    s, d = x2.shape
    row = lambda v: v.reshape(1, g).astype(F32)
    w_qv_t = _qv_weights_transposed(w_in, g)
    once = pl.Buffered(1)
    return pl.pallas_call(
        functools.partial(_qv_conv_kernel, ts=ts, sub=32),
        out_shape=(jax.ShapeDtypeStruct((2 * g, s), BF16), jax.ShapeDtypeStruct((s, g), BF16)),
        grid=(s // ts,),
        in_specs=[pl.BlockSpec((2 * g, d), lambda t: (0, 0), pipeline_mode=once),
                  pl.BlockSpec((ts, d), lambda t: (t, 0)),
                  pl.BlockSpec((ts, g), lambda t: (t, 1)),
                  pl.BlockSpec((ts, g), lambda t: (t, 2)),
                  pl.BlockSpec((CONV_WIDTH, g), lambda t: (0, 0)),
                  pl.BlockSpec((1, g), lambda t: (0, 0)),
                  pl.BlockSpec((1, g), lambda t: (0, 0)),
                  pl.BlockSpec((1, g), lambda t: (0, 0))],
        out_specs=[pl.BlockSpec((2 * g, ts), lambda t: (0, t)),
                   pl.BlockSpec((ts, g), lambda t: (t, 0))],
        scratch_shapes=[pltpu.VMEM((ts + CONV_HALO, g), F32),
                        pltpu.VMEM((SUBLANES, ts + CONV_HALO, g), F32)],
        compiler_params=_cparams(("arbitrary",)),
        name="qv_proj_conformer_conv",
    )(w_qv_t, x2, kgg, kgg, conv_w, row(conv_b), row(conv_ln_g), row(conv_ln_b))


def _layer_norm_rows(z, gain, bias):
    mu = jnp.mean(z, axis=-1, keepdims=True)
    cen = z - mu
    var = jnp.mean(cen * cen, axis=-1, keepdims=True)
    return cen * lax.rsqrt(var + LN_EPS) * gain + bias


HI_HALF = -65536


def _pack_halves(lo, hi):
    lo_bits = pltpu.bitcast(lo.astype(BF16).astype(F32), I32)
    hi_bits = pltpu.bitcast(hi.astype(BF16).astype(F32), I32)
    return lax.shift_right_logical(lo_bits, jnp.int32(16)) | (hi_bits & jnp.int32(HI_HALF))


def _unpack_halves(words):
    lo = pltpu.bitcast(lax.shift_left(words, jnp.int32(16)), F32)
    hi = pltpu.bitcast(words & jnp.int32(HI_HALF), F32)
    return lo, hi


def _outproj_kernel(attn_ref, conv_ref, wo_ref, x_ref, g_ref, b_ref, wr_ref, br_ref,
                    h_ref, hp_ref, lg_ref, *, g):
    tm = x_ref.shape[0]
    sub = min(tm, 256)
    half = x_ref.shape[1] // 2
    n_exp = lg_ref.shape[0]
    wr = wr_ref[...]
    w_hi = wr.astype(BF16)
    w_lo = (wr - w_hi.astype(F32)).astype(BF16)
    mixes = []
    for r0 in range(0, tm, sub):
        mix = jnp.dot(attn_ref[r0:r0 + sub, :], wo_ref[0:g, :], preferred_element_type=F32)
        mixes.append(mix + jnp.dot(conv_ref[r0:r0 + sub, :], wo_ref[g:2 * g, :],
                                   preferred_element_type=F32))
    for k, r0 in enumerate(range(0, tm, sub)):
        h1 = _layer_norm_rows(DEEPNORM_ALPHA * x_ref[r0:r0 + sub, :] + mixes[k],
                              g_ref[...], b_ref[...])
        h_ref[r0:r0 + sub, :] = h1
        hp_ref[r0:r0 + sub, :] = _pack_halves(h1[:, 0:half], h1[:, half:])
        h_hi = h1.astype(BF16)
        h_lo = (h1 - h_hi.astype(F32)).astype(BF16)
        lg = jnp.dot(h_hi, w_hi, preferred_element_type=F32)
        lg = lg + jnp.dot(h_lo, w_hi, preferred_element_type=F32)
        lg = lg + jnp.dot(h_hi, w_lo, preferred_element_type=F32)
        lg_ref[:, r0:r0 + sub] = lg.T[0:n_exp, :] + br_ref[...]


def _out_projection(attn, conv, w_out, x2, ln_g, ln_b, w_router, b_router, tm):
    s, d = x2.shape
    g = attn.shape[1]
    e = w_router.shape[1]
    return pl.pallas_call(
        functools.partial(_outproj_kernel, g=g),
        out_shape=(jax.ShapeDtypeStruct((s, d), F32), jax.ShapeDtypeStruct((s, d // 2), I32),
                   jax.ShapeDtypeStruct((e, s), F32)),
        grid=(s // tm,),
        in_specs=[pl.BlockSpec((tm, g), lambda m: (m, 0)),
                  pl.BlockSpec((tm, g), lambda m: (m, 0)),
                  pl.BlockSpec((2 * g, d), lambda m: (0, 0), pipeline_mode=pl.Buffered(1)),
                  pl.BlockSpec((tm, d), lambda m: (m, 0)),
                  pl.BlockSpec((1, d), lambda m: (0, 0)),
                  pl.BlockSpec((1, d), lambda m: (0, 0)),
                  pl.BlockSpec((d, LANES), lambda m: (0, 0)),
                  pl.BlockSpec((e, 1), lambda m: (0, 0))],
        out_specs=[pl.BlockSpec((tm, d), lambda m: (m, 0)),
                   pl.BlockSpec((tm, d // 2), lambda m: (m, 0)),
                   pl.BlockSpec((e, tm), lambda m: (0, m))],
        compiler_params=_cparams(("arbitrary",)),
        name="outproj_ln1_router",
    )(attn, conv, w_out.astype(BF16), x2, ln_g.reshape(1, d), ln_b.reshape(1, d),
      jnp.pad(w_router, ((0, 0), (0, LANES - e))), b_router.reshape(e, 1))


def _route_kernel(lg_ref, e_ref, gate_ref, rank_ref, cnt_ref, carry_ref, *, n_exp, tr):
    t = pl.program_id(0)

    @pl.when(t == 0)
    def _():
        carry_ref[...] = jnp.zeros_like(carry_ref)

    neg_inf = jnp.float32(-jnp.inf)
    cur = lg_ref[...]
    j = lax.broadcasted_iota(I32, (n_exp, tr), 0)
    vals, picks = [], []
    for r in range(TOP_K):
        m = jnp.max(cur, axis=0, keepdims=True)
        idx = jnp.min(jnp.where(cur == m, j, n_exp), axis=0, keepdims=True)
        pick = j == idx
        e_ref[r:r + 1, :] = idx
        vals.append(m)
        picks.append(pick)
        cur = jnp.where(pick, neg_inf, cur)
    exps = [jnp.exp(v - vals[0]) for v in vals]
    den = exps[0]
    for r in range(1, TOP_K):
        den = den + exps[r]
    for r in range(TOP_K):
        gate_ref[r:r + 1, :] = exps[r] / den

    chosen = picks[0].astype(F32)
    for r in range(1, TOP_K):
        chosen = chosen + picks[r].astype(F32)
    a = lax.broadcasted_iota(I32, (tr, tr), 0)
    b = lax.broadcasted_iota(I32, (tr, tr), 1)
    upper = (a < b).astype(BF16)
    excl = jnp.dot(chosen.astype(BF16), upper, preferred_element_type=F32)
    base = carry_ref[:, 0:1]
    rank = excl + base
    for r in range(TOP_K):
        rank_ref[r:r + 1, :] = jnp.sum(jnp.where(picks[r], rank, 0.0), axis=0,
                                       keepdims=True).astype(I32)
    total = base + jnp.sum(chosen, axis=1, keepdims=True)
    carry_ref[...] = jnp.broadcast_to(total, carry_ref.shape)
    cnt_ref[...] = jnp.broadcast_to(total, cnt_ref.shape).astype(I32)


def _route(logits_t, tr):
    n_exp, s = logits_t.shape
    return pl.pallas_call(
        functools.partial(_route_kernel, n_exp=n_exp, tr=tr),
        out_shape=(jax.ShapeDtypeStruct((TOP_K, s), I32), jax.ShapeDtypeStruct((TOP_K, s), F32),
                   jax.ShapeDtypeStruct((TOP_K, s), I32), jax.ShapeDtypeStruct((n_exp, 128), I32)),
        grid=(s // tr,),
        in_specs=[pl.BlockSpec((n_exp, tr), lambda t: (0, t))],
        out_specs=[pl.BlockSpec((TOP_K, tr), lambda t: (0, t)),
                   pl.BlockSpec((TOP_K, tr), lambda t: (0, t)),
                   pl.BlockSpec((TOP_K, tr), lambda t: (0, t)),
                   pl.BlockSpec((n_exp, 128), lambda t: (0, 0))],
        scratch_shapes=[pltpu.VMEM((n_exp, 128), F32)],
        compiler_params=_cparams(("arbitrary",)),
        name="moe_route",
    )(logits_t)


SC_CORES = 2
SC_SUBCORES = 16
SC_LANES = 16
SC_GATHER_ROWS = 32
SC_SCAN = 8192


_SC_PARAMS = pltpu.CompilerParams(needs_layout_passes=False)


def _sc_mesh():
    return plsc.VectorSubcoreMesh(core_axis_name="c", subcore_axis_name="s",
                                  num_cores=SC_CORES, num_subcores=SC_SUBCORES)


def _sc_worker():
    return lax.axis_index("s") * SC_CORES + lax.axis_index("c")


def _sc_gather_loop(table_hbm, idx_all, out_hbm, bufs, sems, base, n_chunks):
    chunk = SC_GATHER_ROWS

    def gather(ci, b):
        return pltpu.make_async_copy(
            table_hbm.at[idx_all.at[pl.ds(ci * chunk, chunk)]], bufs[b], sems[b])

    gather(0, 0).start()

    @pl.loop(0, n_chunks, step=2)
    def _(ci):
        gather(ci + 1, 1).start()
        gather(ci, 0).wait()
        pltpu.sync_copy(bufs[0], out_hbm.at[pl.ds(base + ci * chunk, chunk)])

        @pl.when(ci + 2 < n_chunks)
        def _():
            gather(ci + 2, 0).start()

        gather(ci + 1, 1).wait()
        pltpu.sync_copy(bufs[1], out_hbm.at[pl.ds(base + (ci + 1) * chunk, chunk)])


def _sc_dest(ps_v, e_buf, r_buf, v):
    e_vec = e_buf[pl.ds(v * SC_LANES, SC_LANES)]
    return plsc.load_gather(ps_v, [e_vec]) + r_buf[pl.ds(v * SC_LANES, SC_LANES)]


def _sc_dispatch(table, e_flat, rank_flat, pad_start, cap, s):
    n_assign, width = e_flat.shape[0], table.shape[1]
    n_workers = SC_CORES * SC_SUBCORES
    per_worker = cap // n_workers
    chunk = SC_GATHER_ROWS
    assert cap % (n_workers * 2 * chunk) == 0 and n_assign % SC_SCAN == 0
    assert per_worker % SC_LANES == 0 and s % SC_SCAN == 0

    @functools.partial(
        pl.kernel, mesh=_sc_mesh(), out_type=jax.ShapeDtypeStruct((cap, width), table.dtype),
        scratch_types=[pltpu.VMEM((per_worker,), I32), pltpu.VMEM((pad_start.shape[0],), I32),
                       pltpu.VMEM((SC_SCAN,), I32), pltpu.VMEM((SC_SCAN,), I32),
                       pltpu.VMEM((chunk, width), table.dtype),
                       pltpu.VMEM((chunk, width), table.dtype),
                       pltpu.SemaphoreType.DMA, pltpu.SemaphoreType.DMA],
        compiler_params=_SC_PARAMS, name="sc_dispatch")
    def dispatch(table_hbm, e_hbm, r_hbm, ps_hbm, out_hbm, idx_all, ps_v, e_buf, r_buf,
                 buf0, buf1, sem0, sem1):
        base = _sc_worker() * per_worker
        lane = lax.iota(I32, SC_LANES)
        pltpu.sync_copy(ps_hbm, ps_v)

        @pl.loop(0, per_worker // SC_LANES)
        def _(k):
            idx_all[pl.ds(k * SC_LANES, SC_LANES)] = lax.rem(base + k * SC_LANES + lane, s)

        @pl.loop(0, n_assign // SC_SCAN)
        def _(c):
            pltpu.sync_copy(e_hbm.at[pl.ds(c * SC_SCAN, SC_SCAN)], e_buf)
            pltpu.sync_copy(r_hbm.at[pl.ds(c * SC_SCAN, SC_SCAN)], r_buf)
            tok0 = lax.rem(c * SC_SCAN, s)

            @pl.loop(0, SC_SCAN // SC_LANES)
            def _(v):
                loc = _sc_dest(ps_v, e_buf, r_buf, v) - base
                mine = jnp.logical_and(loc >= 0, loc < per_worker)
                tok = tok0 + v * SC_LANES + lane
                plsc.store_scatter(idx_all, [jnp.where(mine, loc, 0)], tok, mask=mine)

        _sc_gather_loop(table_hbm, idx_all, out_hbm, (buf0, buf1), (sem0, sem1), base,
                        per_worker // chunk)

    return dispatch(table, e_flat, rank_flat, pad_start)


def _sc_combine_gather(y, e_flat, rank_flat, pad_start):
    n_assign, width = e_flat.shape[0], y.shape[1]
    n_workers = SC_CORES * SC_SUBCORES
    per_worker = n_assign // n_workers
    chunk = SC_GATHER_ROWS
    assert n_assign % (n_workers * 2 * chunk) == 0

    @functools.partial(
        pl.kernel, mesh=_sc_mesh(), out_type=jax.ShapeDtypeStruct((n_assign, width), y.dtype),
        scratch_types=[pltpu.VMEM((per_worker,), I32), pltpu.VMEM((pad_start.shape[0],), I32),
                       pltpu.VMEM((per_worker,), I32), pltpu.VMEM((per_worker,), I32),
                       pltpu.VMEM((chunk, width), y.dtype), pltpu.VMEM((chunk, width), y.dtype),
                       pltpu.SemaphoreType.DMA, pltpu.SemaphoreType.DMA],
        compiler_params=_SC_PARAMS, name="sc_combine_gather")
    def combine(y_hbm, e_hbm, r_hbm, ps_hbm, out_hbm, idx_all, ps_v, e_buf, r_buf,
                buf0, buf1, sem0, sem1):
        base = _sc_worker() * per_worker
        pltpu.sync_copy(ps_hbm, ps_v)
        pltpu.sync_copy(e_hbm.at[pl.ds(base, per_worker)], e_buf)
        pltpu.sync_copy(r_hbm.at[pl.ds(base, per_worker)], r_buf)

        @pl.loop(0, per_worker // SC_LANES)
        def _(v):
            idx_all[pl.ds(v * SC_LANES, SC_LANES)] = _sc_dest(ps_v, e_buf, r_buf, v)

        _sc_gather_loop(y_hbm, idx_all, out_hbm, (buf0, buf1), (sem0, sem1), base,
                        per_worker // chunk)

    return combine(y, e_flat, rank_flat, pad_start)


def _expert_kernel(item_e, item_start, item_rows, xs_ref, wg_ref, wu_ref, wd_ref,
                   bg_ref, bu_ref, bd_ref, y_ref, xin_ref, acc_ref,
                   sem_x, sem_y, *, nf, n_items, big):
    w = pl.program_id(0)
    f = pl.program_id(1)
    rows = item_rows[w]
    slot = w % 2
    half = xin_ref.shape[2]
    has_next = jnp.logical_and(w + 1 < n_items, item_rows[jnp.minimum(w + 1, n_items - 1)] > 0)

    def x_copy(item, sl, c):
        src = pl.multiple_of(item_start[item] + c * ROUTE_PAD, ROUTE_PAD)
        dst = pl.multiple_of(c * ROUTE_PAD, ROUTE_PAD)
        return pltpu.make_async_copy(xs_ref.at[pl.ds(src, ROUTE_PAD)],
                                     xin_ref.at[sl, pl.ds(dst, ROUTE_PAD)], sem_x.at[sl])

    def y_copy(item, sl, c):
        src = pl.multiple_of(c * ROUTE_PAD, ROUTE_PAD)
        dst = pl.multiple_of(item_start[item] + c * ROUTE_PAD, ROUTE_PAD)
        return pltpu.make_async_copy(xin_ref.at[sl, pl.ds(src, ROUTE_PAD)],
                                     y_ref.at[pl.ds(dst, ROUTE_PAD)], sem_y)

    def for_chunks(item, fn):
        def body(c, carry):
            fn(c)
            return carry
        lax.fori_loop(0, item_rows[item] // ROUTE_PAD, body, 0)

    def compute(r0, size, first):
        lo, hi = _unpack_halves(xin_ref[slot, pl.ds(r0, size), :])
        lo = lo.astype(BF16)
        hi = hi.astype(BF16)

        def x_dot(w_ref):
            return (jnp.dot(lo, w_ref[0, 0:half, :].astype(BF16), preferred_element_type=F32)
                    + jnp.dot(hi, w_ref[0, half:2 * half, :].astype(BF16),
                              preferred_element_type=F32))

        gate = x_dot(wg_ref) + bg_ref[0]
        up = x_dot(wu_ref) + bu_ref[0]
        gate = jnp.minimum(gate, SWIGLU_LIMIT)
        up = jnp.clip(up, -SWIGLU_LIMIT, SWIGLU_LIMIT)
        act = (up + 1.0) * (gate * jax.nn.sigmoid(SWIGLU_ALPHA * gate))
        part = jnp.dot(act.astype(BF16), wd_ref[0].astype(BF16), preferred_element_type=F32)
        if first:
            acc_ref[pl.ds(r0, size), :] = part + bd_ref[0]
        else:
            acc_ref[pl.ds(r0, size), :] += part

    def compute_all(first):
        n_big = rows // big
        rem = rows - n_big * big
        merge = jnp.logical_and(n_big >= 1, jnp.logical_and(rem > 0, rem <= big // 2))
        n_loop = n_big - jnp.where(merge, 1, 0)

        def body(c, carry):
            compute(pl.multiple_of(c * big, big), big, first)
            return carry
        lax.fori_loop(0, n_loop, body, 0)
        for extra in range(ROUTE_PAD, big // 2 + 1, ROUTE_PAD):
            @pl.when(jnp.logical_and(merge, rem == extra))
            def _(extra=extra):
                compute(pl.multiple_of(n_loop * big, big), big + extra, first)

        @pl.when(jnp.logical_not(merge))
        def _():
            done = n_big * big
            size = big // 2
            while size >= ROUTE_PAD:
                take = ((rows - done) & size) != 0
                here = done

                @pl.when(take)
                def _(here=here, size=size):
                    compute(pl.multiple_of(here, ROUTE_PAD), size, first)
                done = done + jnp.where(take, size, 0)
                size //= 2

    @pl.when(rows > 0)
    def _():
        @pl.when(f == 0)
        def _():
            @pl.when(w == 0)
            def _():
                for_chunks(w, lambda c: x_copy(w, slot, c).start())

            for_chunks(w, lambda c: x_copy(w, slot, c).wait())

            @pl.when(w > 0)
            def _():
                for_chunks(w - 1, lambda c: y_copy(w - 1, 1 - slot, c).wait())

            @pl.when(has_next)
            def _():
                for_chunks(w + 1, lambda c: x_copy(w + 1, 1 - slot, c).start())

            compute_all(True)

        @pl.when(f > 0)
        def _():
            compute_all(False)

        @pl.when(f == nf - 1)
        def _():
            def pack(c):
                r0 = pl.multiple_of(c * ROUTE_PAD, ROUTE_PAD)
                xin_ref[slot, pl.ds(r0, ROUTE_PAD), :] = _pack_halves(
                    acc_ref[pl.ds(r0, ROUTE_PAD), 0:half],
                    acc_ref[pl.ds(r0, ROUTE_PAD), half:2 * half])
                y_copy(w, slot, c).start()
            for_chunks(w, pack)

            @pl.when(jnp.logical_not(has_next))
            def _():
                for_chunks(w, lambda c: y_copy(w, slot, c).wait())


def _experts(item_e, item_start, item_rows, n_active, xs, w_gate_up, b_gate_up, w_down, b_down,
             tm, tf):
    cap, half = xs.shape
    d = 2 * half
    n_exp, _, two_f = w_gate_up.shape
    ff = two_f // 2
    nf = ff // tf
    n_items = item_e.shape[0]
    big = min(512, tm)
    assert big % ROUTE_PAD == 0 and (big & (big - 1)) == 0

    def fidx(w, f, rows):
        return jnp.where(rows[w] > 0, f, nf - 1)

    return pl.pallas_call(
        functools.partial(_expert_kernel, nf=nf, n_items=n_items, big=big),
        out_shape=jax.ShapeDtypeStruct((cap, half), I32),
        grid_spec=pltpu.PrefetchScalarGridSpec(
            num_scalar_prefetch=3,
            grid=(n_active, nf),
            in_specs=[
                pl.BlockSpec(memory_space=pl.ANY),
                pl.BlockSpec((1, d, tf), lambda w, f, ie, ist, ir: (ie[w], 0, fidx(w, f, ir))),
                pl.BlockSpec((1, d, tf), lambda w, f, ie, ist, ir: (ie[w], 0, nf + fidx(w, f, ir))),
                pl.BlockSpec((1, tf, d), lambda w, f, ie, ist, ir: (ie[w], fidx(w, f, ir), 0)),
                pl.BlockSpec((1, 1, tf), lambda w, f, ie, ist, ir: (ie[w], 0, fidx(w, f, ir))),
                pl.BlockSpec((1, 1, tf), lambda w, f, ie, ist, ir: (ie[w], 0, nf + fidx(w, f, ir))),
                pl.BlockSpec((1, 1, d), lambda w, f, ie, ist, ir: (ie[w], 0, 0)),
            ],
            out_specs=pl.BlockSpec(memory_space=pl.ANY),
            scratch_shapes=[pltpu.VMEM((2, tm, half), I32), pltpu.VMEM((tm, d), F32),
                            pltpu.SemaphoreType.DMA((2,)), pltpu.SemaphoreType.DMA]),
        compiler_params=_cparams(("arbitrary", "arbitrary")),
        name="moe_experts",
    )(item_e, item_start, item_rows, xs, w_gate_up, w_gate_up, w_down,
      b_gate_up.reshape(n_exp, 1, two_f), b_gate_up.reshape(n_exp, 1, two_f),
      b_down.reshape(n_exp, 1, d))


def _combine_kernel(yg_ref, gate_ref, h_ref, g_ref, b_ref, o_ref):
    gates = gate_ref[...]
    half = yg_ref.shape[2]
    z_lo = DEEPNORM_ALPHA * h_ref[:, 0:half]
    z_hi = DEEPNORM_ALPHA * h_ref[:, half:2 * half]
    for r in range(TOP_K):
        lo, hi = _unpack_halves(yg_ref[r])
        z_lo = z_lo + gates[:, r:r + 1] * lo
        z_hi = z_hi + gates[:, r:r + 1] * hi
    inv_d = 1.0 / (2 * half)
    mu = (jnp.sum(z_lo, axis=-1, keepdims=True) + jnp.sum(z_hi, axis=-1, keepdims=True)) * inv_d
    c_lo = z_lo - mu
    c_hi = z_hi - mu
    var = (jnp.sum(c_lo * c_lo, axis=-1, keepdims=True)
           + jnp.sum(c_hi * c_hi, axis=-1, keepdims=True)) * inv_d
    rstd = lax.rsqrt(var + LN_EPS)
    o_ref[:, 0:half] = c_lo * rstd * g_ref[:, 0:half] + b_ref[:, 0:half]
    o_ref[:, half:2 * half] = c_hi * rstd * g_ref[:, half:2 * half] + b_ref[:, half:2 * half]


def _combine_part_kernel(yg_ref, gate_ref, h_ref, g_ref, b_ref, prev_ref, o_ref):
    del prev_ref
    _combine_kernel(yg_ref, gate_ref, h_ref, g_ref, b_ref, o_ref)


def _combine(yg, gates, h1, ln_g, ln_b, tc, first_tile, out_prev):
    s, d = h1.shape
    n_tiles = yg.shape[1] // tc
    in_specs = [pl.BlockSpec((TOP_K, tc, d // 2), lambda t: (0, t, 0)),
                pl.BlockSpec((tc, TOP_K), lambda t: (t + first_tile, 0)),
                pl.BlockSpec((tc, d), lambda t: (t + first_tile, 0)),
                pl.BlockSpec((1, d), lambda t: (0, 0)),
                pl.BlockSpec((1, d), lambda t: (0, 0))]
    args = [yg, gates, h1, ln_g.reshape(1, d), ln_b.reshape(1, d)]
    if out_prev is None:
        body, aliases = _combine_kernel, {}
    else:
        body, aliases = _combine_part_kernel, {len(args): 0}
        in_specs.append(pl.BlockSpec(memory_space=pl.ANY))
        args.append(out_prev)
    return pl.pallas_call(
        body,
        out_shape=jax.ShapeDtypeStruct((s, d), F32),
        grid=(n_tiles,),
        in_specs=in_specs,
        out_specs=pl.BlockSpec((tc, d), lambda t: (t + first_tile, 0)),
        input_output_aliases=aliases,
        compiler_params=_cparams(("arbitrary",)),
        name="moe_combine_ln2",
    )(*args)


def _tiles(s, d, ff, n_heads):
    return dict(
        proj_tm=min(1024, s), select_tq=min(512, s), conv_ts=min(512, s), outproj_tm=min(512, s),
        attn_heads=min(4, n_heads),
        route_tr=min(256, s), combine_tc=min(256, s),
        expert_tm=min(1280, max(ROUTE_PAD, (s * TOP_K // 16) // ROUTE_PAD * ROUTE_PAD)),
        expert_tf=min(512, ff))


def kernel(x, w_in, conv_w, conv_b, conv_ln_g, conv_ln_b, w_out, ln1_g, ln1_b,
           w_router, b_router, w_gate_up, b_gate_up, w_down, b_down, ln2_g, ln2_b):
    b, s, d = x.shape
    assert b == 1, "the kernels treat the sequence of the single batch element as the row axis"
    g = w_in.shape[1] // 5
    n_heads = g // HEAD_DIM
    nb = s // MOBA_BLOCK
    n_exp = w_router.shape[1]
    ff = w_down.shape[1]
    assert s % MOBA_BLOCK == 0 and nb % 8 == 0 and HEAD_DIM + nb + 3 <= AUG_DIM
    tl = _tiles(s, d, ff, n_heads)
    x2 = x.reshape(s, d)

    kgg = _in_projection(x2, w_in, g, tl["proj_tm"])
    qv_t, conv = _qv_projection_and_conv(x2, w_in, kgg, conv_w, conv_b, conv_ln_g, conv_ln_b, g,
                                         tl["conv_ts"])
    kmean = _block_means(kgg, g, nb)
    slopes = 2.0 ** (-(8.0 / n_heads) * jnp.arange(1, n_heads + 1, dtype=F32))
    q_aug, k_aug = _select(slopes, qv_t, kmean, kgg, n_heads, nb, tl["select_tq"])
    attn = _attention(q_aug, k_aug, qv_t, n_heads, nb, tl["attn_heads"])
    h1, h1_packed, logits_t = _out_projection(attn, conv, w_out, x2, ln1_g, ln1_b,
                                              w_router, b_router, tl["outproj_tm"])

    e_t, gate_t, rank_t, counts = _route(logits_t, tl["route_tr"])
    counts = counts[:, 0]
    padded = (counts + ROUTE_PAD - 1) // ROUTE_PAD * ROUTE_PAD
    pad_end = jnp.cumsum(padded)
    pad_start = (pad_end - padded).astype(I32)
    cap = s * TOP_K + n_exp * ROUTE_PAD
    e_flat = e_t.reshape(-1)
    rank_flat = rank_t.reshape(-1)
    xs = _sc_dispatch(h1_packed, e_flat, rank_flat, pad_start, cap, s)

    tm = tl["expert_tm"]
    n_items = cap // tm + n_exp
    per_e = (padded + tm - 1) // tm
    item_end = jnp.cumsum(per_e)
    item_ids = jnp.arange(n_items, dtype=I32)
    item_e = jnp.minimum(jnp.sum(item_ids[:, None] >= item_end[None, :], axis=1),
                         n_exp - 1).astype(I32)
    piece = item_ids - (item_end - per_e)[item_e]
    active = item_ids < item_end[-1]
    item_start = jnp.where(active, pad_start[item_e] + piece * tm, 0).astype(I32)
    item_rows = jnp.where(active, jnp.minimum(tm, padded[item_e] - piece * tm), 0).astype(I32)
    last_e = item_e[jnp.maximum(item_end[-1] - 1, 0)]
    item_e = jnp.where(active, item_e, last_e).astype(I32)

    y = _experts(item_e, item_start, item_rows, item_end[-1].astype(I32), xs,
                 w_gate_up, b_gate_up, w_down, b_down, tm, tl["expert_tf"])
    tc = tl["combine_tc"]
    s_part = s // COMBINE_PARTS
    gates = gate_t.T
    out = None
    for part in range(COMBINE_PARTS):
        cols = slice(part * s_part, (part + 1) * s_part)
        yg = _sc_combine_gather(y, e_t[:, cols].reshape(-1), rank_t[:, cols].reshape(-1), pad_start)
        out = _combine(yg.reshape(TOP_K, s_part, d // 2), gates, h1, ln2_g, ln2_b, tc,
                       part * (s_part // tc), out)
    return out.reshape(b, s, d)
```

```python
import functools

import jax
import jax.numpy as jnp
from jax import lax
from jax.experimental import pallas as pl
from jax.experimental.pallas import tpu as pltpu
from jax.experimental.pallas import tpu_sc as plsc

F32 = jnp.float32
BF16 = jnp.bfloat16
I32 = jnp.int32

HEAD_DIM = 128
MOBA_BLOCK = 256
MOBA_TOPK = 3
CONV_WIDTH = 31
CONV_HALO = 32
SUBLANES = 8
LANES = 128
TOP_K = 4
SWIGLU_ALPHA = 1.702
SWIGLU_LIMIT = 7.0
LN_EPS = 1e-5
DEPTH = 1
DEEPNORM_ALPHA = (2.0 * DEPTH) ** 0.25
ROUTE_PAD = 128
AUG_DIM = 256
MASK_NEG = -1e30
ONES_ROWS = 16
ATTN_UNIT = 4
COMBINE_PARTS = 2
VMEM_LIMIT = 56 * 1024 * 1024


def _cparams(sem):
    return pltpu.CompilerParams(dimension_semantics=sem, vmem_limit_bytes=VMEM_LIMIT)


def _proj_nn_kernel(x_ref, w_ref, o_ref, wb_ref):
    @pl.when(pl.program_id(1) == 0)
    def _():
        wb_ref[...] = w_ref[...].astype(BF16)

    o_ref[...] = jnp.dot(x_ref[...].astype(BF16), wb_ref[...],
                         preferred_element_type=F32).astype(o_ref.dtype)


def _in_projection(x2, w_in, g, tm):
    s, d = x2.shape
    return pl.pallas_call(
        _proj_nn_kernel,
        out_shape=jax.ShapeDtypeStruct((s, 3 * g), BF16),
        grid=(3, s // tm),
        in_specs=[pl.BlockSpec((tm, d), lambda n, m: (m, 0)),
                  pl.BlockSpec((d, g), lambda n, m: (0, jnp.where(n == 0, 1, n + 2)))],
        out_specs=pl.BlockSpec((tm, g), lambda n, m: (m, n)),
        scratch_shapes=[pltpu.VMEM((d, g), BF16)],
        compiler_params=_cparams(("arbitrary", "arbitrary")),
        name="proj_nn",
    )(x2, w_in)


def _kmean_kernel(k_ref, o_ref, *, blocks):
    for b in range(blocks):
        kb = k_ref[b * MOBA_BLOCK:(b + 1) * MOBA_BLOCK, :].astype(F32)
        o_ref[b:b + 1, :] = jnp.sum(kb, axis=0, keepdims=True) * (1.0 / MOBA_BLOCK)


def _block_means(kgg, g, nb):
    blocks = 8
    return pl.pallas_call(
        functools.partial(_kmean_kernel, blocks=blocks),
        out_shape=jax.ShapeDtypeStruct((nb, g), F32),
        grid=(nb // blocks,),
        in_specs=[pl.BlockSpec((blocks * MOBA_BLOCK, g), lambda i: (i, 0))],
        out_specs=pl.BlockSpec((blocks, g), lambda i: (i, 0)),
        compiler_params=_cparams(("arbitrary",)),
        name="moba_kmean",
    )(kgg)


def _select_kernel(slopes_ref, qt_ref, km_ref, k_ref, qa_ref, ka_ref, *, nb, tq, n_heads):
    t = pl.program_id(0)
    col = t * tq + lax.broadcasted_iota(I32, (nb, tq), 1)
    qblk = col // MOBA_BLOCK
    j = lax.broadcasted_iota(I32, (nb, tq), 0)
    neg_inf = jnp.float32(-jnp.inf)
    past = j < qblk
    own = j == qblk
    n_extra = AUG_DIM - HEAD_DIM - nb
    r = lax.broadcasted_iota(I32, (n_extra, tq), 0)
    qb = ((t * tq + lax.broadcasted_iota(I32, (n_extra, tq), 1)) // MOBA_BLOCK).astype(F32)
    extra_unit = jnp.where(r == 0, 1.0,
                           jnp.where(r == 1, float(MOBA_BLOCK),
                                     jnp.where(r == 2, -float(MOBA_BLOCK) * qb, 0.0)))
    n_aug = AUG_DIM - HEAD_DIM
    pos = t * tq + lax.broadcasted_iota(I32, (tq, n_aug), 0)
    kblk = pos // MOBA_BLOCK
    pib = pos % MOBA_BLOCK
    lane = lax.broadcasted_iota(I32, (tq, n_aug), 1)
    aug = jnp.where(lane < nb, (lane == kblk).astype(F32),
                    jnp.where(lane == nb, pib.astype(F32),
                              jnp.where(lane == nb + 1, kblk.astype(F32),
                                        jnp.where(lane == nb + 2, 1.0, 0.0)))).astype(BF16)
    scale = HEAD_DIM ** -0.5
    for h in range(n_heads):
        cols = slice(h * HEAD_DIM, (h + 1) * HEAD_DIM)
        q_t = qt_ref[cols, :].astype(F32)
        gate = jnp.dot(km_ref[:, cols], q_t, preferred_element_type=F32,
                       precision=lax.Precision.HIGHEST)
        gcur = jnp.where(past, gate, neg_inf)
        sel = own
        for _ in range(MOBA_TOPK):
            m = jnp.max(gcur, axis=0, keepdims=True)
            is_max = jnp.logical_and(gcur == m, m > neg_inf)
            idx = jnp.min(jnp.where(is_max, j, nb), axis=0, keepdims=True)
            pick = j == idx
            sel = jnp.logical_or(sel, pick)
            gcur = jnp.where(pick, neg_inf, gcur)
        qa_ref[h, 0:HEAD_DIM, :] = (q_t * scale).astype(BF16)
        qa_ref[h, HEAD_DIM:HEAD_DIM + nb, :] = jnp.where(sel, 0.0, MASK_NEG).astype(BF16)
        qa_ref[h, HEAD_DIM + nb:AUG_DIM, :] = (slopes_ref[h] * extra_unit).astype(BF16)
        ka_ref[h, :, 0:HEAD_DIM] = k_ref[:, cols]
        ka_ref[h, :, HEAD_DIM:AUG_DIM] = aug


def _select(slopes, qv_t, kmean, kgg, n_heads, nb, tq):
    s = kgg.shape[0]
    g = n_heads * HEAD_DIM
    return pl.pallas_call(
        functools.partial(_select_kernel, nb=nb, tq=tq, n_heads=n_heads),
        out_shape=(jax.ShapeDtypeStruct((n_heads, AUG_DIM, s), BF16),
                   jax.ShapeDtypeStruct((n_heads, s, AUG_DIM), BF16)),
        grid_spec=pltpu.PrefetchScalarGridSpec(
            num_scalar_prefetch=1,
            grid=(s // tq,),
            in_specs=[pl.BlockSpec((g, tq), lambda t, sl: (0, t)),
                      pl.BlockSpec((nb, g), lambda t, sl: (0, 0)),
                      pl.BlockSpec((tq, g), lambda t, sl: (t, 0))],
            out_specs=[pl.BlockSpec((n_heads, AUG_DIM, tq), lambda t, sl: (0, 0, t)),
                       pl.BlockSpec((n_heads, tq, AUG_DIM), lambda t, sl: (0, t, 0))]),
        compiler_params=_cparams(("arbitrary",)),
        name="moba_select",
    )(slopes, qv_t, kmean, kgg)


def _attn_kernel(qa_ref, ka_ref, vt_ref, o_ref, sa_ref, sb_ref, *, heads):
    i = pl.program_id(1)
    tq = MOBA_BLOCK
    unit_keys = ATTN_UNIT * MOBA_BLOCK
    neg_inf = jnp.float32(-jnp.inf)

    def scores_of(j, nkeys):
        off = pl.multiple_of(j * MOBA_BLOCK, MOBA_BLOCK)
        return [jnp.dot(ka_ref[hh, pl.ds(off, nkeys), :], qa_ref[hh],
                        preferred_element_type=F32) for hh in range(heads)]

    def scores_into(ref, unit):
        for hh, s_t in enumerate(scores_of(ATTN_UNIT * unit, unit_keys)):
            ref[hh] = s_t

    def update(j, scores, carries, diagonal, nkeys):
        off = pl.multiple_of(j * MOBA_BLOCK, MOBA_BLOCK)
        stats = []
        for hh in range(heads):
            m = carries[hh][0]
            s_t = scores[hh]
            if diagonal:
                key = lax.broadcasted_iota(I32, s_t.shape, 0) - (nkeys - MOBA_BLOCK)
                qry = lax.broadcasted_iota(I32, s_t.shape, 1)
                s_t = jnp.where(key <= qry, s_t, neg_inf)
            m_new = jnp.maximum(m, jnp.max(s_t, axis=0, keepdims=True))
            stats.append((m_new, jnp.exp(m - m_new), jnp.exp(s_t - m_new).astype(BF16)))
        out = []
        ones = jnp.ones((ONES_ROWS, nkeys), BF16)
        for hh in range(heads):
            m_new, alpha, p = stats[hh]
            vb = vt_ref[hh * HEAD_DIM:(hh + 1) * HEAD_DIM, pl.ds(off, nkeys)]
            vb1 = jnp.concatenate([vb, ones], axis=0)
            acc_new = alpha * carries[hh][1] + jnp.dot(vb1, p, preferred_element_type=F32)
            out.append((m_new, acc_new))
        return tuple(out)

    init = tuple((jnp.full((1, tq), neg_inf, F32), jnp.zeros((HEAD_DIM + ONES_ROWS, tq), F32))
                 for _ in range(heads))
    def tail_update(r):
        nkeys = (r + 1) * MOBA_BLOCK

        def branch(cs):
            tail_scores = scores_of(i - r, nkeys)
            scores_into(sa_ref, 0)
            return update(i - r, tail_scores, cs, True, nkeys)
        return branch

    carries = lax.switch(i % ATTN_UNIT, [tail_update(r) for r in range(ATTN_UNIT)], init)

    n_units = i // ATTN_UNIT
    last_unit = jnp.maximum(n_units - 1, 0)

    def from_ref(ref):
        return [ref[hh] for hh in range(heads)]

    def two_units(k, cs):
        scores_into(sb_ref, 2 * k + 1)
        cs = update(2 * ATTN_UNIT * k, from_ref(sa_ref), cs, False, unit_keys)
        scores_into(sa_ref, jnp.minimum(2 * k + 2, last_unit))
        return update(2 * ATTN_UNIT * k + ATTN_UNIT, from_ref(sb_ref), cs, False, unit_keys)

    carries = lax.fori_loop(0, n_units // 2, two_units, carries)
    carries = lax.cond(
        n_units % 2 == 1,
        lambda cs: update(ATTN_UNIT * last_unit, from_ref(sa_ref), cs, False, unit_keys),
        lambda cs: cs, carries)
    for hh in range(heads):
        acc = carries[hh][1]
        out_t = acc[0:HEAD_DIM, :] / acc[HEAD_DIM:HEAD_DIM + 1, :]
        o_ref[:, hh * HEAD_DIM:(hh + 1) * HEAD_DIM] = out_t.T.astype(o_ref.dtype)


def _attention(q_aug, k_aug, qv_t, n_heads, nb, heads):
    s = k_aug.shape[1]
    groups = n_heads // heads
    once = pl.Buffered(1)
    return pl.pallas_call(
        functools.partial(_attn_kernel, heads=heads),
        out_shape=jax.ShapeDtypeStruct((s, n_heads * HEAD_DIM), BF16),
        grid=(groups, nb),
        in_specs=[pl.BlockSpec((heads, AUG_DIM, MOBA_BLOCK), lambda hg, i: (hg, 0, i)),
                  pl.BlockSpec((heads, s, AUG_DIM), lambda hg, i: (hg, 0, 0), pipeline_mode=once),
                  pl.BlockSpec((heads * HEAD_DIM, s), lambda hg, i: (groups + hg, 0),
                               pipeline_mode=once)],
        out_specs=pl.BlockSpec((MOBA_BLOCK, heads * HEAD_DIM), lambda hg, i: (i, hg)),
        scratch_shapes=[pltpu.VMEM((heads, ATTN_UNIT * MOBA_BLOCK, MOBA_BLOCK), F32),
                        pltpu.VMEM((heads, ATTN_UNIT * MOBA_BLOCK, MOBA_BLOCK), F32)],
        compiler_params=_cparams(("arbitrary", "arbitrary")),
        name="moba_attention",
    )(q_aug, k_aug, qv_t)


def _qv_conv_kernel(wt_ref, x_ref, ga_ref, gb_ref, w_ref, b_ref, g_ref, beta_ref, qv_ref, o_ref,
                    u_ref, ush_ref, *, ts, sub):
    t = pl.program_id(0)

    @pl.when(t == 0)
    def _():
        u_ref[0:CONV_HALO, :] = jnp.zeros((CONV_HALO, u_ref.shape[1]), F32)

    @pl.when(t > 0)
    def _():
        u_ref[0:CONV_HALO, :] = u_ref[ts:ts + CONV_HALO, :]

    x_b = x_ref[...].astype(BF16)
    n_slabs = min(8, ts // sub)
    groups_per_slab = (ts // sub) // n_slabs
    slab = wt_ref.shape[0] // n_slabs

    def project(k, anchor):
        rows = slice(k * slab, (k + 1) * slab)
        w_k = wt_ref[rows, :] if anchor is None else wt_ref[rows, :] + anchor
        qv_ref[rows, :] = lax.dot_general(w_k, x_b, (((1,), (1,)), ((), ())),
                                          preferred_element_type=F32).astype(qv_ref.dtype)

    gb = gb_ref[...].astype(F32)
    u_ref[CONV_HALO:CONV_HALO + ts, :] = ga_ref[...].astype(F32) * jax.nn.sigmoid(gb)
    for b in range(SUBLANES):
        ush_ref[b, 0:ts + CONV_HALO - b, :] = u_ref[b:ts + CONV_HALO, :]
    first = CONV_HALO - (CONV_WIDTH - 1)
    anchor = None
    for r0 in range(0, ts, sub):
        if (r0 // sub) % groups_per_slab == 0:
            project(r0 // sub // groups_per_slab, anchor)
        acc = jnp.broadcast_to(b_ref[...], (sub, u_ref.shape[1]))
        for tap in range(CONV_WIDTH):
            off = first + tap
            aligned = r0 + off - off % SUBLANES
            acc = acc + w_ref[tap:tap + 1, :] * ush_ref[off % SUBLANES, aligned:aligned + sub, :]
        mu = jnp.mean(acc, axis=-1, keepdims=True)
        cen = acc - mu
        var = jnp.mean(cen * cen, axis=-1, keepdims=True)
        y = cen * lax.rsqrt(var + LN_EPS) * g_ref[...] + beta_ref[...]
        o_ref[r0:r0 + sub, :] = (y * jax.nn.sigmoid(y)).astype(o_ref.dtype)
        bits = pltpu.bitcast(y[0:1, 0:1], I32)
        zero = lax.shift_right_logical(lax.shift_right_logical(bits, jnp.int32(31)), jnp.int32(1))
        anchor = zero.astype(BF16)


def _transpose_cast_kernel(w_ref, o_ref):
    o_ref[...] = w_ref[...].T.astype(o_ref.dtype)


def _qv_weights_transposed(w_in, g):
    d = w_in.shape[0]
    cols = min(256, g)
    per_group = g // cols
    return pl.pallas_call(
        _transpose_cast_kernel,
        out_shape=jax.ShapeDtypeStruct((2 * g, d), BF16),
        grid=(2 * per_group,),
        in_specs=[pl.BlockSpec((d, cols), lambda c: (0, c + jnp.where(c >= per_group, per_group, 0)))],
        out_specs=pl.BlockSpec((cols, d), lambda c: (c, 0)),
        compiler_params=_cparams(("arbitrary",)),
        name="qv_weight_transpose",
    )(w_in)


def _qv_projection_and_conv(x2, w_in, kgg, conv_w, conv_b, conv_ln_g, conv_ln_b, g, ts):
    s, d = x2.shape
    row = lambda v: v.reshape(1, g).astype(F32)
    w_qv_t = _qv_weights_transposed(w_in, g)
    once = pl.Buffered(1)
    return pl.pallas_call(
        functools.partial(_qv_conv_kernel, ts=ts, sub=32),
        out_shape=(jax.ShapeDtypeStruct((2 * g, s), BF16), jax.ShapeDtypeStruct((s, g), BF16)),
        grid=(s // ts,),
        in_specs=[pl.BlockSpec((2 * g, d), lambda t: (0, 0), pipeline_mode=once),
                  pl.BlockSpec((ts, d), lambda t: (t, 0)),
                  pl.BlockSpec((ts, g), lambda t: (t, 1)),
                  pl.BlockSpec((ts, g), lambda t: (t, 2)),
                  pl.BlockSpec((CONV_WIDTH, g), lambda t: (0, 0)),
                  pl.BlockSpec((1, g), lambda t: (0, 0)),
                  pl.BlockSpec((1, g), lambda t: (0, 0)),
                  pl.BlockSpec((1, g), lambda t: (0, 0))],
        out_specs=[pl.BlockSpec((2 * g, ts), lambda t: (0, t)),
                   pl.BlockSpec((ts, g), lambda t: (t, 0))],
        scratch_shapes=[pltpu.VMEM((ts + CONV_HALO, g), F32),
                        pltpu.VMEM((SUBLANES, ts + CONV_HALO, g), F32)],
        compiler_params=_cparams(("arbitrary",)),
        name="qv_proj_conformer_conv",
    )(w_qv_t, x2, kgg, kgg, conv_w, row(conv_b), row(conv_ln_g), row(conv_ln_b))


def _layer_norm_rows(z, gain, bias):
    mu = jnp.mean(z, axis=-1, keepdims=True)
    cen = z - mu
    var = jnp.mean(cen * cen, axis=-1, keepdims=True)
    return cen * lax.rsqrt(var + LN_EPS) * gain + bias


HI_HALF = -65536


def _pack_halves(lo, hi):
    lo_bits = pltpu.bitcast(lo.astype(BF16).astype(F32), I32)
    hi_bits = pltpu.bitcast(hi.astype(BF16).astype(F32), I32)
    return lax.shift_right_logical(lo_bits, jnp.int32(16)) | (hi_bits & jnp.int32(HI_HALF))


def _unpack_halves(words):
    lo = pltpu.bitcast(lax.shift_left(words, jnp.int32(16)), F32)
    hi = pltpu.bitcast(words & jnp.int32(HI_HALF), F32)
    return lo, hi


def _outproj_kernel(attn_ref, conv_ref, wo_ref, x_ref, g_ref, b_ref, wr_ref, br_ref,
                    h_ref, hp_ref, lg_ref, *, g):
    tm = x_ref.shape[0]
    sub = min(tm, 256)
    half = x_ref.shape[1] // 2
    n_exp = lg_ref.shape[0]
    wr = wr_ref[...]
    w_hi = wr.astype(BF16)
    w_lo = (wr - w_hi.astype(F32)).astype(BF16)
    mixes = []
    for r0 in range(0, tm, sub):
        mix = jnp.dot(attn_ref[r0:r0 + sub, :], wo_ref[0:g, :], preferred_element_type=F32)
        mixes.append(mix + jnp.dot(conv_ref[r0:r0 + sub, :], wo_ref[g:2 * g, :],
                                   preferred_element_type=F32))
    for k, r0 in enumerate(range(0, tm, sub)):
        h1 = _layer_norm_rows(DEEPNORM_ALPHA * x_ref[r0:r0 + sub, :] + mixes[k],
                              g_ref[...], b_ref[...])
        h_ref[r0:r0 + sub, :] = h1
        hp_ref[r0:r0 + sub, :] = _pack_halves(h1[:, 0:half], h1[:, half:])
        h_hi = h1.astype(BF16)
        h_lo = (h1 - h_hi.astype(F32)).astype(BF16)
        lg = jnp.dot(h_hi, w_hi, preferred_element_type=F32)
        lg = lg + jnp.dot(h_lo, w_hi, preferred_element_type=F32)
        lg = lg + jnp.dot(h_hi, w_lo, preferred_element_type=F32)
        lg_ref[:, r0:r0 + sub] = lg.T[0:n_exp, :] + br_ref[...]


def _out_projection(attn, conv, w_out, x2, ln_g, ln_b, w_router, b_router, tm):
    s, d = x2.shape
    g = attn.shape[1]
    e = w_router.shape[1]
    return pl.pallas_call(
        functools.partial(_outproj_kernel, g=g),
        out_shape=(jax.ShapeDtypeStruct((s, d), F32), jax.ShapeDtypeStruct((s, d // 2), I32),
                   jax.ShapeDtypeStruct((e, s), F32)),
        grid=(s // tm,),
        in_specs=[pl.BlockSpec((tm, g), lambda m: (m, 0)),
                  pl.BlockSpec((tm, g), lambda m: (m, 0)),
                  pl.BlockSpec((2 * g, d), lambda m: (0, 0), pipeline_mode=pl.Buffered(1)),
                  pl.BlockSpec((tm, d), lambda m: (m, 0)),
                  pl.BlockSpec((1, d), lambda m: (0, 0)),
                  pl.BlockSpec((1, d), lambda m: (0, 0)),
                  pl.BlockSpec((d, LANES), lambda m: (0, 0)),
                  pl.BlockSpec((e, 1), lambda m: (0, 0))],
        out_specs=[pl.BlockSpec((tm, d), lambda m: (m, 0)),
                   pl.BlockSpec((tm, d // 2), lambda m: (m, 0)),
                   pl.BlockSpec((e, tm), lambda m: (0, m))],
        compiler_params=_cparams(("arbitrary",)),
        name="outproj_ln1_router",
    )(attn, conv, w_out.astype(BF16), x2, ln_g.reshape(1, d), ln_b.reshape(1, d),
      jnp.pad(w_router, ((0, 0), (0, LANES - e))), b_router.reshape(e, 1))


def _route_kernel(lg_ref, e_ref, gate_ref, rank_ref, cnt_ref, carry_ref, *, n_exp, tr):
    t = pl.program_id(0)

    @pl.when(t == 0)
    def _():
        carry_ref[...] = jnp.zeros_like(carry_ref)

    neg_inf = jnp.float32(-jnp.inf)
    cur = lg_ref[...]
    j = lax.broadcasted_iota(I32, (n_exp, tr), 0)
    vals, picks = [], []
    for r in range(TOP_K):
        m = jnp.max(cur, axis=0, keepdims=True)
        idx = jnp.min(jnp.where(cur == m, j, n_exp), axis=0, keepdims=True)
        pick = j == idx
        e_ref[r:r + 1, :] = idx
        vals.append(m)
        picks.append(pick)
        cur = jnp.where(pick, neg_inf, cur)
    exps = [jnp.exp(v - vals[0]) for v in vals]
    den = exps[0]
    for r in range(1, TOP_K):
        den = den + exps[r]
    for r in range(TOP_K):
        gate_ref[r:r + 1, :] = exps[r] / den

    chosen = picks[0].astype(F32)
    for r in range(1, TOP_K):
        chosen = chosen + picks[r].astype(F32)
    a = lax.broadcasted_iota(I32, (tr, tr), 0)
    b = lax.broadcasted_iota(I32, (tr, tr), 1)
    upper = (a < b).astype(BF16)
    excl = jnp.dot(chosen.astype(BF16), upper, preferred_element_type=F32)
    base = carry_ref[:, 0:1]
    rank = excl + base
    for r in range(TOP_K):
        rank_ref[r:r + 1, :] = jnp.sum(jnp.where(picks[r], rank, 0.0), axis=0,
                                       keepdims=True).astype(I32)
    total = base + jnp.sum(chosen, axis=1, keepdims=True)
    carry_ref[...] = jnp.broadcast_to(total, carry_ref.shape)
    cnt_ref[...] = jnp.broadcast_to(total, cnt_ref.shape).astype(I32)


def _route(logits_t, tr):
    n_exp, s = logits_t.shape
    return pl.pallas_call(
        functools.partial(_route_kernel, n_exp=n_exp, tr=tr),
        out_shape=(jax.ShapeDtypeStruct((TOP_K, s), I32), jax.ShapeDtypeStruct((TOP_K, s), F32),
                   jax.ShapeDtypeStruct((TOP_K, s), I32), jax.ShapeDtypeStruct((n_exp, 128), I32)),
        grid=(s // tr,),
        in_specs=[pl.BlockSpec((n_exp, tr), lambda t: (0, t))],
        out_specs=[pl.BlockSpec((TOP_K, tr), lambda t: (0, t)),
                   pl.BlockSpec((TOP_K, tr), lambda t: (0, t)),
                   pl.BlockSpec((TOP_K, tr), lambda t: (0, t)),
                   pl.BlockSpec((n_exp, 128), lambda t: (0, 0))],
        scratch_shapes=[pltpu.VMEM((n_exp, 128), F32)],
        compiler_params=_cparams(("arbitrary",)),
        name="moe_route",
    )(logits_t)


SC_CORES = 2
SC_SUBCORES = 16
SC_LANES = 16
SC_GATHER_ROWS = 32
SC_SCAN = 8192


_SC_PARAMS = pltpu.CompilerParams(needs_layout_passes=False)


def _sc_mesh():
    return plsc.VectorSubcoreMesh(core_axis_name="c", subcore_axis_name="s",
                                  num_cores=SC_CORES, num_subcores=SC_SUBCORES)


def _sc_worker():
    return lax.axis_index("s") * SC_CORES + lax.axis_index("c")


def _sc_gather_loop(table_hbm, idx_all, out_hbm, bufs, sems, base, n_chunks):
    chunk = SC_GATHER_ROWS

    def gather(ci, b):
        return pltpu.make_async_copy(
            table_hbm.at[idx_all.at[pl.ds(ci * chunk, chunk)]], bufs[b], sems[b])

    gather(0, 0).start()

    @pl.loop(0, n_chunks, step=2)
    def _(ci):
        gather(ci + 1, 1).start()
        gather(ci, 0).wait()
        pltpu.sync_copy(bufs[0], out_hbm.at[pl.ds(base + ci * chunk, chunk)])

        @pl.when(ci + 2 < n_chunks)
        def _():
            gather(ci + 2, 0).start()

        gather(ci + 1, 1).wait()
        pltpu.sync_copy(bufs[1], out_hbm.at[pl.ds(base + (ci + 1) * chunk, chunk)])


def _sc_dest(ps_v, e_buf, r_buf, v):
    e_vec = e_buf[pl.ds(v * SC_LANES, SC_LANES)]
    return plsc.load_gather(ps_v, [e_vec]) + r_buf[pl.ds(v * SC_LANES, SC_LANES)]


def _sc_dispatch(table, e_flat, rank_flat, pad_start, cap, s):
    n_assign, width = e_flat.shape[0], table.shape[1]
    n_workers = SC_CORES * SC_SUBCORES
    per_worker = cap // n_workers
    chunk = SC_GATHER_ROWS
    assert cap % (n_workers * 2 * chunk) == 0 and n_assign % SC_SCAN == 0
    assert per_worker % SC_LANES == 0 and s % SC_SCAN == 0

    @functools.partial(
        pl.kernel, mesh=_sc_mesh(), out_type=jax.ShapeDtypeStruct((cap, width), table.dtype),
        scratch_types=[pltpu.VMEM((per_worker,), I32), pltpu.VMEM((pad_start.shape[0],), I32),
                       pltpu.VMEM((SC_SCAN,), I32), pltpu.VMEM((SC_SCAN,), I32),
                       pltpu.VMEM((chunk, width), table.dtype),
                       pltpu.VMEM((chunk, width), table.dtype),
                       pltpu.SemaphoreType.DMA, pltpu.SemaphoreType.DMA],
        compiler_params=_SC_PARAMS, name="sc_dispatch")
    def dispatch(table_hbm, e_hbm, r_hbm, ps_hbm, out_hbm, idx_all, ps_v, e_buf, r_buf,
                 buf0, buf1, sem0, sem1):
        base = _sc_worker() * per_worker
        lane = lax.iota(I32, SC_LANES)
        pltpu.sync_copy(ps_hbm, ps_v)

        @pl.loop(0, per_worker // SC_LANES)
        def _(k):
            idx_all[pl.ds(k * SC_LANES, SC_LANES)] = lax.rem(base + k * SC_LANES + lane, s)

        @pl.loop(0, n_assign // SC_SCAN)
        def _(c):
            pltpu.sync_copy(e_hbm.at[pl.ds(c * SC_SCAN, SC_SCAN)], e_buf)
            pltpu.sync_copy(r_hbm.at[pl.ds(c * SC_SCAN, SC_SCAN)], r_buf)
            tok0 = lax.rem(c * SC_SCAN, s)

            @pl.loop(0, SC_SCAN // SC_LANES)
            def _(v):
                loc = _sc_dest(ps_v, e_buf, r_buf, v) - base
                mine = jnp.logical_and(loc >= 0, loc < per_worker)
                tok = tok0 + v * SC_LANES + lane
                plsc.store_scatter(idx_all, [jnp.where(mine, loc, 0)], tok, mask=mine)

        _sc_gather_loop(table_hbm, idx_all, out_hbm, (buf0, buf1), (sem0, sem1), base,
                        per_worker // chunk)

    return dispatch(table, e_flat, rank_flat, pad_start)


def _sc_combine_gather(y, e_flat, rank_flat, pad_start):
    n_assign, width = e_flat.shape[0], y.shape[1]
    n_workers = SC_CORES * SC_SUBCORES
    per_worker = n_assign // n_workers
    chunk = SC_GATHER_ROWS
    assert n_assign % (n_workers * 2 * chunk) == 0

    @functools.partial(
        pl.kernel, mesh=_sc_mesh(), out_type=jax.ShapeDtypeStruct((n_assign, width), y.dtype),
        scratch_types=[pltpu.VMEM((per_worker,), I32), pltpu.VMEM((pad_start.shape[0],), I32),
                       pltpu.VMEM((per_worker,), I32), pltpu.VMEM((per_worker,), I32),
                       pltpu.VMEM((chunk, width), y.dtype), pltpu.VMEM((chunk, width), y.dtype),
                       pltpu.SemaphoreType.DMA, pltpu.SemaphoreType.DMA],
        compiler_params=_SC_PARAMS, name="sc_combine_gather")
    def combine(y_hbm, e_hbm, r_hbm, ps_hbm, out_hbm, idx_all, ps_v, e_buf, r_buf,
                buf0, buf1, sem0, sem1):
        base = _sc_worker() * per_worker
        pltpu.sync_copy(ps_hbm, ps_v)
        pltpu.sync_copy(e_hbm.at[pl.ds(base, per_worker)], e_buf)
        pltpu.sync_copy(r_hbm.at[pl.ds(base, per_worker)], r_buf)

        @pl.loop(0, per_worker // SC_LANES)
        def _(v):
            idx_all[pl.ds(v * SC_LANES, SC_LANES)] = _sc_dest(ps_v, e_buf, r_buf, v)

        _sc_gather_loop(y_hbm, idx_all, out_hbm, (buf0, buf1), (sem0, sem1), base,
                        per_worker // chunk)

    return combine(y, e_flat, rank_flat, pad_start)


def _expert_kernel(item_e, item_start, item_rows, xs_ref, wg_ref, wu_ref, wd_ref,
                   bg_ref, bu_ref, bd_ref, y_ref, xin_ref, acc_ref,
                   sem_x, sem_y, *, nf, n_items, big):
    w = pl.program_id(0)
    f = pl.program_id(1)
    rows = item_rows[w]
    slot = w % 2
    half = xin_ref.shape[2]
    has_next = jnp.logical_and(w + 1 < n_items, item_rows[jnp.minimum(w + 1, n_items - 1)] > 0)

    def x_copy(item, sl, c):
        src = pl.multiple_of(item_start[item] + c * ROUTE_PAD, ROUTE_PAD)
        dst = pl.multiple_of(c * ROUTE_PAD, ROUTE_PAD)
        return pltpu.make_async_copy(xs_ref.at[pl.ds(src, ROUTE_PAD)],
                                     xin_ref.at[sl, pl.ds(dst, ROUTE_PAD)], sem_x.at[sl])

    def y_copy(item, sl, c):
        src = pl.multiple_of(c * ROUTE_PAD, ROUTE_PAD)
        dst = pl.multiple_of(item_start[item] + c * ROUTE_PAD, ROUTE_PAD)
        return pltpu.make_async_copy(xin_ref.at[sl, pl.ds(src, ROUTE_PAD)],
                                     y_ref.at[pl.ds(dst, ROUTE_PAD)], sem_y)

    def for_chunks(item, fn):
        def body(c, carry):
            fn(c)
            return carry
        lax.fori_loop(0, item_rows[item] // ROUTE_PAD, body, 0)

    def compute(r0, size, first):
        lo, hi = _unpack_halves(xin_ref[slot, pl.ds(r0, size), :])
        lo = lo.astype(BF16)
        hi = hi.astype(BF16)

        def x_dot(w_ref):
            return (jnp.dot(lo, w_ref[0, 0:half, :].astype(BF16), preferred_element_type=F32)
                    + jnp.dot(hi, w_ref[0, half:2 * half, :].astype(BF16),
                              preferred_element_type=F32))

        gate = x_dot(wg_ref) + bg_ref[0]
        up = x_dot(wu_ref) + bu_ref[0]
        gate = jnp.minimum(gate, SWIGLU_LIMIT)
        up = jnp.clip(up, -SWIGLU_LIMIT, SWIGLU_LIMIT)
        act = (up + 1.0) * (gate * jax.nn.sigmoid(SWIGLU_ALPHA * gate))
        part = jnp.dot(act.astype(BF16), wd_ref[0].astype(BF16), preferred_element_type=F32)
        if first:
            acc_ref[pl.ds(r0, size), :] = part + bd_ref[0]
        else:
            acc_ref[pl.ds(r0, size), :] += part

    def compute_all(first):
        n_big = rows // big
        rem = rows - n_big * big
        merge = jnp.logical_and(n_big >= 1, jnp.logical_and(rem > 0, rem <= big // 2))
        n_loop = n_big - jnp.where(merge, 1, 0)

        def body(c, carry):
            compute(pl.multiple_of(c * big, big), big, first)
            return carry
        lax.fori_loop(0, n_loop, body, 0)
        for extra in range(ROUTE_PAD, big // 2 + 1, ROUTE_PAD):
            @pl.when(jnp.logical_and(merge, rem == extra))
            def _(extra=extra):
                compute(pl.multiple_of(n_loop * big, big), big + extra, first)

        @pl.when(jnp.logical_not(merge))
        def _():
            done = n_big * big
            size = big // 2
            while size >= ROUTE_PAD:
                take = ((rows - done) & size) != 0
                here = done

                @pl.when(take)
                def _(here=here, size=size):
                    compute(pl.multiple_of(here, ROUTE_PAD), size, first)
                done = done + jnp.where(take, size, 0)
                size //= 2

    @pl.when(rows > 0)
    def _():
        @pl.when(f == 0)
        def _():
            @pl.when(w == 0)
            def _():
                for_chunks(w, lambda c: x_copy(w, slot, c).start())

            for_chunks(w, lambda c: x_copy(w, slot, c).wait())

            @pl.when(w > 0)
            def _():
                for_chunks(w - 1, lambda c: y_copy(w - 1, 1 - slot, c).wait())

            @pl.when(has_next)
            def _():
                for_chunks(w + 1, lambda c: x_copy(w + 1, 1 - slot, c).start())

            compute_all(True)

        @pl.when(f > 0)
        def _():
            compute_all(False)

        @pl.when(f == nf - 1)
        def _():
            def pack(c):
                r0 = pl.multiple_of(c * ROUTE_PAD, ROUTE_PAD)
                xin_ref[slot, pl.ds(r0, ROUTE_PAD), :] = _pack_halves(
                    acc_ref[pl.ds(r0, ROUTE_PAD), 0:half],
                    acc_ref[pl.ds(r0, ROUTE_PAD), half:2 * half])
                y_copy(w, slot, c).start()
            for_chunks(w, pack)

            @pl.when(jnp.logical_not(has_next))
            def _():
                for_chunks(w, lambda c: y_copy(w, slot, c).wait())


def _experts(item_e, item_start, item_rows, n_active, xs, w_gate_up, b_gate_up, w_down, b_down,
             tm, tf):
    cap, half = xs.shape
    d = 2 * half
    n_exp, _, two_f = w_gate_up.shape
    ff = two_f // 2
    nf = ff // tf
    n_items = item_e.shape[0]
    big = min(512, tm)
    assert big % ROUTE_PAD == 0 and (big & (big - 1)) == 0

    def fidx(w, f, rows):
        return jnp.where(rows[w] > 0, f, nf - 1)

    return pl.pallas_call(
        functools.partial(_expert_kernel, nf=nf, n_items=n_items, big=big),
        out_shape=jax.ShapeDtypeStruct((cap, half), I32),
        grid_spec=pltpu.PrefetchScalarGridSpec(
            num_scalar_prefetch=3,
            grid=(n_active, nf),
            in_specs=[
                pl.BlockSpec(memory_space=pl.ANY),
                pl.BlockSpec((1, d, tf), lambda w, f, ie, ist, ir: (ie[w], 0, fidx(w, f, ir))),
                pl.BlockSpec((1, d, tf), lambda w, f, ie, ist, ir: (ie[w], 0, nf + fidx(w, f, ir))),
                pl.BlockSpec((1, tf, d), lambda w, f, ie, ist, ir: (ie[w], fidx(w, f, ir), 0)),
                pl.BlockSpec((1, 1, tf), lambda w, f, ie, ist, ir: (ie[w], 0, fidx(w, f, ir))),
                pl.BlockSpec((1, 1, tf), lambda w, f, ie, ist, ir: (ie[w], 0, nf + fidx(w, f, ir))),
                pl.BlockSpec((1, 1, d), lambda w, f, ie, ist, ir: (ie[w], 0, 0)),
            ],
            out_specs=pl.BlockSpec(memory_space=pl.ANY),
            scratch_shapes=[pltpu.VMEM((2, tm, half), I32), pltpu.VMEM((tm, d), F32),
                            pltpu.SemaphoreType.DMA((2,)), pltpu.SemaphoreType.DMA]),
        compiler_params=_cparams(("arbitrary", "arbitrary")),
        name="moe_experts",
    )(item_e, item_start, item_rows, xs, w_gate_up, w_gate_up, w_down,
      b_gate_up.reshape(n_exp, 1, two_f), b_gate_up.reshape(n_exp, 1, two_f),
      b_down.reshape(n_exp, 1, d))


def _combine_kernel(yg_ref, gate_ref, h_ref, g_ref, b_ref, o_ref):
    gates = gate_ref[...]
    half = yg_ref.shape[2]
    z_lo = DEEPNORM_ALPHA * h_ref[:, 0:half]
    z_hi = DEEPNORM_ALPHA * h_ref[:, half:2 * half]
    for r in range(TOP_K):
        lo, hi = _unpack_halves(yg_ref[r])
        z_lo = z_lo + gates[:, r:r + 1] * lo
        z_hi = z_hi + gates[:, r:r + 1] * hi
    inv_d = 1.0 / (2 * half)
    mu = (jnp.sum(z_lo, axis=-1, keepdims=True) + jnp.sum(z_hi, axis=-1, keepdims=True)) * inv_d
    c_lo = z_lo - mu
    c_hi = z_hi - mu
    var = (jnp.sum(c_lo * c_lo, axis=-1, keepdims=True)
           + jnp.sum(c_hi * c_hi, axis=-1, keepdims=True)) * inv_d
    rstd = lax.rsqrt(var + LN_EPS)
    o_ref[:, 0:half] = c_lo * rstd * g_ref[:, 0:half] + b_ref[:, 0:half]
    o_ref[:, half:2 * half] = c_hi * rstd * g_ref[:, half:2 * half] + b_ref[:, half:2 * half]


def _combine_part_kernel(yg_ref, gate_ref, h_ref, g_ref, b_ref, prev_ref, o_ref):
    del prev_ref
    _combine_kernel(yg_ref, gate_ref, h_ref, g_ref, b_ref, o_ref)


def _combine(yg, gates, h1, ln_g, ln_b, tc, first_tile, out_prev):
    s, d = h1.shape
    n_tiles = yg.shape[1] // tc
    in_specs = [pl.BlockSpec((TOP_K, tc, d // 2), lambda t: (0, t, 0)),
                pl.BlockSpec((tc, TOP_K), lambda t: (t + first_tile, 0)),
                pl.BlockSpec((tc, d), lambda t: (t + first_tile, 0)),
                pl.BlockSpec((1, d), lambda t: (0, 0)),
                pl.BlockSpec((1, d), lambda t: (0, 0))]
    args = [yg, gates, h1, ln_g.reshape(1, d), ln_b.reshape(1, d)]
    if out_prev is None:
        body, aliases = _combine_kernel, {}
    else:
        body, aliases = _combine_part_kernel, {len(args): 0}
        in_specs.append(pl.BlockSpec(memory_space=pl.ANY))
        args.append(out_prev)
    return pl.pallas_call(
        body,
        out_shape=jax.ShapeDtypeStruct((s, d), F32),
        grid=(n_tiles,),
        in_specs=in_specs,
        out_specs=pl.BlockSpec((tc, d), lambda t: (t + first_tile, 0)),
        input_output_aliases=aliases,
        compiler_params=_cparams(("arbitrary",)),
        name="moe_combine_ln2",
    )(*args)


def _tiles(s, d, ff, n_heads):
    return dict(
        proj_tm=min(1024, s), select_tq=min(512, s), conv_ts=min(512, s), outproj_tm=min(512, s),
        attn_heads=min(4, n_heads),
        route_tr=min(256, s), combine_tc=min(256, s),
        expert_tm=min(1280, max(ROUTE_PAD, (s * TOP_K // 16) // ROUTE_PAD * ROUTE_PAD)),
        expert_tf=min(512, ff))


def kernel(x, w_in, conv_w, conv_b, conv_ln_g, conv_ln_b, w_out, ln1_g, ln1_b,
           w_router, b_router, w_gate_up, b_gate_up, w_down, b_down, ln2_g, ln2_b):
    b, s, d = x.shape
    assert b == 1, "the kernels treat the sequence of the single batch element as the row axis"
    g = w_in.shape[1] // 5
    n_heads = g // HEAD_DIM
    nb = s // MOBA_BLOCK
    n_exp = w_router.shape[1]
    ff = w_down.shape[1]
    assert s % MOBA_BLOCK == 0 and nb % 8 == 0 and HEAD_DIM + nb + 3 <= AUG_DIM
    tl = _tiles(s, d, ff, n_heads)
    x2 = x.reshape(s, d)

    kgg = _in_projection(x2, w_in, g, tl["proj_tm"])
    qv_t, conv = _qv_projection_and_conv(x2, w_in, kgg, conv_w, conv_b, conv_ln_g, conv_ln_b, g,
                                         tl["conv_ts"])
    kmean = _block_means(kgg, g, nb)
    slopes = 2.0 ** (-(8.0 / n_heads) * jnp.arange(1, n_heads + 1, dtype=F32))
    q_aug, k_aug = _select(slopes, qv_t, kmean, kgg, n_heads, nb, tl["select_tq"])
    attn = _attention(q_aug, k_aug, qv_t, n_heads, nb, tl["attn_heads"])
    h1, h1_packed, logits_t = _out_projection(attn, conv, w_out, x2, ln1_g, ln1_b,
                                              w_router, b_router, tl["outproj_tm"])

    e_t, gate_t, rank_t, counts = _route(logits_t, tl["route_tr"])
    counts = counts[:, 0]
    padded = (counts + ROUTE_PAD - 1) // ROUTE_PAD * ROUTE_PAD
    pad_end = jnp.cumsum(padded)
    pad_start = (pad_end - padded).astype(I32)
    cap = s * TOP_K + n_exp * ROUTE_PAD
    e_flat = e_t.reshape(-1)
    rank_flat = rank_t.reshape(-1)
    xs = _sc_dispatch(h1_packed, e_flat, rank_flat, pad_start, cap, s)

    tm = tl["expert_tm"]
    n_items = cap // tm + n_exp
    per_e = (padded + tm - 1) // tm
    item_end = jnp.cumsum(per_e)
    item_ids = jnp.arange(n_items, dtype=I32)
    item_e = jnp.minimum(jnp.sum(item_ids[:, None] >= item_end[None, :], axis=1),
                         n_exp - 1).astype(I32)
    piece = item_ids - (item_end - per_e)[item_e]
    active = item_ids < item_end[-1]
    item_start = jnp.where(active, pad_start[item_e] + piece * tm, 0).astype(I32)
    item_rows = jnp.where(active, jnp.minimum(tm, padded[item_e] - piece * tm), 0).astype(I32)
    last_e = item_e[jnp.maximum(item_end[-1] - 1, 0)]
    item_e = jnp.where(active, item_e, last_e).astype(I32)

    y = _experts(item_e, item_start, item_rows, item_end[-1].astype(I32), xs,
                 w_gate_up, b_gate_up, w_down, b_down, tm, tl["expert_tf"])
    tc = tl["combine_tc"]
    s_part = s // COMBINE_PARTS
    gates = gate_t.T
    out = None
    for part in range(COMBINE_PARTS):
        cols = slice(part * s_part, (part + 1) * s_part)
        yg = _sc_combine_gather(y, e_t[:, cols].reshape(-1), rank_t[:, cols].reshape(-1), pad_start)
        out = _combine(yg.reshape(TOP_K, s_part, d // 2), gates, h1, ln2_g, ln2_b, tc,
                       part * (s_part // tc), out)
    return out.reshape(b, s, d)
```

```python
import functools

import jax
import jax.numpy as jnp
from jax import lax
from jax.experimental import pallas as pl
from jax.experimental.pallas import tpu as pltpu
from jax.experimental.pallas import tpu_sc as plsc

F32 = jnp.float32
BF16 = jnp.bfloat16
I32 = jnp.int32

HEAD_DIM = 128
MOBA_BLOCK = 256
MOBA_TOPK = 3
CONV_WIDTH = 31
CONV_HALO = 32
SUBLANES = 8
LANES = 128
TOP_K = 4
SWIGLU_ALPHA = 1.702
SWIGLU_LIMIT = 7.0
LN_EPS = 1e-5
DEPTH = 1
DEEPNORM_ALPHA = (2.0 * DEPTH) ** 0.25
ROUTE_PAD = 128
AUG_DIM = 256
MASK_NEG = -1e30
ONES_ROWS = 16
ATTN_UNIT = 4
COMBINE_PARTS = 2
VMEM_LIMIT = 56 * 1024 * 1024


def _cparams(sem):
    return pltpu.CompilerParams(dimension_semantics=sem, vmem_limit_bytes=VMEM_LIMIT)


def _proj_nn_kernel(x_ref, w_ref, o_ref, wb_ref):
    @pl.when(pl.program_id(1) == 0)
    def _():
        wb_ref[...] = w_ref[...].astype(BF16)

    o_ref[...] = jnp.dot(x_ref[...].astype(BF16), wb_ref[...],
                         preferred_element_type=F32).astype(o_ref.dtype)


def _in_projection(x2, w_in, g, tm):
    s, d = x2.shape
    return pl.pallas_call(
        _proj_nn_kernel,
        out_shape=jax.ShapeDtypeStruct((s, 3 * g), BF16),
        grid=(3, s // tm),
        in_specs=[pl.BlockSpec((tm, d), lambda n, m: (m, 0)),
                  pl.BlockSpec((d, g), lambda n, m: (0, jnp.where(n == 0, 1, n + 2)))],
        out_specs=pl.BlockSpec((tm, g), lambda n, m: (m, n)),
        scratch_shapes=[pltpu.VMEM((d, g), BF16)],
        compiler_params=_cparams(("arbitrary", "arbitrary")),
        name="proj_nn",
    )(x2, w_in)


def _kmean_kernel(k_ref, o_ref, *, blocks):
    for b in range(blocks):
        kb = k_ref[b * MOBA_BLOCK:(b + 1) * MOBA_BLOCK, :].astype(F32)
        o_ref[b:b + 1, :] = jnp.sum(kb, axis=0, keepdims=True) * (1.0 / MOBA_BLOCK)


def _block_means(kgg, g, nb):
    blocks = 8
    return pl.pallas_call(
        functools.partial(_kmean_kernel, blocks=blocks),
        out_shape=jax.ShapeDtypeStruct((nb, g), F32),
        grid=(nb // blocks,),
        in_specs=[pl.BlockSpec((blocks * MOBA_BLOCK, g), lambda i: (i, 0))],
        out_specs=pl.BlockSpec((blocks, g), lambda i: (i, 0)),
        compiler_params=_cparams(("arbitrary",)),
        name="moba_kmean",
    )(kgg)


def _select_kernel(slopes_ref, qt_ref, km_ref, k_ref, qa_ref, ka_ref, *, nb, tq, n_heads):
    t = pl.program_id(0)
    col = t * tq + lax.broadcasted_iota(I32, (nb, tq), 1)
    qblk = col // MOBA_BLOCK
    j = lax.broadcasted_iota(I32, (nb, tq), 0)
    neg_inf = jnp.float32(-jnp.inf)
    past = j < qblk
    own = j == qblk
    n_extra = AUG_DIM - HEAD_DIM - nb
    r = lax.broadcasted_iota(I32, (n_extra, tq), 0)
    qb = ((t * tq + lax.broadcasted_iota(I32, (n_extra, tq), 1)) // MOBA_BLOCK).astype(F32)
    extra_unit = jnp.where(r == 0, 1.0,
                           jnp.where(r == 1, float(MOBA_BLOCK),
                                     jnp.where(r == 2, -float(MOBA_BLOCK) * qb, 0.0)))
    n_aug = AUG_DIM - HEAD_DIM
    pos = t * tq + lax.broadcasted_iota(I32, (tq, n_aug), 0)
    kblk = pos // MOBA_BLOCK
    pib = pos % MOBA_BLOCK
    lane = lax.broadcasted_iota(I32, (tq, n_aug), 1)
    aug = jnp.where(lane < nb, (lane == kblk).astype(F32),
                    jnp.where(lane == nb, pib.astype(F32),
                              jnp.where(lane == nb + 1, kblk.astype(F32),
                                        jnp.where(lane == nb + 2, 1.0, 0.0)))).astype(BF16)
    scale = HEAD_DIM ** -0.5
    for h in range(n_heads):
        cols = slice(h * HEAD_DIM, (h + 1) * HEAD_DIM)
        q_t = qt_ref[cols, :].astype(F32)
        gate = jnp.dot(km_ref[:, cols], q_t, preferred_element_type=F32,
                       precision=lax.Precision.HIGHEST)
        gcur = jnp.where(past, gate, neg_inf)
        sel = own
        for _ in range(MOBA_TOPK):
            m = jnp.max(gcur, axis=0, keepdims=True)
            is_max = jnp.logical_and(gcur == m, m > neg_inf)
            idx = jnp.min(jnp.where(is_max, j, nb), axis=0, keepdims=True)
            pick = j == idx
            sel = jnp.logical_or(sel, pick)
            gcur = jnp.where(pick, neg_inf, gcur)
        qa_ref[h, 0:HEAD_DIM, :] = (q_t * scale).astype(BF16)
        qa_ref[h, HEAD_DIM:HEAD_DIM + nb, :] = jnp.where(sel, 0.0, MASK_NEG).astype(BF16)
        qa_ref[h, HEAD_DIM + nb:AUG_DIM, :] = (slopes_ref[h] * extra_unit).astype(BF16)
        ka_ref[h, :, 0:HEAD_DIM] = k_ref[:, cols]
        ka_ref[h, :, HEAD_DIM:AUG_DIM] = aug


def _select(slopes, qv_t, kmean, kgg, n_heads, nb, tq):
    s = kgg.shape[0]
    g = n_heads * HEAD_DIM
    return pl.pallas_call(
        functools.partial(_select_kernel, nb=nb, tq=tq, n_heads=n_heads),
        out_shape=(jax.ShapeDtypeStruct((n_heads, AUG_DIM, s), BF16),
                   jax.ShapeDtypeStruct((n_heads, s, AUG_DIM), BF16)),
        grid_spec=pltpu.PrefetchScalarGridSpec(
            num_scalar_prefetch=1,
            grid=(s // tq,),
            in_specs=[pl.BlockSpec((g, tq), lambda t, sl: (0, t)),
                      pl.BlockSpec((nb, g), lambda t, sl: (0, 0)),
                      pl.BlockSpec((tq, g), lambda t, sl: (t, 0))],
            out_specs=[pl.BlockSpec((n_heads, AUG_DIM, tq), lambda t, sl: (0, 0, t)),
                       pl.BlockSpec((n_heads, tq, AUG_DIM), lambda t, sl: (0, t, 0))]),
        compiler_params=_cparams(("arbitrary",)),
        name="moba_select",
    )(slopes, qv_t, kmean, kgg)


def _attn_kernel(qa_ref, ka_ref, vt_ref, o_ref, sa_ref, sb_ref, *, heads):
    i = pl.program_id(1)
    tq = MOBA_BLOCK
    unit_keys = ATTN_UNIT * MOBA_BLOCK
    neg_inf = jnp.float32(-jnp.inf)

    def scores_of(j, nkeys):
        off = pl.multiple_of(j * MOBA_BLOCK, MOBA_BLOCK)
        return [jnp.dot(ka_ref[hh, pl.ds(off, nkeys), :], qa_ref[hh],
                        preferred_element_type=F32) for hh in range(heads)]

    def scores_into(ref, unit):
        for hh, s_t in enumerate(scores_of(ATTN_UNIT * unit, unit_keys)):
            ref[hh] = s_t

    def update(j, scores, carries, diagonal, nkeys):
        off = pl.multiple_of(j * MOBA_BLOCK, MOBA_BLOCK)
        stats = []
        for hh in range(heads):
            m = carries[hh][0]
            s_t = scores[hh]
            if diagonal:
                key = lax.broadcasted_iota(I32, s_t.shape, 0) - (nkeys - MOBA_BLOCK)
                qry = lax.broadcasted_iota(I32, s_t.shape, 1)
                s_t = jnp.where(key <= qry, s_t, neg_inf)
            m_new = jnp.maximum(m, jnp.max(s_t, axis=0, keepdims=True))
            stats.append((m_new, jnp.exp(m - m_new), jnp.exp(s_t - m_new).astype(BF16)))
        out = []
        ones = jnp.ones((ONES_ROWS, nkeys), BF16)
        for hh in range(heads):
            m_new, alpha, p = stats[hh]
            vb = vt_ref[hh * HEAD_DIM:(hh + 1) * HEAD_DIM, pl.ds(off, nkeys)]
            vb1 = jnp.concatenate([vb, ones], axis=0)
            acc_new = alpha * carries[hh][1] + jnp.dot(vb1, p, preferred_element_type=F32)
            out.append((m_new, acc_new))
        return tuple(out)

    init = tuple((jnp.full((1, tq), neg_inf, F32), jnp.zeros((HEAD_DIM + ONES_ROWS, tq), F32))
                 for _ in range(heads))
    def tail_update(r):
        nkeys = (r + 1) * MOBA_BLOCK

        def branch(cs):
            tail_scores = scores_of(i - r, nkeys)
            scores_into(sa_ref, 0)
            return update(i - r, tail_scores, cs, True, nkeys)
        return branch

    carries = lax.switch(i % ATTN_UNIT, [tail_update(r) for r in range(ATTN_UNIT)], init)

    n_units = i // ATTN_UNIT
    last_unit = jnp.maximum(n_units - 1, 0)

    def from_ref(ref):
        return [ref[hh] for hh in range(heads)]

    def two_units(k, cs):
        scores_into(sb_ref, 2 * k + 1)
        cs = update(2 * ATTN_UNIT * k, from_ref(sa_ref), cs, False, unit_keys)
        scores_into(sa_ref, jnp.minimum(2 * k + 2, last_unit))
        return update(2 * ATTN_UNIT * k + ATTN_UNIT, from_ref(sb_ref), cs, False, unit_keys)

    carries = lax.fori_loop(0, n_units // 2, two_units, carries)
    carries = lax.cond(
        n_units % 2 == 1,
        lambda cs: update(ATTN_UNIT * last_unit, from_ref(sa_ref), cs, False, unit_keys),
        lambda cs: cs, carries)
    for hh in range(heads):
        acc = carries[hh][1]
        out_t = acc[0:HEAD_DIM, :] / acc[HEAD_DIM:HEAD_DIM + 1, :]
        o_ref[:, hh * HEAD_DIM:(hh + 1) * HEAD_DIM] = out_t.T.astype(o_ref.dtype)


def _attention(q_aug, k_aug, qv_t, n_heads, nb, heads):
    s = k_aug.shape[1]
    groups = n_heads // heads
    once = pl.Buffered(1)
    return pl.pallas_call(
        functools.partial(_attn_kernel, heads=heads),
        out_shape=jax.ShapeDtypeStruct((s, n_heads * HEAD_DIM), BF16),
        grid=(groups, nb),
        in_specs=[pl.BlockSpec((heads, AUG_DIM, MOBA_BLOCK), lambda hg, i: (hg, 0, i)),
                  pl.BlockSpec((heads, s, AUG_DIM), lambda hg, i: (hg, 0, 0), pipeline_mode=once),
                  pl.BlockSpec((heads * HEAD_DIM, s), lambda hg, i: (groups + hg, 0),
                               pipeline_mode=once)],
        out_specs=pl.BlockSpec((MOBA_BLOCK, heads * HEAD_DIM), lambda hg, i: (i, hg)),
        scratch_shapes=[pltpu.VMEM((heads, ATTN_UNIT * MOBA_BLOCK, MOBA_BLOCK), F32),
                        pltpu.VMEM((heads, ATTN_UNIT * MOBA_BLOCK, MOBA_BLOCK), F32)],
        compiler_params=_cparams(("arbitrary", "arbitrary")),
        name="moba_attention",
    )(q_aug, k_aug, qv_t)


def _qv_conv_kernel(wt_ref, x_ref, ga_ref, gb_ref, w_ref, b_ref, g_ref, beta_ref, qv_ref, o_ref,
                    u_ref, ush_ref, *, ts, sub):
    t = pl.program_id(0)

    @pl.when(t == 0)
    def _():
        u_ref[0:CONV_HALO, :] = jnp.zeros((CONV_HALO, u_ref.shape[1]), F32)

    @pl.when(t > 0)
    def _():
        u_ref[0:CONV_HALO, :] = u_ref[ts:ts + CONV_HALO, :]

    x_b = x_ref[...].astype(BF16)
    n_slabs = min(8, ts // sub)
    groups_per_slab = (ts // sub) // n_slabs
    slab = wt_ref.shape[0] // n_slabs

    def project(k, anchor):
        rows = slice(k * slab, (k + 1) * slab)
        w_k = wt_ref[rows, :] if anchor is None else wt_ref[rows, :] + anchor
        qv_ref[rows, :] = lax.dot_general(w_k, x_b, (((1,), (1,)), ((), ())),
                                          preferred_element_type=F32).astype(qv_ref.dtype)

    gb = gb_ref[...].astype(F32)
    u_ref[CONV_HALO:CONV_HALO + ts, :] = ga_ref[...].astype(F32) * jax.nn.sigmoid(gb)
    for b in range(SUBLANES):
        ush_ref[b, 0:ts + CONV_HALO - b, :] = u_ref[b:ts + CONV_HALO, :]
    first = CONV_HALO - (CONV_WIDTH - 1)
    anchor = None
    for r0 in range(0, ts, sub):
        if (r0 // sub) % groups_per_slab == 0:
            project(r0 // sub // groups_per_slab, anchor)
        acc = jnp.broadcast_to(b_ref[...], (sub, u_ref.shape[1]))
        for tap in range(CONV_WIDTH):
            off = first + tap
            aligned = r0 + off - off % SUBLANES
            acc = acc + w_ref[tap:tap + 1, :] * ush_ref[off % SUBLANES, aligned:aligned + sub, :]
        mu = jnp.mean(acc, axis=-1, keepdims=True)
        cen = acc - mu
        var = jnp.mean(cen * cen, axis=-1, keepdims=True)
        y = cen * lax.rsqrt(var + LN_EPS) * g_ref[...] + beta_ref[...]
        o_ref[r0:r0 + sub, :] = (y * jax.nn.sigmoid(y)).astype(o_ref.dtype)
        bits = pltpu.bitcast(y[0:1, 0:1], I32)
        zero = lax.shift_right_logical(lax.shift_right_logical(bits, jnp.int32(31)), jnp.int32(1))
        anchor = zero.astype(BF16)


def _transpose_cast_kernel(w_ref, o_ref):
    o_ref[...] = w_ref[...].T.astype(o_ref.dtype)


def _qv_weights_transposed(w_in, g):
    d = w_in.shape[0]
    cols = min(256, g)
    per_group = g // cols
    return pl.pallas_call(
        _transpose_cast_kernel,
        out_shape=jax.ShapeDtypeStruct((2 * g, d), BF16),
        grid=(2 * per_group,),
        in_specs=[pl.BlockSpec((d, cols), lambda c: (0, c + jnp.where(c >= per_group, per_group, 0)))],
        out_specs=pl.BlockSpec((cols, d), lambda c: (c, 0)),
        compiler_params=_cparams(("arbitrary",)),
        name="qv_weight_transpose",
    )(w_in)


def _qv_projection_and_conv(x2, w_in, kgg, conv_w, conv_b, conv_ln_g, conv_ln_b, g, ts):
    s, d = x2.shape
    row = lambda v: v.reshape(1, g).astype(F32)
    w_qv_t = _qv_weights_transposed(w_in, g)
    once = pl.Buffered(1)
    return pl.pallas_call(
        functools.partial(_qv_conv_kernel, ts=ts, sub=32),
        out_shape=(jax.ShapeDtypeStruct((2 * g, s), BF16), jax.ShapeDtypeStruct((s, g), BF16)),
        grid=(s // ts,),
        in_specs=[pl.BlockSpec((2 * g, d), lambda t: (0, 0), pipeline_mode=once),
                  pl.BlockSpec((ts, d), lambda t: (t, 0)),
                  pl.BlockSpec((ts, g), lambda t: (t, 1)),
                  pl.BlockSpec((ts, g), lambda t: (t, 2)),
                  pl.BlockSpec((CONV_WIDTH, g), lambda t: (0, 0)),
                  pl.BlockSpec((1, g), lambda t: (0, 0)),
                  pl.BlockSpec((1, g), lambda t: (0, 0)),
                  pl.BlockSpec((1, g), lambda t: (0, 0))],
        out_specs=[pl.BlockSpec((2 * g, ts), lambda t: (0, t)),
                   pl.BlockSpec((ts, g), lambda t: (t, 0))],
        scratch_shapes=[pltpu.VMEM((ts + CONV_HALO, g), F32),
                        pltpu.VMEM((SUBLANES, ts + CONV_HALO, g), F32)],
        compiler_params=_cparams(("arbitrary",)),
        name="qv_proj_conformer_conv",
    )(w_qv_t, x2, kgg, kgg, conv_w, row(conv_b), row(conv_ln_g), row(conv_ln_b))


def _layer_norm_rows(z, gain, bias):
    mu = jnp.mean(z, axis=-1, keepdims=True)
    cen = z - mu
    var = jnp.mean(cen * cen, axis=-1, keepdims=True)
    return cen * lax.rsqrt(var + LN_EPS) * gain + bias


HI_HALF = -65536


def _pack_halves(lo, hi):
    lo_bits = pltpu.bitcast(lo.astype(BF16).astype(F32), I32)
    hi_bits = pltpu.bitcast(hi.astype(BF16).astype(F32), I32)
    return lax.shift_right_logical(lo_bits, jnp.int32(16)) | (hi_bits & jnp.int32(HI_HALF))


def _unpack_halves(words):
    lo = pltpu.bitcast(lax.shift_left(words, jnp.int32(16)), F32)
    hi = pltpu.bitcast(words & jnp.int32(HI_HALF), F32)
    return lo, hi


def _outproj_kernel(attn_ref, conv_ref, wo_ref, x_ref, g_ref, b_ref, wr_ref, br_ref,
                    h_ref, hp_ref, lg_ref, *, g):
    tm = x_ref.shape[0]
    sub = min(tm, 256)
    half = x_ref.shape[1] // 2
    n_exp = lg_ref.shape[0]
    wr = wr_ref[...]
    w_hi = wr.astype(BF16)
    w_lo = (wr - w_hi.astype(F32)).astype(BF16)
    mixes = []
    for r0 in range(0, tm, sub):
        mix = jnp.dot(attn_ref[r0:r0 + sub, :], wo_ref[0:g, :], preferred_element_type=F32)
        mixes.append(mix + jnp.dot(conv_ref[r0:r0 + sub, :], wo_ref[g:2 * g, :],
                                   preferred_element_type=F32))
    for k, r0 in enumerate(range(0, tm, sub)):
        h1 = _layer_norm_rows(DEEPNORM_ALPHA * x_ref[r0:r0 + sub, :] + mixes[k],
                              g_ref[...], b_ref[...])
        h_ref[r0:r0 + sub, :] = h1
        hp_ref[r0:r0 + sub, :] = _pack_halves(h1[:, 0:half], h1[:, half:])
        h_hi = h1.astype(BF16)
        h_lo = (h1 - h_hi.astype(F32)).astype(BF16)
        lg = jnp.dot(h_hi, w_hi, preferred_element_type=F32)
        lg = lg + jnp.dot(h_lo, w_hi, preferred_element_type=F32)
        lg = lg + jnp.dot(h_hi, w_lo, preferred_element_type=F32)
        lg_ref[:, r0:r0 + sub] = lg.T[0:n_exp, :] + br_ref[...]


def _out_projection(attn, conv, w_out, x2, ln_g, ln_b, w_router, b_router, tm):
    s, d = x2.shape
    g = attn.shape[1]
    e = w_router.shape[1]
    return pl.pallas_call(
        functools.partial(_outproj_kernel, g=g),
        out_shape=(jax.ShapeDtypeStruct((s, d), F32), jax.ShapeDtypeStruct((s, d // 2), I32),
                   jax.ShapeDtypeStruct((e, s), F32)),
        grid=(s // tm,),
        in_specs=[pl.BlockSpec((tm, g), lambda m: (m, 0)),
                  pl.BlockSpec((tm, g), lambda m: (m, 0)),
                  pl.BlockSpec((2 * g, d), lambda m: (0, 0), pipeline_mode=pl.Buffered(1)),
                  pl.BlockSpec((tm, d), lambda m: (m, 0)),
                  pl.BlockSpec((1, d), lambda m: (0, 0)),
                  pl.BlockSpec((1, d), lambda m: (0, 0)),
                  pl.BlockSpec((d, LANES), lambda m: (0, 0)),
                  pl.BlockSpec((e, 1), lambda m: (0, 0))],
        out_specs=[pl.BlockSpec((tm, d), lambda m: (m, 0)),
                   pl.BlockSpec((tm, d // 2), lambda m: (m, 0)),
                   pl.BlockSpec((e, tm), lambda m: (0, m))],
        compiler_params=_cparams(("arbitrary",)),
        name="outproj_ln1_router",
    )(attn, conv, w_out.astype(BF16), x2, ln_g.reshape(1, d), ln_b.reshape(1, d),
      jnp.pad(w_router, ((0, 0), (0, LANES - e))), b_router.reshape(e, 1))


def _route_kernel(lg_ref, e_ref, gate_ref, rank_ref, cnt_ref, carry_ref, *, n_exp, tr):
    t = pl.program_id(0)

    @pl.when(t == 0)
    def _():
        carry_ref[...] = jnp.zeros_like(carry_ref)

    neg_inf = jnp.float32(-jnp.inf)
    cur = lg_ref[...]
    j = lax.broadcasted_iota(I32, (n_exp, tr), 0)
    vals, picks = [], []
    for r in range(TOP_K):
        m = jnp.max(cur, axis=0, keepdims=True)
        idx = jnp.min(jnp.where(cur == m, j, n_exp), axis=0, keepdims=True)
        pick = j == idx
        e_ref[r:r + 1, :] = idx
        vals.append(m)
        picks.append(pick)
        cur = jnp.where(pick, neg_inf, cur)
    exps = [jnp.exp(v - vals[0]) for v in vals]
    den = exps[0]
    for r in range(1, TOP_K):
        den = den + exps[r]
    for r in range(TOP_K):
        gate_ref[r:r + 1, :] = exps[r] / den

    chosen = picks[0].astype(F32)
    for r in range(1, TOP_K):
        chosen = chosen + picks[r].astype(F32)
    a = lax.broadcasted_iota(I32, (tr, tr), 0)
    b = lax.broadcasted_iota(I32, (tr, tr), 1)
    upper = (a < b).astype(BF16)
    excl = jnp.dot(chosen.astype(BF16), upper, preferred_element_type=F32)
    base = carry_ref[:, 0:1]
    rank = excl + base
    for r in range(TOP_K):
        rank_ref[r:r + 1, :] = jnp.sum(jnp.where(picks[r], rank, 0.0), axis=0,
                                       keepdims=True).astype(I32)
    total = base + jnp.sum(chosen, axis=1, keepdims=True)
    carry_ref[...] = jnp.broadcast_to(total, carry_ref.shape)
    cnt_ref[...] = jnp.broadcast_to(total, cnt_ref.shape).astype(I32)


def _route(logits_t, tr):
    n_exp, s = logits_t.shape
    return pl.pallas_call(
        functools.partial(_route_kernel, n_exp=n_exp, tr=tr),
        out_shape=(jax.ShapeDtypeStruct((TOP_K, s), I32), jax.ShapeDtypeStruct((TOP_K, s), F32),
                   jax.ShapeDtypeStruct((TOP_K, s), I32), jax.ShapeDtypeStruct((n_exp, 128), I32)),
        grid=(s // tr,),
        in_specs=[pl.BlockSpec((n_exp, tr), lambda t: (0, t))],
        out_specs=[pl.BlockSpec((TOP_K, tr), lambda t: (0, t)),
                   pl.BlockSpec((TOP_K, tr), lambda t: (0, t)),
                   pl.BlockSpec((TOP_K, tr), lambda t: (0, t)),
                   pl.BlockSpec((n_exp, 128), lambda t: (0, 0))],
        scratch_shapes=[pltpu.VMEM((n_exp, 128), F32)],
        compiler_params=_cparams(("arbitrary",)),
        name="moe_route",
    )(logits_t)


SC_CORES = 2
SC_SUBCORES = 16
SC_LANES = 16
SC_GATHER_ROWS = 32
SC_SCAN = 8192


_SC_PARAMS = pltpu.CompilerParams(needs_layout_passes=False)


def _sc_mesh():
    return plsc.VectorSubcoreMesh(core_axis_name="c", subcore_axis_name="s",
                                  num_cores=SC_CORES, num_subcores=SC_SUBCORES)


def _sc_worker():
    return lax.axis_index("s") * SC_CORES + lax.axis_index("c")


def _sc_gather_loop(table_hbm, idx_all, out_hbm, bufs, sems, base, n_chunks):
    chunk = SC_GATHER_ROWS

    def gather(ci, b):
        return pltpu.make_async_copy(
            table_hbm.at[idx_all.at[pl.ds(ci * chunk, chunk)]], bufs[b], sems[b])

    gather(0, 0).start()

    @pl.loop(0, n_chunks, step=2)
    def _(ci):
        gather(ci + 1, 1).start()
        gather(ci, 0).wait()
        pltpu.sync_copy(bufs[0], out_hbm.at[pl.ds(base + ci * chunk, chunk)])

        @pl.when(ci + 2 < n_chunks)
        def _():
            gather(ci + 2, 0).start()

        gather(ci + 1, 1).wait()
        pltpu.sync_copy(bufs[1], out_hbm.at[pl.ds(base + (ci + 1) * chunk, chunk)])


def _sc_dest(ps_v, e_buf, r_buf, v):
    e_vec = e_buf[pl.ds(v * SC_LANES, SC_LANES)]
    return plsc.load_gather(ps_v, [e_vec]) + r_buf[pl.ds(v * SC_LANES, SC_LANES)]


def _sc_dispatch(table, e_flat, rank_flat, pad_start, cap, s):
    n_assign, width = e_flat.shape[0], table.shape[1]
    n_workers = SC_CORES * SC_SUBCORES
    per_worker = cap // n_workers
    chunk = SC_GATHER_ROWS
    assert cap % (n_workers * 2 * chunk) == 0 and n_assign % SC_SCAN == 0
    assert per_worker % SC_LANES == 0 and s % SC_SCAN == 0

    @functools.partial(
        pl.kernel, mesh=_sc_mesh(), out_type=jax.ShapeDtypeStruct((cap, width), table.dtype),
        scratch_types=[pltpu.VMEM((per_worker,), I32), pltpu.VMEM((pad_start.shape[0],), I32),
                       pltpu.VMEM((SC_SCAN,), I32), pltpu.VMEM((SC_SCAN,), I32),
                       pltpu.VMEM((chunk, width), table.dtype),
                       pltpu.VMEM((chunk, width), table.dtype),
                       pltpu.SemaphoreType.DMA, pltpu.SemaphoreType.DMA],
        compiler_params=_SC_PARAMS, name="sc_dispatch")
    def dispatch(table_hbm, e_hbm, r_hbm, ps_hbm, out_hbm, idx_all, ps_v, e_buf, r_buf,
                 buf0, buf1, sem0, sem1):
        base = _sc_worker() * per_worker
        lane = lax.iota(I32, SC_LANES)
        pltpu.sync_copy(ps_hbm, ps_v)

        @pl.loop(0, per_worker // SC_LANES)
        def _(k):
            idx_all[pl.ds(k * SC_LANES, SC_LANES)] = lax.rem(base + k * SC_LANES + lane, s)

        @pl.loop(0, n_assign // SC_SCAN)
        def _(c):
            pltpu.sync_copy(e_hbm.at[pl.ds(c * SC_SCAN, SC_SCAN)], e_buf)
            pltpu.sync_copy(r_hbm.at[pl.ds(c * SC_SCAN, SC_SCAN)], r_buf)
            tok0 = lax.rem(c * SC_SCAN, s)

            @pl.loop(0, SC_SCAN // SC_LANES)
            def _(v):
                loc = _sc_dest(ps_v, e_buf, r_buf, v) - base
                mine = jnp.logical_and(loc >= 0, loc < per_worker)
                tok = tok0 + v * SC_LANES + lane
                plsc.store_scatter(idx_all, [jnp.where(mine, loc, 0)], tok, mask=mine)

        _sc_gather_loop(table_hbm, idx_all, out_hbm, (buf0, buf1), (sem0, sem1), base,
                        per_worker // chunk)

    return dispatch(table, e_flat, rank_flat, pad_start)


def _sc_combine_gather(y, e_flat, rank_flat, pad_start):
    n_assign, width = e_flat.shape[0], y.shape[1]
    n_workers = SC_CORES * SC_SUBCORES
    per_worker = n_assign // n_workers
    chunk = SC_GATHER_ROWS
    assert n_assign % (n_workers * 2 * chunk) == 0

    @functools.partial(
        pl.kernel, mesh=_sc_mesh(), out_type=jax.ShapeDtypeStruct((n_assign, width), y.dtype),
        scratch_types=[pltpu.VMEM((per_worker,), I32), pltpu.VMEM((pad_start.shape[0],), I32),
                       pltpu.VMEM((per_worker,), I32), pltpu.VMEM((per_worker,), I32),
                       pltpu.VMEM((chunk, width), y.dtype), pltpu.VMEM((chunk, width), y.dtype),
                       pltpu.SemaphoreType.DMA, pltpu.SemaphoreType.DMA],
        compiler_params=_SC_PARAMS, name="sc_combine_gather")
    def combine(y_hbm, e_hbm, r_hbm, ps_hbm, out_hbm, idx_all, ps_v, e_buf, r_buf,
                buf0, buf1, sem0, sem1):
        base = _sc_worker() * per_worker
        pltpu.sync_copy(ps_hbm, ps_v)
        pltpu.sync_copy(e_hbm.at[pl.ds(base, per_worker)], e_buf)
        pltpu.sync_copy(r_hbm.at[pl.ds(base, per_worker)], r_buf)

        @pl.loop(0, per_worker // SC_LANES)
        def _(v):
            idx_all[pl.ds(v * SC_LANES, SC_LANES)] = _sc_dest(ps_v, e_buf, r_buf, v)

        _sc_gather_loop(y_hbm, idx_all, out_hbm, (buf0, buf1), (sem0, sem1), base,
                        per_worker // chunk)

    return combine(y, e_flat, rank_flat, pad_start)


def _expert_kernel(item_e, item_start, item_rows, xs_ref, wg_ref, wu_ref, wd_ref,
                   bg_ref, bu_ref, bd_ref, y_ref, xin_ref, acc_ref,
                   sem_x, sem_y, *, nf, n_items, big):
    w = pl.program_id(0)
    f = pl.program_id(1)
    rows = item_rows[w]
    slot = w % 2
    half = xin_ref.shape[2]
    has_next = jnp.logical_and(w + 1 < n_items, item_rows[jnp.minimum(w + 1, n_items - 1)] > 0)

    def x_copy(item, sl, c):
        src = pl.multiple_of(item_start[item] + c * ROUTE_PAD, ROUTE_PAD)
        dst = pl.multiple_of(c * ROUTE_PAD, ROUTE_PAD)
        return pltpu.make_async_copy(xs_ref.at[pl.ds(src, ROUTE_PAD)],
                                     xin_ref.at[sl, pl.ds(dst, ROUTE_PAD)], sem_x.at[sl])

    def y_copy(item, sl, c):
        src = pl.multiple_of(c * ROUTE_PAD, ROUTE_PAD)
        dst = pl.multiple_of(item_start[item] + c * ROUTE_PAD, ROUTE_PAD)
        return pltpu.make_async_copy(xin_ref.at[sl, pl.ds(src, ROUTE_PAD)],
                                     y_ref.at[pl.ds(dst, ROUTE_PAD)], sem_y)

    def for_chunks(item, fn):
        def body(c, carry):
            fn(c)
            return carry
        lax.fori_loop(0, item_rows[item] // ROUTE_PAD, body, 0)

    def compute(r0, size, first):
        lo, hi = _unpack_halves(xin_ref[slot, pl.ds(r0, size), :])
        lo = lo.astype(BF16)
        hi = hi.astype(BF16)

        def x_dot(w_ref):
            return (jnp.dot(lo, w_ref[0, 0:half, :].astype(BF16), preferred_element_type=F32)
                    + jnp.dot(hi, w_ref[0, half:2 * half, :].astype(BF16),
                              preferred_element_type=F32))

        gate = x_dot(wg_ref) + bg_ref[0]
        up = x_dot(wu_ref) + bu_ref[0]
        gate = jnp.minimum(gate, SWIGLU_LIMIT)
        up = jnp.clip(up, -SWIGLU_LIMIT, SWIGLU_LIMIT)
        act = (up + 1.0) * (gate * jax.nn.sigmoid(SWIGLU_ALPHA * gate))
        part = jnp.dot(act.astype(BF16), wd_ref[0].astype(BF16), preferred_element_type=F32)
        if first:
            acc_ref[pl.ds(r0, size), :] = part + bd_ref[0]
        else:
            acc_ref[pl.ds(r0, size), :] += part

    def compute_all(first):
        n_big = rows // big
        rem = rows - n_big * big
        merge = jnp.logical_and(n_big >= 1, jnp.logical_and(rem > 0, rem <= big // 2))
        n_loop = n_big - jnp.where(merge, 1, 0)

        def body(c, carry):
            compute(pl.multiple_of(c * big, big), big, first)
            return carry
        lax.fori_loop(0, n_loop, body, 0)
        for extra in range(ROUTE_PAD, big // 2 + 1, ROUTE_PAD):
            @pl.when(jnp.logical_and(merge, rem == extra))
            def _(extra=extra):
                compute(pl.multiple_of(n_loop * big, big), big + extra, first)

        @pl.when(jnp.logical_not(merge))
        def _():
            done = n_big * big
            size = big // 2
            while size >= ROUTE_PAD:
                take = ((rows - done) & size) != 0
                here = done

                @pl.when(take)
                def _(here=here, size=size):
                    compute(pl.multiple_of(here, ROUTE_PAD), size, first)
                done = done + jnp.where(take, size, 0)
                size //= 2

    @pl.when(rows > 0)
    def _():
        @pl.when(f == 0)
        def _():
            @pl.when(w == 0)
            def _():
                for_chunks(w, lambda c: x_copy(w, slot, c).start())

            for_chunks(w, lambda c: x_copy(w, slot, c).wait())

            @pl.when(w > 0)
            def _():
                for_chunks(w - 1, lambda c: y_copy(w - 1, 1 - slot, c).wait())

            @pl.when(has_next)
            def _():
                for_chunks(w + 1, lambda c: x_copy(w + 1, 1 - slot, c).start())

            compute_all(True)

        @pl.when(f > 0)
        def _():
            compute_all(False)

        @pl.when(f == nf - 1)
        def _():
            def pack(c):
                r0 = pl.multiple_of(c * ROUTE_PAD, ROUTE_PAD)
                xin_ref[slot, pl.ds(r0, ROUTE_PAD), :] = _pack_halves(
                    acc_ref[pl.ds(r0, ROUTE_PAD), 0:half],
                    acc_ref[pl.ds(r0, ROUTE_PAD), half:2 * half])
                y_copy(w, slot, c).start()
            for_chunks(w, pack)

            @pl.when(jnp.logical_not(has_next))
            def _():
                for_chunks(w, lambda c: y_copy(w, slot, c).wait())


def _experts(item_e, item_start, item_rows, n_active, xs, w_gate_up, b_gate_up, w_down, b_down,
             tm, tf):
    cap, half = xs.shape
    d = 2 * half
    n_exp, _, two_f = w_gate_up.shape
    ff = two_f // 2
    nf = ff // tf
    n_items = item_e.shape[0]
    big = min(512, tm)
    assert big % ROUTE_PAD == 0 and (big & (big - 1)) == 0

    def fidx(w, f, rows):
        return jnp.where(rows[w] > 0, f, nf - 1)

    return pl.pallas_call(
        functools.partial(_expert_kernel, nf=nf, n_items=n_items, big=big),
        out_shape=jax.ShapeDtypeStruct((cap, half), I32),
        grid_spec=pltpu.PrefetchScalarGridSpec(
            num_scalar_prefetch=3,
            grid=(n_active, nf),
            in_specs=[
                pl.BlockSpec(memory_space=pl.ANY),
                pl.BlockSpec((1, d, tf), lambda w, f, ie, ist, ir: (ie[w], 0, fidx(w, f, ir))),
                pl.BlockSpec((1, d, tf), lambda w, f, ie, ist, ir: (ie[w], 0, nf + fidx(w, f, ir))),
                pl.BlockSpec((1, tf, d), lambda w, f, ie, ist, ir: (ie[w], fidx(w, f, ir), 0)),
                pl.BlockSpec((1, 1, tf), lambda w, f, ie, ist, ir: (ie[w], 0, fidx(w, f, ir))),
                pl.BlockSpec((1, 1, tf), lambda w, f, ie, ist, ir: (ie[w], 0, nf + fidx(w, f, ir))),
                pl.BlockSpec((1, 1, d), lambda w, f, ie, ist, ir: (ie[w], 0, 0)),
            ],
            out_specs=pl.BlockSpec(memory_space=pl.ANY),
            scratch_shapes=[pltpu.VMEM((2, tm, half), I32), pltpu.VMEM((tm, d), F32),
                            pltpu.SemaphoreType.DMA((2,)), pltpu.SemaphoreType.DMA]),
        compiler_params=_cparams(("arbitrary", "arbitrary")),
        name="moe_experts",
    )(item_e, item_start, item_rows, xs, w_gate_up, w_gate_up, w_down,
      b_gate_up.reshape(n_exp, 1, two_f), b_gate_up.reshape(n_exp, 1, two_f),
      b_down.reshape(n_exp, 1, d))


def _combine_kernel(yg_ref, gate_ref, h_ref, g_ref, b_ref, o_ref):
    gates = gate_ref[...]
    half = yg_ref.shape[2]
    z_lo = DEEPNORM_ALPHA * h_ref[:, 0:half]
    z_hi = DEEPNORM_ALPHA * h_ref[:, half:2 * half]
    for r in range(TOP_K):
        lo, hi = _unpack_halves(yg_ref[r])
        z_lo = z_lo + gates[:, r:r + 1] * lo
        z_hi = z_hi + gates[:, r:r + 1] * hi
    inv_d = 1.0 / (2 * half)
    mu = (jnp.sum(z_lo, axis=-1, keepdims=True) + jnp.sum(z_hi, axis=-1, keepdims=True)) * inv_d
    c_lo = z_lo - mu
    c_hi = z_hi - mu
    var = (jnp.sum(c_lo * c_lo, axis=-1, keepdims=True)
           + jnp.sum(c_hi * c_hi, axis=-1, keepdims=True)) * inv_d
    rstd = lax.rsqrt(var + LN_EPS)
    o_ref[:, 0:half] = c_lo * rstd * g_ref[:, 0:half] + b_ref[:, 0:half]
    o_ref[:, half:2 * half] = c_hi * rstd * g_ref[:, half:2 * half] + b_ref[:, half:2 * half]


def _combine_part_kernel(yg_ref, gate_ref, h_ref, g_ref, b_ref, prev_ref, o_ref):
    del prev_ref
    _combine_kernel(yg_ref, gate_ref, h_ref, g_ref, b_ref, o_ref)


def _combine(yg, gates, h1, ln_g, ln_b, tc, first_tile, out_prev):
    s, d = h1.shape
    n_tiles = yg.shape[1] // tc
    in_specs = [pl.BlockSpec((TOP_K, tc, d // 2), lambda t: (0, t, 0)),
                pl.BlockSpec((tc, TOP_K), lambda t: (t + first_tile, 0)),
                pl.BlockSpec((tc, d), lambda t: (t + first_tile, 0)),
                pl.BlockSpec((1, d), lambda t: (0, 0)),
                pl.BlockSpec((1, d), lambda t: (0, 0))]
    args = [yg, gates, h1, ln_g.reshape(1, d), ln_b.reshape(1, d)]
    if out_prev is None:
        body, aliases = _combine_kernel, {}
    else:
        body, aliases = _combine_part_kernel, {len(args): 0}
        in_specs.append(pl.BlockSpec(memory_space=pl.ANY))
        args.append(out_prev)
    return pl.pallas_call(
        body,
        out_shape=jax.ShapeDtypeStruct((s, d), F32),
        grid=(n_tiles,),
        in_specs=in_specs,
        out_specs=pl.BlockSpec((tc, d), lambda t: (t + first_tile, 0)),
        input_output_aliases=aliases,
        compiler_params=_cparams(("arbitrary",)),
        name="moe_combine_ln2",
    )(*args)


def _tiles(s, d, ff, n_heads):
    return dict(
        proj_tm=min(1024, s), select_tq=min(1024, s), conv_ts=min(512, s), outproj_tm=min(512, s),
        attn_heads=min(4, n_heads),
        route_tr=min(512, s), combine_tc=min(512, s),
        expert_tm=min(1280, max(ROUTE_PAD, (s * TOP_K // 16) // ROUTE_PAD * ROUTE_PAD)),
        expert_tf=min(512, ff))


def kernel(x, w_in, conv_w, conv_b, conv_ln_g, conv_ln_b, w_out, ln1_g, ln1_b,
           w_router, b_router, w_gate_up, b_gate_up, w_down, b_down, ln2_g, ln2_b):
    b, s, d = x.shape
    assert b == 1, "the kernels treat the sequence of the single batch element as the row axis"
    g = w_in.shape[1] // 5
    n_heads = g // HEAD_DIM
    nb = s // MOBA_BLOCK
    n_exp = w_router.shape[1]
    ff = w_down.shape[1]
    assert s % MOBA_BLOCK == 0 and nb % 8 == 0 and HEAD_DIM + nb + 3 <= AUG_DIM
    tl = _tiles(s, d, ff, n_heads)
    x2 = x.reshape(s, d)

    kgg = _in_projection(x2, w_in, g, tl["proj_tm"])
    qv_t, conv = _qv_projection_and_conv(x2, w_in, kgg, conv_w, conv_b, conv_ln_g, conv_ln_b, g,
                                         tl["conv_ts"])
    kmean = _block_means(kgg, g, nb)
    slopes = 2.0 ** (-(8.0 / n_heads) * jnp.arange(1, n_heads + 1, dtype=F32))
    q_aug, k_aug = _select(slopes, qv_t, kmean, kgg, n_heads, nb, tl["select_tq"])
    attn = _attention(q_aug, k_aug, qv_t, n_heads, nb, tl["attn_heads"])
    h1, h1_packed, logits_t = _out_projection(attn, conv, w_out, x2, ln1_g, ln1_b,
                                              w_router, b_router, tl["outproj_tm"])

    e_t, gate_t, rank_t, counts = _route(logits_t, tl["route_tr"])
    counts = counts[:, 0]
    padded = (counts + ROUTE_PAD - 1) // ROUTE_PAD * ROUTE_PAD
    pad_end = jnp.cumsum(padded)
    pad_start = (pad_end - padded).astype(I32)
    cap = s * TOP_K + n_exp * ROUTE_PAD
    e_flat = e_t.reshape(-1)
    rank_flat = rank_t.reshape(-1)
    xs = _sc_dispatch(h1_packed, e_flat, rank_flat, pad_start, cap, s)

    tm = tl["expert_tm"]
    n_items = cap // tm + n_exp
    per_e = (padded + tm - 1) // tm
    item_end = jnp.cumsum(per_e)
    item_ids = jnp.arange(n_items, dtype=I32)
    item_e = jnp.minimum(jnp.sum(item_ids[:, None] >= item_end[None, :], axis=1),
                         n_exp - 1).astype(I32)
    piece = item_ids - (item_end - per_e)[item_e]
    active = item_ids < item_end[-1]
    item_start = jnp.where(active, pad_start[item_e] + piece * tm, 0).astype(I32)
    item_rows = jnp.where(active, jnp.minimum(tm, padded[item_e] - piece * tm), 0).astype(I32)
    last_e = item_e[jnp.maximum(item_end[-1] - 1, 0)]
    item_e = jnp.where(active, item_e, last_e).astype(I32)

    y = _experts(item_e, item_start, item_rows, item_end[-1].astype(I32), xs,
                 w_gate_up, b_gate_up, w_down, b_down, tm, tl["expert_tf"])
    tc = tl["combine_tc"]
    s_part = s // COMBINE_PARTS
    gates = gate_t.T
    out = None
    for part in range(COMBINE_PARTS):
        cols = slice(part * s_part, (part + 1) * s_part)
        yg = _sc_combine_gather(y, e_t[:, cols].reshape(-1), rank_t[:, cols].reshape(-1), pad_start)
        out = _combine(yg.reshape(TOP_K, s_part, d // 2), gates, h1, ln2_g, ln2_b, tc,
                       part * (s_part // tc), out)
    return out.reshape(b, s, d)
```

```python
import functools

import jax
import jax.numpy as jnp
from jax import lax
from jax.experimental import pallas as pl
from jax.experimental.pallas import tpu as pltpu
from jax.experimental.pallas import tpu_sc as plsc

F32 = jnp.float32
BF16 = jnp.bfloat16
I32 = jnp.int32

HEAD_DIM = 128
MOBA_BLOCK = 256
MOBA_TOPK = 3
CONV_WIDTH = 31
CONV_HALO = 32
SUBLANES = 8
LANES = 128
TOP_K = 4
SWIGLU_ALPHA = 1.702
SWIGLU_LIMIT = 7.0
LN_EPS = 1e-5
DEPTH = 1
DEEPNORM_ALPHA = (2.0 * DEPTH) ** 0.25
ROUTE_PAD = 128
AUG_DIM = 256
MASK_NEG = -1e30
ONES_ROWS = 16
ATTN_UNIT = 4
COMBINE_PARTS = 2
VMEM_LIMIT = 56 * 1024 * 1024


def _cparams(sem):
    return pltpu.CompilerParams(dimension_semantics=sem, vmem_limit_bytes=VMEM_LIMIT)


def _proj_nn_kernel(x_ref, w_ref, o_ref, wb_ref):
    @pl.when(pl.program_id(1) == 0)
    def _():
        wb_ref[...] = w_ref[...].astype(BF16)

    o_ref[...] = jnp.dot(x_ref[...].astype(BF16), wb_ref[...],
                         preferred_element_type=F32).astype(o_ref.dtype)


def _in_projection(x2, w_in, g, tm):
    s, d = x2.shape
    return pl.pallas_call(
        _proj_nn_kernel,
        out_shape=jax.ShapeDtypeStruct((s, 3 * g), BF16),
        grid=(3, s // tm),
        in_specs=[pl.BlockSpec((tm, d), lambda n, m: (m, 0)),
                  pl.BlockSpec((d, g), lambda n, m: (0, jnp.where(n == 0, 1, n + 2)))],
        out_specs=pl.BlockSpec((tm, g), lambda n, m: (m, n)),
        scratch_shapes=[pltpu.VMEM((d, g), BF16)],
        compiler_params=_cparams(("arbitrary", "arbitrary")),
        name="proj_nn",
    )(x2, w_in)


def _kmean_kernel(k_ref, o_ref, *, blocks):
    for b in range(blocks):
        kb = k_ref[b * MOBA_BLOCK:(b + 1) * MOBA_BLOCK, :].astype(F32)
        o_ref[b:b + 1, :] = jnp.sum(kb, axis=0, keepdims=True) * (1.0 / MOBA_BLOCK)


def _block_means(kgg, g, nb):
    blocks = 8
    return pl.pallas_call(
        functools.partial(_kmean_kernel, blocks=blocks),
        out_shape=jax.ShapeDtypeStruct((nb, g), F32),
        grid=(nb // blocks,),
        in_specs=[pl.BlockSpec((blocks * MOBA_BLOCK, g), lambda i: (i, 0))],
        out_specs=pl.BlockSpec((blocks, g), lambda i: (i, 0)),
        compiler_params=_cparams(("arbitrary",)),
        name="moba_kmean",
    )(kgg)


def _select_kernel(slopes_ref, qt_ref, km_ref, k_ref, qa_ref, ka_ref, *, nb, tq, n_heads):
    t = pl.program_id(0)
    col = t * tq + lax.broadcasted_iota(I32, (nb, tq), 1)
    qblk = col // MOBA_BLOCK
    j = lax.broadcasted_iota(I32, (nb, tq), 0)
    neg_inf = jnp.float32(-jnp.inf)
    past = j < qblk
    own = j == qblk
    n_extra = AUG_DIM - HEAD_DIM - nb
    r = lax.broadcasted_iota(I32, (n_extra, tq), 0)
    qb = ((t * tq + lax.broadcasted_iota(I32, (n_extra, tq), 1)) // MOBA_BLOCK).astype(F32)
    extra_unit = jnp.where(r == 0, 1.0,
                           jnp.where(r == 1, float(MOBA_BLOCK),
                                     jnp.where(r == 2, -float(MOBA_BLOCK) * qb, 0.0)))
    n_aug = AUG_DIM - HEAD_DIM
    pos = t * tq + lax.broadcasted_iota(I32, (tq, n_aug), 0)
    kblk = pos // MOBA_BLOCK
    pib = pos % MOBA_BLOCK
    lane = lax.broadcasted_iota(I32, (tq, n_aug), 1)
    aug = jnp.where(lane < nb, (lane == kblk).astype(F32),
                    jnp.where(lane == nb, pib.astype(F32),
                              jnp.where(lane == nb + 1, kblk.astype(F32),
                                        jnp.where(lane == nb + 2, 1.0, 0.0)))).astype(BF16)
    scale = HEAD_DIM ** -0.5
    for h in range(n_heads):
        cols = slice(h * HEAD_DIM, (h + 1) * HEAD_DIM)
        q_t = qt_ref[cols, :].astype(F32)
        gate = jnp.dot(km_ref[:, cols], q_t, preferred_element_type=F32,
                       precision=lax.Precision.HIGHEST)
        gcur = jnp.where(past, gate, neg_inf)
        sel = own
        for _ in range(MOBA_TOPK):
            m = jnp.max(gcur, axis=0, keepdims=True)
            is_max = jnp.logical_and(gcur == m, m > neg_inf)
            idx = jnp.min(jnp.where(is_max, j, nb), axis=0, keepdims=True)
            pick = j == idx
            sel = jnp.logical_or(sel, pick)
            gcur = jnp.where(pick, neg_inf, gcur)
        qa_ref[h, 0:HEAD_DIM, :] = (q_t * scale).astype(BF16)
        qa_ref[h, HEAD_DIM:HEAD_DIM + nb, :] = jnp.where(sel, 0.0, MASK_NEG).astype(BF16)
        qa_ref[h, HEAD_DIM + nb:AUG_DIM, :] = (slopes_ref[h] * extra_unit).astype(BF16)
        ka_ref[h, :, 0:HEAD_DIM] = k_ref[:, cols]
        ka_ref[h, :, HEAD_DIM:AUG_DIM] = aug


def _select(slopes, qv_t, kmean, kgg, n_heads, nb, tq):
    s = kgg.shape[0]
    g = n_heads * HEAD_DIM
    return pl.pallas_call(
        functools.partial(_select_kernel, nb=nb, tq=tq, n_heads=n_heads),
        out_shape=(jax.ShapeDtypeStruct((n_heads, AUG_DIM, s), BF16),
                   jax.ShapeDtypeStruct((n_heads, s, AUG_DIM), BF16)),
        grid_spec=pltpu.PrefetchScalarGridSpec(
            num_scalar_prefetch=1,
            grid=(s // tq,),
            in_specs=[pl.BlockSpec((g, tq), lambda t, sl: (0, t)),
                      pl.BlockSpec((nb, g), lambda t, sl: (0, 0)),
                      pl.BlockSpec((tq, g), lambda t, sl: (t, 0))],
            out_specs=[pl.BlockSpec((n_heads, AUG_DIM, tq), lambda t, sl: (0, 0, t)),
                       pl.BlockSpec((n_heads, tq, AUG_DIM), lambda t, sl: (0, t, 0))]),
        compiler_params=_cparams(("arbitrary",)),
        name="moba_select",
    )(slopes, qv_t, kmean, kgg)


def _attn_kernel(qa_ref, ka_ref, vt_ref, o_ref, sa_ref, sb_ref, *, heads):
    i = pl.program_id(1)
    tq = MOBA_BLOCK
    unit_keys = ATTN_UNIT * MOBA_BLOCK
    neg_inf = jnp.float32(-jnp.inf)

    def scores_of(j, nkeys):
        off = pl.multiple_of(j * MOBA_BLOCK, MOBA_BLOCK)
        return [jnp.dot(ka_ref[hh, pl.ds(off, nkeys), :], qa_ref[hh],
                        preferred_element_type=F32) for hh in range(heads)]

    def scores_into(ref, unit):
        for hh, s_t in enumerate(scores_of(ATTN_UNIT * unit, unit_keys)):
            ref[hh] = s_t

    def update(j, scores, carries, diagonal, nkeys):
        off = pl.multiple_of(j * MOBA_BLOCK, MOBA_BLOCK)
        stats = []
        for hh in range(heads):
            m = carries[hh][0]
            s_t = scores[hh]
            if diagonal:
                key = lax.broadcasted_iota(I32, s_t.shape, 0) - (nkeys - MOBA_BLOCK)
                qry = lax.broadcasted_iota(I32, s_t.shape, 1)
                s_t = jnp.where(key <= qry, s_t, neg_inf)
            m_new = jnp.maximum(m, jnp.max(s_t, axis=0, keepdims=True))
            stats.append((m_new, jnp.exp(m - m_new), jnp.exp(s_t - m_new).astype(BF16)))
        out = []
        ones = jnp.ones((ONES_ROWS, nkeys), BF16)
        for hh in range(heads):
            m_new, alpha, p = stats[hh]
            vb = vt_ref[hh * HEAD_DIM:(hh + 1) * HEAD_DIM, pl.ds(off, nkeys)]
            vb1 = jnp.concatenate([vb, ones], axis=0)
            acc_new = alpha * carries[hh][1] + jnp.dot(vb1, p, preferred_element_type=F32)
            out.append((m_new, acc_new))
        return tuple(out)

    init = tuple((jnp.full((1, tq), neg_inf, F32), jnp.zeros((HEAD_DIM + ONES_ROWS, tq), F32))
                 for _ in range(heads))
    def tail_update(r):
        nkeys = (r + 1) * MOBA_BLOCK

        def branch(cs):
            tail_scores = scores_of(i - r, nkeys)
            scores_into(sa_ref, 0)
            return update(i - r, tail_scores, cs, True, nkeys)
        return branch

    carries = lax.switch(i % ATTN_UNIT, [tail_update(r) for r in range(ATTN_UNIT)], init)

    n_units = i // ATTN_UNIT
    last_unit = jnp.maximum(n_units - 1, 0)

    def from_ref(ref):
        return [ref[hh] for hh in range(heads)]

    def two_units(k, cs):
        scores_into(sb_ref, 2 * k + 1)
        cs = update(2 * ATTN_UNIT * k, from_ref(sa_ref), cs, False, unit_keys)
        scores_into(sa_ref, jnp.minimum(2 * k + 2, last_unit))
        return update(2 * ATTN_UNIT * k + ATTN_UNIT, from_ref(sb_ref), cs, False, unit_keys)

    carries = lax.fori_loop(0, n_units // 2, two_units, carries)
    carries = lax.cond(
        n_units % 2 == 1,
        lambda cs: update(ATTN_UNIT * last_unit, from_ref(sa_ref), cs, False, unit_keys),
        lambda cs: cs, carries)
    for hh in range(heads):
        acc = carries[hh][1]
        out_t = acc[0:HEAD_DIM, :] / acc[HEAD_DIM:HEAD_DIM + 1, :]
        o_ref[:, hh * HEAD_DIM:(hh + 1) * HEAD_DIM] = out_t.T.astype(o_ref.dtype)


def _attention(q_aug, k_aug, qv_t, n_heads, nb, heads):
    s = k_aug.shape[1]
    groups = n_heads // heads
    once = pl.Buffered(1)
    return pl.pallas_call(
        functools.partial(_attn_kernel, heads=heads),
        out_shape=jax.ShapeDtypeStruct((s, n_heads * HEAD_DIM), BF16),
        grid=(groups, nb),
        in_specs=[pl.BlockSpec((heads, AUG_DIM, MOBA_BLOCK), lambda hg, i: (hg, 0, i)),
                  pl.BlockSpec((heads, s, AUG_DIM), lambda hg, i: (hg, 0, 0), pipeline_mode=once),
                  pl.BlockSpec((heads * HEAD_DIM, s), lambda hg, i: (groups + hg, 0),
                               pipeline_mode=once)],
        out_specs=pl.BlockSpec((MOBA_BLOCK, heads * HEAD_DIM), lambda hg, i: (i, hg)),
        scratch_shapes=[pltpu.VMEM((heads, ATTN_UNIT * MOBA_BLOCK, MOBA_BLOCK), F32),
                        pltpu.VMEM((heads, ATTN_UNIT * MOBA_BLOCK, MOBA_BLOCK), F32)],
        compiler_params=_cparams(("arbitrary", "arbitrary")),
        name="moba_attention",
    )(q_aug, k_aug, qv_t)


def _qv_conv_kernel(wt_ref, x_ref, ga_ref, gb_ref, w_ref, b_ref, g_ref, beta_ref, qv_ref, o_ref,
                    u_ref, ush_ref, *, ts, sub):
    t = pl.program_id(0)

    @pl.when(t == 0)
    def _():
        u_ref[0:CONV_HALO, :] = jnp.zeros((CONV_HALO, u_ref.shape[1]), F32)

    @pl.when(t > 0)
    def _():
        u_ref[0:CONV_HALO, :] = u_ref[ts:ts + CONV_HALO, :]

    x_b = x_ref[...].astype(BF16)
    n_slabs = min(8, ts // sub)
    groups_per_slab = (ts // sub) // n_slabs
    slab = wt_ref.shape[0] // n_slabs

    def project(k, anchor):
        rows = slice(k * slab, (k + 1) * slab)
        w_k = wt_ref[rows, :] if anchor is None else wt_ref[rows, :] + anchor
        qv_ref[rows, :] = lax.dot_general(w_k, x_b, (((1,), (1,)), ((), ())),
                                          preferred_element_type=F32).astype(qv_ref.dtype)

    gb = gb_ref[...].astype(F32)
    u_ref[CONV_HALO:CONV_HALO + ts, :] = ga_ref[...].astype(F32) * jax.nn.sigmoid(gb)
    for b in range(SUBLANES):
        ush_ref[b, 0:ts + CONV_HALO - b, :] = u_ref[b:ts + CONV_HALO, :]
    first = CONV_HALO - (CONV_WIDTH - 1)
    anchor = None
    for r0 in range(0, ts, sub):
        if (r0 // sub) % groups_per_slab == 0:
            project(r0 // sub // groups_per_slab, anchor)
        acc = jnp.broadcast_to(b_ref[...], (sub, u_ref.shape[1]))
        for tap in range(CONV_WIDTH):
            off = first + tap
            aligned = r0 + off - off % SUBLANES
            acc = acc + w_ref[tap:tap + 1, :] * ush_ref[off % SUBLANES, aligned:aligned + sub, :]
        mu = jnp.mean(acc, axis=-1, keepdims=True)
        cen = acc - mu
        var = jnp.mean(cen * cen, axis=-1, keepdims=True)
        y = cen * lax.rsqrt(var + LN_EPS) * g_ref[...] + beta_ref[...]
        o_ref[r0:r0 + sub, :] = (y * jax.nn.sigmoid(y)).astype(o_ref.dtype)
        bits = pltpu.bitcast(y[0:1, 0:1], I32)
        zero = lax.shift_right_logical(lax.shift_right_logical(bits, jnp.int32(31)), jnp.int32(1))
        anchor = zero.astype(BF16)


def _transpose_cast_kernel(w_ref, o_ref):
    o_ref[...] = w_ref[...].T.astype(o_ref.dtype)


def _qv_weights_transposed(w_in, g):
    d = w_in.shape[0]
    cols = min(256, g)
    per_group = g // cols
    return pl.pallas_call(
        _transpose_cast_kernel,
        out_shape=jax.ShapeDtypeStruct((2 * g, d), BF16),
        grid=(2 * per_group,),
        in_specs=[pl.BlockSpec((d, cols), lambda c: (0, c + jnp.where(c >= per_group, per_group, 0)))],
        out_specs=pl.BlockSpec((cols, d), lambda c: (c, 0)),
        compiler_params=_cparams(("arbitrary",)),
        name="qv_weight_transpose",
    )(w_in)


def _qv_projection_and_conv(x2, w_in, kgg, conv_w, conv_b, conv_ln_g, conv_ln_b, g, ts):
    s, d = x2.shape
    row = lambda v: v.reshape(1, g).astype(F32)
    w_qv_t = _qv_weights_transposed(w_in, g)
    once = pl.Buffered(1)
    return pl.pallas_call(
        functools.partial(_qv_conv_kernel, ts=ts, sub=32),
        out_shape=(jax.ShapeDtypeStruct((2 * g, s), BF16), jax.ShapeDtypeStruct((s, g), BF16)),
        grid=(s // ts,),
        in_specs=[pl.BlockSpec((2 * g, d), lambda t: (0, 0), pipeline_mode=once),
                  pl.BlockSpec((ts, d), lambda t: (t, 0)),
                  pl.BlockSpec((ts, g), lambda t: (t, 1)),
                  pl.BlockSpec((ts, g), lambda t: (t, 2)),
                  pl.BlockSpec((CONV_WIDTH, g), lambda t: (0, 0)),
                  pl.BlockSpec((1, g), lambda t: (0, 0)),
                  pl.BlockSpec((1, g), lambda t: (0, 0)),
                  pl.BlockSpec((1, g), lambda t: (0, 0))],
        out_specs=[pl.BlockSpec((2 * g, ts), lambda t: (0, t)),
                   pl.BlockSpec((ts, g), lambda t: (t, 0))],
        scratch_shapes=[pltpu.VMEM((ts + CONV_HALO, g), F32),
                        pltpu.VMEM((SUBLANES, ts + CONV_HALO, g), F32)],
        compiler_params=_cparams(("arbitrary",)),
        name="qv_proj_conformer_conv",
    )(w_qv_t, x2, kgg, kgg, conv_w, row(conv_b), row(conv_ln_g), row(conv_ln_b))


def _layer_norm_rows(z, gain, bias):
    mu = jnp.mean(z, axis=-1, keepdims=True)
    cen = z - mu
    var = jnp.mean(cen * cen, axis=-1, keepdims=True)
    return cen * lax.rsqrt(var + LN_EPS) * gain + bias


HI_HALF = -65536


def _pack_halves(lo, hi):
    lo_bits = pltpu.bitcast(lo.astype(BF16).astype(F32), I32)
    hi_bits = pltpu.bitcast(hi.astype(BF16).astype(F32), I32)
    return lax.shift_right_logical(lo_bits, jnp.int32(16)) | (hi_bits & jnp.int32(HI_HALF))


def _unpack_halves(words):
    lo = pltpu.bitcast(lax.shift_left(words, jnp.int32(16)), F32)
    hi = pltpu.bitcast(words & jnp.int32(HI_HALF), F32)
    return lo, hi


def _outproj_kernel(attn_ref, conv_ref, wo_ref, x_ref, g_ref, b_ref, wr_ref, br_ref,
                    h_ref, hp_ref, lg_ref, *, g):
    tm = x_ref.shape[0]
    sub = min(tm, 256)
    half = x_ref.shape[1] // 2
    n_exp = lg_ref.shape[0]
    wr = wr_ref[...]
    w_hi = wr.astype(BF16)
    w_lo = (wr - w_hi.astype(F32)).astype(BF16)
    mixes = []
    for r0 in range(0, tm, sub):
        mix = jnp.dot(attn_ref[r0:r0 + sub, :], wo_ref[0:g, :], preferred_element_type=F32)
        mixes.append(mix + jnp.dot(conv_ref[r0:r0 + sub, :], wo_ref[g:2 * g, :],
                                   preferred_element_type=F32))
    for k, r0 in enumerate(range(0, tm, sub)):
        h1 = _layer_norm_rows(DEEPNORM_ALPHA * x_ref[r0:r0 + sub, :] + mixes[k],
                              g_ref[...], b_ref[...])
        h_ref[r0:r0 + sub, :] = h1
        hp_ref[r0:r0 + sub, :] = _pack_halves(h1[:, 0:half], h1[:, half:])
        h_hi = h1.astype(BF16)
        h_lo = (h1 - h_hi.astype(F32)).astype(BF16)
        lg = jnp.dot(h_hi, w_hi, preferred_element_type=F32)
        lg = lg + jnp.dot(h_lo, w_hi, preferred_element_type=F32)
        lg = lg + jnp.dot(h_hi, w_lo, preferred_element_type=F32)
        lg_ref[:, r0:r0 + sub] = lg.T[0:n_exp, :] + br_ref[...]


def _out_projection(attn, conv, w_out, x2, ln_g, ln_b, w_router, b_router, tm):
    s, d = x2.shape
    g = attn.shape[1]
    e = w_router.shape[1]
    return pl.pallas_call(
        functools.partial(_outproj_kernel, g=g),
        out_shape=(jax.ShapeDtypeStruct((s, d), F32), jax.ShapeDtypeStruct((s, d // 2), I32),
                   jax.ShapeDtypeStruct((e, s), F32)),
        grid=(s // tm,),
        in_specs=[pl.BlockSpec((tm, g), lambda m: (m, 0)),
                  pl.BlockSpec((tm, g), lambda m: (m, 0)),
                  pl.BlockSpec((2 * g, d), lambda m: (0, 0), pipeline_mode=pl.Buffered(1)),
                  pl.BlockSpec((tm, d), lambda m: (m, 0)),
                  pl.BlockSpec((1, d), lambda m: (0, 0)),
                  pl.BlockSpec((1, d), lambda m: (0, 0)),
                  pl.BlockSpec((d, LANES), lambda m: (0, 0)),
                  pl.BlockSpec((e, 1), lambda m: (0, 0))],
        out_specs=[pl.BlockSpec((tm, d), lambda m: (m, 0)),
                   pl.BlockSpec((tm, d // 2), lambda m: (m, 0)),
                   pl.BlockSpec((e, tm), lambda m: (0, m))],
        compiler_params=_cparams(("arbitrary",)),
        name="outproj_ln1_router",
    )(attn, conv, w_out.astype(BF16), x2, ln_g.reshape(1, d), ln_b.reshape(1, d),
      jnp.pad(w_router, ((0, 0), (0, LANES - e))), b_router.reshape(e, 1))


def _route_kernel(lg_ref, e_ref, gate_ref, rank_ref, cnt_ref, carry_ref, *, n_exp, tr):
    t = pl.program_id(0)

    @pl.when(t == 0)
    def _():
        carry_ref[...] = jnp.zeros_like(carry_ref)

    neg_inf = jnp.float32(-jnp.inf)
    cur = lg_ref[...]
    j = lax.broadcasted_iota(I32, (n_exp, tr), 0)
    vals, picks = [], []
    for r in range(TOP_K):
        m = jnp.max(cur, axis=0, keepdims=True)
        idx = jnp.min(jnp.where(cur == m, j, n_exp), axis=0, keepdims=True)
        pick = j == idx
        e_ref[r:r + 1, :] = idx
        vals.append(m)
        picks.append(pick)
        cur = jnp.where(pick, neg_inf, cur)
    exps = [jnp.exp(v - vals[0]) for v in vals]
    den = exps[0]
    for r in range(1, TOP_K):
        den = den + exps[r]
    for r in range(TOP_K):
        gate_ref[r:r + 1, :] = exps[r] / den

    chosen = picks[0].astype(F32)
    for r in range(1, TOP_K):
        chosen = chosen + picks[r].astype(F32)
    a = lax.broadcasted_iota(I32, (tr, tr), 0)
    b = lax.broadcasted_iota(I32, (tr, tr), 1)
    upper = (a < b).astype(BF16)
    excl = jnp.dot(chosen.astype(BF16), upper, preferred_element_type=F32)
    base = carry_ref[:, 0:1]
    rank = excl + base
    for r in range(TOP_K):
        rank_ref[r:r + 1, :] = jnp.sum(jnp.where(picks[r], rank, 0.0), axis=0,
                                       keepdims=True).astype(I32)
    total = base + jnp.sum(chosen, axis=1, keepdims=True)
    carry_ref[...] = jnp.broadcast_to(total, carry_ref.shape)
    cnt_ref[...] = jnp.broadcast_to(total, cnt_ref.shape).astype(I32)


def _route(logits_t, tr):
    n_exp, s = logits_t.shape
    return pl.pallas_call(
        functools.partial(_route_kernel, n_exp=n_exp, tr=tr),
        out_shape=(jax.ShapeDtypeStruct((TOP_K, s), I32), jax.ShapeDtypeStruct((TOP_K, s), F32),
                   jax.ShapeDtypeStruct((TOP_K, s), I32), jax.ShapeDtypeStruct((n_exp, 128), I32)),
        grid=(s // tr,),
        in_specs=[pl.BlockSpec((n_exp, tr), lambda t: (0, t))],
        out_specs=[pl.BlockSpec((TOP_K, tr), lambda t: (0, t)),
                   pl.BlockSpec((TOP_K, tr), lambda t: (0, t)),
                   pl.BlockSpec((TOP_K, tr), lambda t: (0, t)),
                   pl.BlockSpec((n_exp, 128), lambda t: (0, 0))],
        scratch_shapes=[pltpu.VMEM((n_exp, 128), F32)],
        compiler_params=_cparams(("arbitrary",)),
        name="moe_route",
    )(logits_t)


SC_CORES = 2
SC_SUBCORES = 16
SC_LANES = 16
SC_GATHER_ROWS = 32
SC_SCAN = 8192


_SC_PARAMS = pltpu.CompilerParams(needs_layout_passes=False)


def _sc_mesh():
    return plsc.VectorSubcoreMesh(core_axis_name="c", subcore_axis_name="s",
                                  num_cores=SC_CORES, num_subcores=SC_SUBCORES)


def _sc_worker():
    return lax.axis_index("s") * SC_CORES + lax.axis_index("c")


def _sc_gather_loop(table_hbm, idx_all, out_hbm, bufs, sems, base, n_chunks):
    chunk = SC_GATHER_ROWS

    def gather(ci, b):
        return pltpu.make_async_copy(
            table_hbm.at[idx_all.at[pl.ds(ci * chunk, chunk)]], bufs[b], sems[b])

    gather(0, 0).start()

    @pl.loop(0, n_chunks, step=2)
    def _(ci):
        gather(ci + 1, 1).start()
        gather(ci, 0).wait()
        pltpu.sync_copy(bufs[0], out_hbm.at[pl.ds(base + ci * chunk, chunk)])

        @pl.when(ci + 2 < n_chunks)
        def _():
            gather(ci + 2, 0).start()

        gather(ci + 1, 1).wait()
        pltpu.sync_copy(bufs[1], out_hbm.at[pl.ds(base + (ci + 1) * chunk, chunk)])


def _sc_dest(ps_v, e_buf, r_buf, v):
    e_vec = e_buf[pl.ds(v * SC_LANES, SC_LANES)]
    return plsc.load_gather(ps_v, [e_vec]) + r_buf[pl.ds(v * SC_LANES, SC_LANES)]


def _sc_dispatch(table, e_flat, rank_flat, pad_start, cap, s):
    n_assign, width = e_flat.shape[0], table.shape[1]
    n_workers = SC_CORES * SC_SUBCORES
    per_worker = cap // n_workers
    chunk = SC_GATHER_ROWS
    assert cap % (n_workers * 2 * chunk) == 0 and n_assign % SC_SCAN == 0
    assert per_worker % SC_LANES == 0 and s % SC_SCAN == 0

    @functools.partial(
        pl.kernel, mesh=_sc_mesh(), out_type=jax.ShapeDtypeStruct((cap, width), table.dtype),
        scratch_types=[pltpu.VMEM((per_worker,), I32), pltpu.VMEM((pad_start.shape[0],), I32),
                       pltpu.VMEM((SC_SCAN,), I32), pltpu.VMEM((SC_SCAN,), I32),
                       pltpu.VMEM((chunk, width), table.dtype),
                       pltpu.VMEM((chunk, width), table.dtype),
                       pltpu.SemaphoreType.DMA, pltpu.SemaphoreType.DMA],
        compiler_params=_SC_PARAMS, name="sc_dispatch")
    def dispatch(table_hbm, e_hbm, r_hbm, ps_hbm, out_hbm, idx_all, ps_v, e_buf, r_buf,
                 buf0, buf1, sem0, sem1):
        base = _sc_worker() * per_worker
        lane = lax.iota(I32, SC_LANES)
        pltpu.sync_copy(ps_hbm, ps_v)

        @pl.loop(0, per_worker // SC_LANES)
        def _(k):
            idx_all[pl.ds(k * SC_LANES, SC_LANES)] = lax.rem(base + k * SC_LANES + lane, s)

        @pl.loop(0, n_assign // SC_SCAN)
        def _(c):
            pltpu.sync_copy(e_hbm.at[pl.ds(c * SC_SCAN, SC_SCAN)], e_buf)
            pltpu.sync_copy(r_hbm.at[pl.ds(c * SC_SCAN, SC_SCAN)], r_buf)
            tok0 = lax.rem(c * SC_SCAN, s)

            @pl.loop(0, SC_SCAN // SC_LANES)
            def _(v):
                loc = _sc_dest(ps_v, e_buf, r_buf, v) - base
                mine = jnp.logical_and(loc >= 0, loc < per_worker)
                tok = tok0 + v * SC_LANES + lane
                plsc.store_scatter(idx_all, [jnp.where(mine, loc, 0)], tok, mask=mine)

        _sc_gather_loop(table_hbm, idx_all, out_hbm, (buf0, buf1), (sem0, sem1), base,
                        per_worker // chunk)

    return dispatch(table, e_flat, rank_flat, pad_start)


def _sc_combine_gather(y, e_flat, rank_flat, pad_start):
    n_assign, width = e_flat.shape[0], y.shape[1]
    n_workers = SC_CORES * SC_SUBCORES
    per_worker = n_assign // n_workers
    chunk = SC_GATHER_ROWS
    assert n_assign % (n_workers * 2 * chunk) == 0

    @functools.partial(
        pl.kernel, mesh=_sc_mesh(), out_type=jax.ShapeDtypeStruct((n_assign, width), y.dtype),
        scratch_types=[pltpu.VMEM((per_worker,), I32), pltpu.VMEM((pad_start.shape[0],), I32),
                       pltpu.VMEM((per_worker,), I32), pltpu.VMEM((per_worker,), I32),
                       pltpu.VMEM((chunk, width), y.dtype), pltpu.VMEM((chunk, width), y.dtype),
                       pltpu.SemaphoreType.DMA, pltpu.SemaphoreType.DMA],
        compiler_params=_SC_PARAMS, name="sc_combine_gather")
    def combine(y_hbm, e_hbm, r_hbm, ps_hbm, out_hbm, idx_all, ps_v, e_buf, r_buf,
                buf0, buf1, sem0, sem1):
        base = _sc_worker() * per_worker
        pltpu.sync_copy(ps_hbm, ps_v)
        pltpu.sync_copy(e_hbm.at[pl.ds(base, per_worker)], e_buf)
        pltpu.sync_copy(r_hbm.at[pl.ds(base, per_worker)], r_buf)

        @pl.loop(0, per_worker // SC_LANES)
        def _(v):
            idx_all[pl.ds(v * SC_LANES, SC_LANES)] = _sc_dest(ps_v, e_buf, r_buf, v)

        _sc_gather_loop(y_hbm, idx_all, out_hbm, (buf0, buf1), (sem0, sem1), base,
                        per_worker // chunk)

    return combine(y, e_flat, rank_flat, pad_start)


def _expert_kernel(item_e, item_start, item_rows, xs_ref, wg_ref, wu_ref, wd_ref,
                   bg_ref, bu_ref, bd_ref, y_ref, xin_ref, acc_ref,
                   sem_x, sem_y, *, nf, n_items, big):
    w = pl.program_id(0)
    f = pl.program_id(1)
    rows = item_rows[w]
    slot = w % 2
    half = xin_ref.shape[2]
    has_next = jnp.logical_and(w + 1 < n_items, item_rows[jnp.minimum(w + 1, n_items - 1)] > 0)

    def x_copy(item, sl, c):
        src = pl.multiple_of(item_start[item] + c * ROUTE_PAD, ROUTE_PAD)
        dst = pl.multiple_of(c * ROUTE_PAD, ROUTE_PAD)
        return pltpu.make_async_copy(xs_ref.at[pl.ds(src, ROUTE_PAD)],
                                     xin_ref.at[sl, pl.ds(dst, ROUTE_PAD)], sem_x.at[sl])

    def y_copy(item, sl, c):
        src = pl.multiple_of(c * ROUTE_PAD, ROUTE_PAD)
        dst = pl.multiple_of(item_start[item] + c * ROUTE_PAD, ROUTE_PAD)
        return pltpu.make_async_copy(xin_ref.at[sl, pl.ds(src, ROUTE_PAD)],
                                     y_ref.at[pl.ds(dst, ROUTE_PAD)], sem_y)

    def for_chunks(item, fn):
        def body(c, carry):
            fn(c)
            return carry
        lax.fori_loop(0, item_rows[item] // ROUTE_PAD, body, 0)

    def compute(r0, size, first):
        lo, hi = _unpack_halves(xin_ref[slot, pl.ds(r0, size), :])
        lo = lo.astype(BF16)
        hi = hi.astype(BF16)

        def x_dot(w_ref):
            return (jnp.dot(lo, w_ref[0, 0:half, :].astype(BF16), preferred_element_type=F32)
                    + jnp.dot(hi, w_ref[0, half:2 * half, :].astype(BF16),
                              preferred_element_type=F32))

        gate = x_dot(wg_ref) + bg_ref[0]
        up = x_dot(wu_ref) + bu_ref[0]
        gate = jnp.minimum(gate, SWIGLU_LIMIT)
        up = jnp.clip(up, -SWIGLU_LIMIT, SWIGLU_LIMIT)
        act = (up + 1.0) * (gate * jax.nn.sigmoid(SWIGLU_ALPHA * gate))
        part = jnp.dot(act.astype(BF16), wd_ref[0].astype(BF16), preferred_element_type=F32)
        if first:
            acc_ref[pl.ds(r0, size), :] = part + bd_ref[0]
        else:
            acc_ref[pl.ds(r0, size), :] += part

    def compute_all(first):
        n_big = rows // big
        rem = rows - n_big * big
        merge = jnp.logical_and(n_big >= 1, jnp.logical_and(rem > 0, rem <= big // 2))
        n_loop = n_big - jnp.where(merge, 1, 0)

        def body(c, carry):
            compute(pl.multiple_of(c * big, big), big, first)
            return carry
        lax.fori_loop(0, n_loop, body, 0)
        for extra in range(ROUTE_PAD, big // 2 + 1, ROUTE_PAD):
            @pl.when(jnp.logical_and(merge, rem == extra))
            def _(extra=extra):
                compute(pl.multiple_of(n_loop * big, big), big + extra, first)

        @pl.when(jnp.logical_not(merge))
        def _():
            done = n_big * big
            size = big // 2
            while size >= ROUTE_PAD:
                take = ((rows - done) & size) != 0
                here = done

                @pl.when(take)
                def _(here=here, size=size):
                    compute(pl.multiple_of(here, ROUTE_PAD), size, first)
                done = done + jnp.where(take, size, 0)
                size //= 2

    @pl.when(rows > 0)
    def _():
        @pl.when(f == 0)
        def _():
            @pl.when(w == 0)
            def _():
                for_chunks(w, lambda c: x_copy(w, slot, c).start())

            for_chunks(w, lambda c: x_copy(w, slot, c).wait())

            @pl.when(w > 0)
            def _():
                for_chunks(w - 1, lambda c: y_copy(w - 1, 1 - slot, c).wait())

            @pl.when(has_next)
            def _():
                for_chunks(w + 1, lambda c: x_copy(w + 1, 1 - slot, c).start(priority=1))

            compute_all(True)

        @pl.when(f > 0)
        def _():
            compute_all(False)

        @pl.when(f == nf - 1)
        def _():
            def pack(c):
                r0 = pl.multiple_of(c * ROUTE_PAD, ROUTE_PAD)
                xin_ref[slot, pl.ds(r0, ROUTE_PAD), :] = _pack_halves(
                    acc_ref[pl.ds(r0, ROUTE_PAD), 0:half],
                    acc_ref[pl.ds(r0, ROUTE_PAD), half:2 * half])
                y_copy(w, slot, c).start(priority=1)
            for_chunks(w, pack)

            @pl.when(jnp.logical_not(has_next))
            def _():
                for_chunks(w, lambda c: y_copy(w, slot, c).wait())


def _experts(item_e, item_start, item_rows, n_active, xs, w_gate_up, b_gate_up, w_down, b_down,
             tm, tf):
    cap, half = xs.shape
    d = 2 * half
    n_exp, _, two_f = w_gate_up.shape
    ff = two_f // 2
    nf = ff // tf
    n_items = item_e.shape[0]
    big = min(512, tm)
    assert big % ROUTE_PAD == 0 and (big & (big - 1)) == 0

    def fidx(w, f, rows):
        return jnp.where(rows[w] > 0, f, nf - 1)

    return pl.pallas_call(
        functools.partial(_expert_kernel, nf=nf, n_items=n_items, big=big),
        out_shape=jax.ShapeDtypeStruct((cap, half), I32),
        grid_spec=pltpu.PrefetchScalarGridSpec(
            num_scalar_prefetch=3,
            grid=(n_active, nf),
            in_specs=[
                pl.BlockSpec(memory_space=pl.ANY),
                pl.BlockSpec((1, d, tf), lambda w, f, ie, ist, ir: (ie[w], 0, fidx(w, f, ir))),
                pl.BlockSpec((1, d, tf), lambda w, f, ie, ist, ir: (ie[w], 0, nf + fidx(w, f, ir))),
                pl.BlockSpec((1, tf, d), lambda w, f, ie, ist, ir: (ie[w], fidx(w, f, ir), 0)),
                pl.BlockSpec((1, 1, tf), lambda w, f, ie, ist, ir: (ie[w], 0, fidx(w, f, ir))),
                pl.BlockSpec((1, 1, tf), lambda w, f, ie, ist, ir: (ie[w], 0, nf + fidx(w, f, ir))),
                pl.BlockSpec((1, 1, d), lambda w, f, ie, ist, ir: (ie[w], 0, 0)),
            ],
            out_specs=pl.BlockSpec(memory_space=pl.ANY),
            scratch_shapes=[pltpu.VMEM((2, tm, half), I32), pltpu.VMEM((tm, d), F32),
                            pltpu.SemaphoreType.DMA((2,)), pltpu.SemaphoreType.DMA]),
        compiler_params=_cparams(("arbitrary", "arbitrary")),
        name="moe_experts",
    )(item_e, item_start, item_rows, xs, w_gate_up, w_gate_up, w_down,
      b_gate_up.reshape(n_exp, 1, two_f), b_gate_up.reshape(n_exp, 1, two_f),
      b_down.reshape(n_exp, 1, d))


def _combine_kernel(yg_ref, gate_ref, h_ref, g_ref, b_ref, o_ref):
    gates = gate_ref[...]
    half = yg_ref.shape[2]
    z_lo = DEEPNORM_ALPHA * h_ref[:, 0:half]
    z_hi = DEEPNORM_ALPHA * h_ref[:, half:2 * half]
    for r in range(TOP_K):
        lo, hi = _unpack_halves(yg_ref[r])
        z_lo = z_lo + gates[:, r:r + 1] * lo
        z_hi = z_hi + gates[:, r:r + 1] * hi
    inv_d = 1.0 / (2 * half)
    mu = (jnp.sum(z_lo, axis=-1, keepdims=True) + jnp.sum(z_hi, axis=-1, keepdims=True)) * inv_d
    c_lo = z_lo - mu
    c_hi = z_hi - mu
    var = (jnp.sum(c_lo * c_lo, axis=-1, keepdims=True)
           + jnp.sum(c_hi * c_hi, axis=-1, keepdims=True)) * inv_d
    rstd = lax.rsqrt(var + LN_EPS)
    o_ref[:, 0:half] = c_lo * rstd * g_ref[:, 0:half] + b_ref[:, 0:half]
    o_ref[:, half:2 * half] = c_hi * rstd * g_ref[:, half:2 * half] + b_ref[:, half:2 * half]


def _combine_part_kernel(yg_ref, gate_ref, h_ref, g_ref, b_ref, prev_ref, o_ref):
    del prev_ref
    _combine_kernel(yg_ref, gate_ref, h_ref, g_ref, b_ref, o_ref)


def _combine(yg, gates, h1, ln_g, ln_b, tc, first_tile, out_prev):
    s, d = h1.shape
    n_tiles = yg.shape[1] // tc
    in_specs = [pl.BlockSpec((TOP_K, tc, d // 2), lambda t: (0, t, 0)),
                pl.BlockSpec((tc, TOP_K), lambda t: (t + first_tile, 0)),
                pl.BlockSpec((tc, d), lambda t: (t + first_tile, 0)),
                pl.BlockSpec((1, d), lambda t: (0, 0)),
                pl.BlockSpec((1, d), lambda t: (0, 0))]
    args = [yg, gates, h1, ln_g.reshape(1, d), ln_b.reshape(1, d)]
    if out_prev is None:
        body, aliases = _combine_kernel, {}
    else:
        body, aliases = _combine_part_kernel, {len(args): 0}
        in_specs.append(pl.BlockSpec(memory_space=pl.ANY))
        args.append(out_prev)
    return pl.pallas_call(
        body,
        out_shape=jax.ShapeDtypeStruct((s, d), F32),
        grid=(n_tiles,),
        in_specs=in_specs,
        out_specs=pl.BlockSpec((tc, d), lambda t: (t + first_tile, 0)),
        input_output_aliases=aliases,
        compiler_params=_cparams(("arbitrary",)),
        name="moe_combine_ln2",
    )(*args)


def _tiles(s, d, ff, n_heads):
    return dict(
        proj_tm=min(1024, s), select_tq=min(1024, s), conv_ts=min(512, s), outproj_tm=min(512, s),
        attn_heads=min(4, n_heads),
        route_tr=min(512, s), combine_tc=min(512, s),
        expert_tm=min(1280, max(ROUTE_PAD, (s * TOP_K // 16) // ROUTE_PAD * ROUTE_PAD)),
        expert_tf=min(512, ff))


def kernel(x, w_in, conv_w, conv_b, conv_ln_g, conv_ln_b, w_out, ln1_g, ln1_b,
           w_router, b_router, w_gate_up, b_gate_up, w_down, b_down, ln2_g, ln2_b):
    b, s, d = x.shape
    assert b == 1, "the kernels treat the sequence of the single batch element as the row axis"
    g = w_in.shape[1] // 5
    n_heads = g // HEAD_DIM
    nb = s // MOBA_BLOCK
    n_exp = w_router.shape[1]
    ff = w_down.shape[1]
    assert s % MOBA_BLOCK == 0 and nb % 8 == 0 and HEAD_DIM + nb + 3 <= AUG_DIM
    tl = _tiles(s, d, ff, n_heads)
    x2 = x.reshape(s, d)

    kgg = _in_projection(x2, w_in, g, tl["proj_tm"])
    qv_t, conv = _qv_projection_and_conv(x2, w_in, kgg, conv_w, conv_b, conv_ln_g, conv_ln_b, g,
                                         tl["conv_ts"])
    kmean = _block_means(kgg, g, nb)
    slopes = 2.0 ** (-(8.0 / n_heads) * jnp.arange(1, n_heads + 1, dtype=F32))
    q_aug, k_aug = _select(slopes, qv_t, kmean, kgg, n_heads, nb, tl["select_tq"])
    attn = _attention(q_aug, k_aug, qv_t, n_heads, nb, tl["attn_heads"])
    h1, h1_packed, logits_t = _out_projection(attn, conv, w_out, x2, ln1_g, ln1_b,
                                              w_router, b_router, tl["outproj_tm"])

    e_t, gate_t, rank_t, counts = _route(logits_t, tl["route_tr"])
    counts = counts[:, 0]
    padded = (counts + ROUTE_PAD - 1) // ROUTE_PAD * ROUTE_PAD
    pad_end = jnp.cumsum(padded)
    pad_start = (pad_end - padded).astype(I32)
    cap = s * TOP_K + n_exp * ROUTE_PAD
    e_flat = e_t.reshape(-1)
    rank_flat = rank_t.reshape(-1)
    xs = _sc_dispatch(h1_packed, e_flat, rank_flat, pad_start, cap, s)

    tm = tl["expert_tm"]
    n_items = cap // tm + n_exp
    per_e = (padded + tm - 1) // tm
    item_end = jnp.cumsum(per_e)
    item_ids = jnp.arange(n_items, dtype=I32)
    item_e = jnp.minimum(jnp.sum(item_ids[:, None] >= item_end[None, :], axis=1),
                         n_exp - 1).astype(I32)
    piece = item_ids - (item_end - per_e)[item_e]
    active = item_ids < item_end[-1]
    item_start = jnp.where(active, pad_start[item_e] + piece * tm, 0).astype(I32)
    item_rows = jnp.where(active, jnp.minimum(tm, padded[item_e] - piece * tm), 0).astype(I32)
    last_e = item_e[jnp.maximum(item_end[-1] - 1, 0)]
    item_e = jnp.where(active, item_e, last_e).astype(I32)

    y = _experts(item_e, item_start, item_rows, item_end[-1].astype(I32), xs,
                 w_gate_up, b_gate_up, w_down, b_down, tm, tl["expert_tf"])
    tc = tl["combine_tc"]
    s_part = s // COMBINE_PARTS
    gates = gate_t.T
    out = None
    for part in range(COMBINE_PARTS):
        cols = slice(part * s_part, (part + 1) * s_part)
        yg = _sc_combine_gather(y, e_t[:, cols].reshape(-1), rank_t[:, cols].reshape(-1), pad_start)
        out = _combine(yg.reshape(TOP_K, s_part, d // 2), gates, h1, ln2_g, ln2_b, tc,
                       part * (s_part // tc), out)
    return out.reshape(b, s, d)
```
